```python
import math
import jax, jax.numpy as jnp
from jax import lax
import numpy as np

D_MODEL = 2048
BATCH = 2
SEQ = 16384
DEPTH = 1
DEC_BATCH = 8
DEC_SEQ = 4096
PAST_LEN = 128

POOL_WIDTH = 1024
POOL_WINDOWS = (2, 4, 8, 16)
N_POOL_GROUPS = len(POOL_WINDOWS)
POOL_GROUP = POOL_WIDTH // N_POOL_GROUPS
N_HEADS = 8
HEAD_DIM = 64
V_DIM = 2 * HEAD_DIM
QK_WIDTH = N_HEADS * 2 * HEAD_DIM
ATTN_WIDTH = N_HEADS * V_DIM
Q_BLOCK = 128
NUM_BUCKETS = 32
MAX_DISTANCE = 128
N_BRANCHES = 2
IN_WIDTH = POOL_WIDTH + 2 * QK_WIDTH + ATTN_WIDTH + N_BRANCHES * D_MODEL
N_EXPERTS = 16
CAPACITY_FACTOR = 2
D_FF = 2048
EPS = 1e-6

kernel_name = "hybrid_pool_diffattn_ec_moe_encoder"


def rms_norm(x, g):
    xf = x.astype(jnp.float32)
    y = xf * lax.rsqrt(jnp.mean(xf * xf, axis=-1, keepdims=True) + EPS)
    return (y * g.astype(jnp.float32)).astype(x.dtype)


def t5_bucket(rel):
    half = NUM_BUCKETS // 2
    max_exact = half // 2
    ret = jnp.where(rel > 0, half, 0)
    n = jnp.abs(rel)
    nf = jnp.maximum(n, 1).astype(jnp.float32)
    large = max_exact + (jnp.log(nf / max_exact) / math.log(MAX_DISTANCE / max_exact)
                         * (half - max_exact)).astype(jnp.int32)
    large = jnp.minimum(large, half - 1)
    return ret + jnp.where(n < max_exact, n, large)


def centred_mean_minus_self(u, w):
    B, S, C = u.shape
    c = jnp.concatenate([jnp.zeros((B, 1, C), u.dtype), jnp.cumsum(u, axis=1)], axis=1)
    t = jnp.arange(S)
    lo = jnp.clip(t - w // 2, 0, S)
    hi = jnp.clip(t + (w - w // 2), 0, S)
    cnt = (hi - lo).astype(jnp.float32)
    return (c[:, hi] - c[:, lo]) / cnt[None, :, None] - u


def pool_mixer(u, pool_w, pool_scale):
    B, S, _ = u.shape
    ug = u.astype(jnp.float32).reshape(B, S, N_POOL_GROUPS, POOL_GROUP)
    pooled = [centred_mean_minus_self(ug[:, :, i], POOL_WINDOWS[i]) for i in range(N_POOL_GROUPS)]
    p = jnp.stack(pooled, axis=2).astype(u.dtype)
    p = jnp.einsum('bsgc,gcd->bsgd', p, pool_w)
    return p.reshape(B, S, POOL_WIDTH) * pool_scale


def diff_attention(q, k, v, rel_bias, lam, q_norm_g, k_norm_g):
    B, S = q.shape[0], q.shape[1]
    q = rms_norm(q, q_norm_g) * (HEAD_DIM ** -0.5)
    k = rms_norm(k, k_norm_g)
    nblk = S // Q_BLOCK
    qb = q.reshape(B, nblk, Q_BLOCK, N_HEADS, 2, HEAD_DIM).transpose(1, 0, 2, 3, 4, 5)
    starts = jnp.arange(nblk) * Q_BLOCK
    k_pos = jnp.arange(S)

    def block(args):
        qi, st = args
        rel = k_pos[None, :] - (st + jnp.arange(Q_BLOCK))[:, None]
        bias = jnp.transpose(rel_bias[t5_bucket(rel)], (2, 0, 1)).astype(jnp.float32)
        logits = jnp.einsum('bqhcd,bkhcd->bhcqk', qi, k).astype(jnp.float32) + bias[None, :, None]
        p = jax.nn.softmax(logits, axis=-1)
        a = p[:, :, 0] - lam * p[:, :, 1]
        return jnp.einsum('bhqk,bkhe->bqhe', a.astype(v.dtype), v)

    o = lax.map(block, (qb, starts))
    return o.transpose(1, 0, 2, 3, 4).reshape(B, S, N_HEADS, V_DIM)


def mixer_sublayer(x, layer, norm1_g, w_in, pool_w, pool_scale, w_pool_out, q_norm_g, k_norm_g,
                   rel_bias, lambda_qk, subln_g, w_attn_out, w_o):
    B, S, D = x.shape
    xn = rms_norm(x, norm1_g)
    z = xn @ w_in
    o1 = POOL_WIDTH
    o2 = o1 + QK_WIDTH
    o3 = o2 + QK_WIDTH
    o4 = o3 + ATTN_WIDTH
    u, q, k, v, g = z[..., :o1], z[..., o1:o2], z[..., o2:o3], z[..., o3:o4], z[..., o4:]
    pool_out = pool_mixer(u, pool_w, pool_scale) @ w_pool_out
    lam_init = 0.8 - 0.6 * math.exp(-0.3 * layer)
    lqk = lambda_qk.astype(jnp.float32)
    lam = jnp.exp(jnp.sum(lqk[0] * lqk[1])) - jnp.exp(jnp.sum(lqk[2] * lqk[3])) + lam_init
    att = diff_attention(q.reshape(B, S, N_HEADS, 2, HEAD_DIM), k.reshape(B, S, N_HEADS, 2, HEAD_DIM),
                         v.reshape(B, S, N_HEADS, V_DIM), rel_bias, lam, q_norm_g, k_norm_g)
    att = (rms_norm(att, subln_g) * (1.0 - lam_init)).reshape(B, S, ATTN_WIDTH)
    attn_out = att @ w_attn_out
    gates = jax.nn.sigmoid(g.astype(jnp.float32)).reshape(B, S, N_BRANCHES, D)
    merged = (gates[:, :, 0] * pool_out + gates[:, :, 1] * attn_out).astype(x.dtype)
    return x + merged @ w_o


def expert_choice_moe(h, norm2_g, w_router, w_gate, w_up, w_down):
    B, S, D = h.shape
    n_tok = B * S
    cap = max(1, (CAPACITY_FACTOR * n_tok) // N_EXPERTS)
    t = rms_norm(h, norm2_g).reshape(n_tok, D)
    aff = jax.nn.softmax((t @ w_router).astype(jnp.float32), axis=-1)
    gate, idx = lax.top_k(aff.T, cap)
    xe = t[idx]
    hid = jax.nn.silu(jnp.einsum('ecd,edf->ecf', xe, w_gate)) * jnp.einsum('ecd,edf->ecf', xe, w_up)
    ye = jnp.einsum('ecf,efd->ecd', hid, w_down) * gate[..., None].astype(h.dtype)
    out = jnp.zeros((n_tok, D), h.dtype).at[idx.reshape(-1)].add(ye.reshape(-1, D))
    return h + out.reshape(B, S, D)


def setup_inputs(seed: int = 0) -> dict:
    key = jax.random.key(seed)
    ks = jax.random.split(key, 20)
    f32 = jnp.float32
    nrm = lambda k, shape, s: jax.random.normal(k, shape, f32) * s
    return {
        "x_prompt": nrm(ks[0], (BATCH, SEQ, D_MODEL), 1.0),
        "x_sample": nrm(ks[1], (DEC_BATCH, DEC_SEQ, D_MODEL), 1.0),
        "norm1_g": 1.0 + nrm(ks[2], (DEPTH, D_MODEL), 0.02),
        "w_in": nrm(ks[3], (DEPTH, D_MODEL, IN_WIDTH), D_MODEL ** -0.5),
        "pool_w": nrm(ks[4], (DEPTH, N_POOL_GROUPS, POOL_GROUP, POOL_GROUP), POOL_GROUP ** -0.5),
        "pool_scale": 1.0 + nrm(ks[5], (DEPTH, POOL_WIDTH), 0.1),
        "w_pool_out": nrm(ks[6], (DEPTH, POOL_WIDTH, D_MODEL), POOL_WIDTH ** -0.5),
        "q_norm_g": 1.0 + nrm(ks[7], (DEPTH, HEAD_DIM), 0.02),
        "k_norm_g": 1.0 + nrm(ks[8], (DEPTH, HEAD_DIM), 0.02),
        "rel_bias": nrm(ks[9], (NUM_BUCKETS, N_HEADS), 0.5),
        "lambda_qk": nrm(ks[10], (DEPTH, 4, HEAD_DIM), 0.1),
        "subln_g": 1.0 + nrm(ks[11], (DEPTH, V_DIM), 0.02),
        "w_attn_out": nrm(ks[12], (DEPTH, ATTN_WIDTH, D_MODEL), ATTN_WIDTH ** -0.5),
        "w_o": nrm(ks[13], (DEPTH, D_MODEL, D_MODEL), D_MODEL ** -0.5),
        "norm2_g": 1.0 + nrm(ks[14], (DEPTH, D_MODEL), 0.02),
        "w_router": nrm(ks[15], (DEPTH, D_MODEL, N_EXPERTS), D_MODEL ** -0.5),
        "w_gate": nrm(ks[16], (DEPTH, N_EXPERTS, D_MODEL, D_FF), D_MODEL ** -0.5),
        "w_up": nrm(ks[17], (DEPTH, N_EXPERTS, D_MODEL, D_FF), D_MODEL ** -0.5),
        "w_down": nrm(ks[18], (DEPTH, N_EXPERTS, D_FF, D_MODEL), D_FF ** -0.5),
    }


def reference(x_prompt, x_sample, norm1_g, w_in, pool_w, pool_scale, w_pool_out, q_norm_g, k_norm_g,
              rel_bias, lambda_qk, subln_g, w_attn_out, w_o, norm2_g, w_router, w_gate, w_up, w_down):
    hp = x_prompt
    hs = x_sample
    for l in range(DEPTH):
        mix = lambda x: mixer_sublayer(x, l, norm1_g[l], w_in[l], pool_w[l], pool_scale[l], w_pool_out[l],
                                       q_norm_g[l], k_norm_g[l], rel_bias, lambda_qk[l], subln_g[l],
                                       w_attn_out[l], w_o[l])
        ffn = lambda x: expert_choice_moe(x, norm2_g[l], w_router[l], w_gate[l], w_up[l], w_down[l])
        hp = ffn(mix(hp))
        hs = ffn(mix(hs))
    return (hp, hs)
```

```python
import functools
import math

import jax
import jax.numpy as jnp
from jax import lax
from jax.experimental import pallas as pl
from jax.experimental.pallas import tpu as pltpu

F32, BF16, I32 = jnp.float32, jnp.bfloat16, jnp.int32

EPS = 1e-6
POOL_WINDOWS = (2, 4, 8, 16)
MAX_DISTANCE = 128
CAPACITY_FACTOR = 2

V7X_VMEM_BYTES = 64 * 1024 * 1024
VMEM_LIMIT_BYTES = V7X_VMEM_BYTES - 8 * 1024 * 1024
LANES = 128
MXU_DIM = 256

TM_NORM = 512
TM_INPROJ = 512
TN_INPROJ = 1024
TS_POOL = 512
POOL_PAD = 128
T_ATTN = 512
TM_MERGE = 256
TM_OPROJ = 256
ROUTE_CHUNK = 256
R_GATHER = 256
TM_FFN = 1024
TF_FFN = 512


def _cparams(*sem):
    return pltpu.CompilerParams(dimension_semantics=sem, vmem_limit_bytes=VMEM_LIMIT_BYTES)


def _split_bf16(a):
    hi = a.astype(BF16)
    lo = (a - hi.astype(F32)).astype(BF16)
    return hi, lo


def _rmsnorm_kernel(x_ref, g_ref, o_ref):
    x = x_ref[...]
    ms = jnp.mean(x * x, axis=-1, keepdims=True)
    o_ref[...] = (x * lax.rsqrt(ms + EPS) * g_ref[...]).astype(o_ref.dtype)


def _rmsnorm(x2d, g):
    n, d = x2d.shape
    tm = min(TM_NORM, n)
    return pl.pallas_call(
        _rmsnorm_kernel,
        grid=(n // tm,),
        in_specs=[pl.BlockSpec((tm, d), lambda i: (i, 0)), pl.BlockSpec((1, d), lambda i: (0, 0))],
        out_specs=pl.BlockSpec((tm, d), lambda i: (i, 0)),
        out_shape=jax.ShapeDtypeStruct((n, d), BF16),
        compiler_params=_cparams("parallel"),
    )(x2d, g.reshape(1, d))


def _inproj_kernel(x_ref, w_ref, gain_ref, bd_ref, u_ref, qk_ref, g_ref, *, n_qk_tiles, head_dim):
    j = pl.program_id(1)
    acc = jnp.dot(x_ref[...], w_ref[...], preferred_element_type=F32)

    @pl.when(j == 0)
    def _():
        u_ref[...] = acc

    @pl.when((j >= 1) & (j <= n_qk_tiles))
    def _():
        bd = bd_ref[...]
        for c0 in range(0, acc.shape[1], MXU_DIM):
            a = acc[:, c0:c0 + MXU_DIM]
            hi, lo = _split_bf16(a * a)
            ss = jnp.dot(hi, bd, preferred_element_type=F32) + jnp.dot(lo, bd, preferred_element_type=F32)
            y = a * lax.rsqrt(ss * (1.0 / head_dim) + EPS) * gain_ref[:, c0:c0 + MXU_DIM]
            qk_ref[:, c0:c0 + MXU_DIM] = y.astype(qk_ref.dtype)

    @pl.when(j > n_qk_tiles)
    def _():
        g_ref[...] = acc


def _inproj(xn, w_uqkg, qk_gain, head_dim, pool_width, qk_width):
    n, d = xn.shape
    tn = TN_INPROJ
    assert pool_width == tn and qk_width % tn == 0
    tm = min(TM_INPROJ, n)
    n_qk_tiles = 2 * qk_width // tn
    n_gate_tiles = (w_uqkg.shape[1] - pool_width - 2 * qk_width) // tn
    n_col = 1 + n_qk_tiles + n_gate_tiles
    rows = lax.broadcasted_iota(I32, (MXU_DIM, MXU_DIM), 0) // head_dim
    cols = lax.broadcasted_iota(I32, (MXU_DIM, MXU_DIM), 1) // head_dim
    bd = (rows == cols).astype(BF16)
    kern = functools.partial(_inproj_kernel, n_qk_tiles=n_qk_tiles, head_dim=head_dim)
    return pl.pallas_call(
        kern,
        grid=(n // tm, n_col),
        in_specs=[
            pl.BlockSpec((tm, d), lambda i, j: (i, 0)),
            pl.BlockSpec((d, tn), lambda i, j: (0, j)),
            pl.BlockSpec((1, tn), lambda i, j: (0, jnp.clip(j - 1, 0, n_qk_tiles - 1))),
            pl.BlockSpec((MXU_DIM, MXU_DIM), lambda i, j: (0, 0)),
        ],
        out_specs=[
            pl.BlockSpec((tm, tn), lambda i, j: (i, 0)),
            pl.BlockSpec((tm, tn), lambda i, j: (i, jnp.clip(j - 1, 0, n_qk_tiles - 1))),
            pl.BlockSpec((tm, tn), lambda i, j: (i, jnp.clip(j - 1 - n_qk_tiles, 0, n_gate_tiles - 1))),
        ],
        out_shape=[
            jax.ShapeDtypeStruct((n, pool_width), F32),
            jax.ShapeDtypeStruct((n, 2 * qk_width), BF16),
            jax.ShapeDtypeStruct((n, n_gate_tiles * tn), F32),
        ],
        compiler_params=_cparams("parallel", "arbitrary"),
    )(xn, w_uqkg, qk_gain, bd)


def _vt_kernel(w_ref, x_ref, o_ref):
    o_ref[...] = lax.dot_general(w_ref[...], x_ref[...], (((1,), (1,)), ((), ())),
                                 preferred_element_type=F32).astype(o_ref.dtype)


def _v_transposed(xn, w_vt):
    n, d = xn.shape
    aw = w_vt.shape[0]
    tm = min(TM_INPROJ, n)
    return pl.pallas_call(
        _vt_kernel,
        grid=(n // tm,),
        in_specs=[pl.BlockSpec((aw, d), lambda i: (0, 0)), pl.BlockSpec((tm, d), lambda i: (i, 0))],
        out_specs=pl.BlockSpec((aw, tm), lambda i: (0, i)),
        out_shape=jax.ShapeDtypeStruct((aw, n), BF16),
        compiler_params=_cparams("parallel"),
    )(w_vt, xn)


def _pool_kernel(cur_ref, prev_ref, next_ref, pw_ref, ps_ref, o_ref, ext_hi, ext_lo, *, seq, ts, group):
    i = pl.program_id(1)
    pad = POOL_PAD
    halo = prev_ref.shape[1]
    cur = cur_ref[0]
    width = cur.shape[1]
    zeros = jnp.zeros((pad - halo, width), BF16)
    for ext, part in ((ext_hi, 0), (ext_lo, 1)):
        ext[0:pad - halo, :] = zeros
        ext[pad - halo:pad, :] = _split_bf16(prev_ref[0])[part]
        ext[pad:pad + ts, :] = _split_bf16(cur)[part]
        ext[pad + ts:pad + ts + halo, :] = _split_bf16(next_ref[0])[part]
        ext[pad + ts + halo:, :] = zeros
    t = i * ts + lax.broadcasted_iota(I32, (ts, ts + 2 * pad), 0)
    p = i * ts - pad + lax.broadcasted_iota(I32, (ts, ts + 2 * pad), 1)
    t_col = i * ts + lax.broadcasted_iota(I32, (ts, 1), 0)
    for gi, w in enumerate(POOL_WINDOWS):
        lo = jnp.maximum(t - w // 2, 0)
        hi = jnp.minimum(t + (w - w // 2), seq)
        band = ((p >= lo) & (p < hi)).astype(BF16)
        cnt = (jnp.minimum(t_col + (w - w // 2), seq) - jnp.maximum(t_col - w // 2, 0)).astype(F32)
        ch = slice(gi * group, (gi + 1) * group)
        wsum = (jnp.dot(band, ext_hi[:, ch], preferred_element_type=F32)
                + jnp.dot(band, ext_lo[:, ch], preferred_element_type=F32))
        pooled = wsum / cnt - cur[:, ch]
        y = jnp.dot(pooled.astype(BF16), pw_ref[gi], preferred_element_type=F32) * ps_ref[:, ch]
        o_ref[0, :, ch] = y.astype(o_ref.dtype)


def _pool_mixer(u3, pool_w, pool_scale):
    b, s, pwid = u3.shape
    ng, group, _ = pool_w.shape
    assert ng == len(POOL_WINDOWS) and ng * group == pwid
    ts = min(TS_POOL, s)
    halo = 16
    assert max(POOL_WINDOWS) // 2 <= halo and s % ts == 0 and ts % halo == 0
    nblk = ts // halo
    last = s // halo - 1
    kern = functools.partial(_pool_kernel, seq=s, ts=ts, group=group)
    return pl.pallas_call(
        kern,
        grid=(b, s // ts),
        in_specs=[
            pl.BlockSpec((1, ts, pwid), lambda bi, i: (bi, i, 0)),
            pl.BlockSpec((1, halo, pwid), lambda bi, i: (bi, jnp.maximum(i * nblk - 1, 0), 0)),
            pl.BlockSpec((1, halo, pwid), lambda bi, i: (bi, jnp.minimum((i + 1) * nblk, last), 0)),
            pl.BlockSpec((ng, group, group), lambda bi, i: (0, 0, 0)),
            pl.BlockSpec((1, pwid), lambda bi, i: (0, 0)),
        ],
        out_specs=pl.BlockSpec((1, ts, pwid), lambda bi, i: (bi, i, 0)),
        out_shape=jax.ShapeDtypeStruct((b, s, pwid), BF16),
        scratch_shapes=[pltpu.VMEM((ts + 2 * POOL_PAD, pwid), BF16), pltpu.VMEM((ts + 2 * POOL_PAD, pwid), BF16)],
        compiler_params=_cparams("parallel", "parallel"),
    )(u3, u3, u3, pool_w, pool_scale.reshape(1, pwid))


def _bucket_thresholds(num_buckets):
    half = num_buckets // 2
    max_exact = half // 2
    ratio = MAX_DISTANCE // max_exact
    assert ratio * max_exact == MAX_DISTANCE
    steps = half - max_exact
    thr = []
    for jj in range(1, steps):
        n = max_exact
        while n ** steps < (max_exact ** steps) * (ratio ** jj):
            n += 1
        thr.append(n)
    return half, max_exact, thr


def _bias_kernel(tab_ref, o_ref, *, tile, num_buckets):
    h = pl.program_id(0)
    delta = pl.program_id(1) - 2
    half, max_exact, thr = _bucket_thresholds(num_buckets)
    kk = lax.broadcasted_iota(I32, (tile, tile), 0)
    qq = lax.broadcasted_iota(I32, (tile, tile), 1)
    rel = delta * tile + kk - qq
    n = jnp.abs(rel)
    large = jnp.full((tile, tile), max_exact, I32)
    for th in thr:
        large = large + (n >= th).astype(I32)
    bucket = jnp.where(n < max_exact, n, large) + jnp.where(rel > 0, half, 0)
    out = jnp.zeros((tile, tile), F32)
    for bkt in range(num_buckets):
        out = jnp.where(bucket == bkt, tab_ref[bkt, h], out)
    o_ref[0, 0] = out


def _bias_tiles(rel_bias, tile):
    nb, nh = rel_bias.shape
    assert tile >= MAX_DISTANCE
    kern = functools.partial(_bias_kernel, tile=tile, num_buckets=nb)
    return pl.pallas_call(
        kern,
        grid=(nh, 5),
        in_specs=[pl.BlockSpec(memory_space=pltpu.SMEM)],
        out_specs=pl.BlockSpec((1, 1, tile, tile), lambda h, dd: (h, dd, 0, 0)),
        out_shape=jax.ShapeDtypeStruct((nh, 5, tile, tile), F32),
        compiler_params=_cparams("parallel", "parallel"),
    )(rel_bias)


def _attn_kernel(q_ref, k_ref, vt_ref, b_ref, lqk_ref, sg_ref, o_ref, m_sc, l_sc, acc_sc, *, head_dim, lam_init, nkv):
    j = pl.program_id(3)

    @pl.when(j == 0)
    def _():
        m_sc[...] = jnp.full(m_sc.shape, -1e30, F32)
        l_sc[...] = jnp.zeros(l_sc.shape, F32)
        acc_sc[...] = jnp.zeros(acc_sc.shape, F32)

    q = q_ref[0]
    k = k_ref[0]
    vt = vt_ref[...]
    bias = b_ref[0, 0]
    lane = lax.broadcasted_iota(I32, k.shape, 1)
    for c in range(2):
        kc = jnp.where((lane >= c * head_dim) & (lane < (c + 1) * head_dim), k, jnp.zeros_like(k))
        s = lax.dot_general(kc, q, (((1,), (1,)), ((), ())), preferred_element_type=F32) + bias
        m_prev = m_sc[c]
        m_new = jnp.maximum(m_prev, jnp.max(s, axis=0, keepdims=True))
        alpha = jnp.exp(m_prev - m_new)
        p = jnp.exp(s - m_new)
        l_sc[c] = alpha * l_sc[c] + jnp.sum(p, axis=0, keepdims=True)
        acc_sc[c] = alpha * acc_sc[c] + jnp.dot(vt, p.astype(BF16), preferred_element_type=F32)
        m_sc[c] = m_new

    @pl.when(j == nkv - 1)
    def _():
        lq = lqk_ref[...]
        lam = (jnp.exp(jnp.sum(lq[0:1] * lq[1:2], axis=1, keepdims=True))
               - jnp.exp(jnp.sum(lq[2:3] * lq[3:4], axis=1, keepdims=True)) + lam_init)
        o = acc_sc[0] / l_sc[0] - lam * (acc_sc[1] / l_sc[1])
        ms = jnp.mean(o * o, axis=0, keepdims=True)
        y = o * lax.rsqrt(ms + EPS) * sg_ref[...] * (1.0 - lam_init)
        o_ref[0] = y.T.astype(o_ref.dtype)


def _diff_attention(qk3, vt, bias_tiles, lambda_qk, subln_g, head_dim, lam_init):
    b, s, two_qk = qk3.shape
    nh = bias_tiles.shape[0]
    tile = bias_tiles.shape[2]
    vdim = vt.shape[0] // nh
    assert vdim == 2 * head_dim == LANES and two_qk == 2 * nh * LANES and s % tile == 0
    nblk = s // tile
    kern = functools.partial(_attn_kernel, head_dim=head_dim, lam_init=lam_init, nkv=nblk)
    return pl.pallas_call(
        kern,
        grid=(b, nh, nblk, nblk),
        in_specs=[
            pl.BlockSpec((1, tile, LANES), lambda bi, h, i, j: (bi, i, h)),
            pl.BlockSpec((1, tile, LANES), lambda bi, h, i, j: (bi, j, nh + h)),
            pl.BlockSpec((vdim, tile), lambda bi, h, i, j: (h, bi * nblk + j)),
            pl.BlockSpec((1, 1, tile, tile), lambda bi, h, i, j: (h, jnp.clip(j - i, -2, 2) + 2, 0, 0)),
            pl.BlockSpec(lambda_qk.shape, lambda bi, h, i, j: (0, 0)),
            pl.BlockSpec((vdim, 1), lambda bi, h, i, j: (0, 0)),
        ],
        out_specs=pl.BlockSpec((1, tile, vdim), lambda bi, h, i, j: (bi, i, h)),
        out_shape=jax.ShapeDtypeStruct((b, s, nh * vdim), BF16),
        scratch_shapes=[pltpu.VMEM((2, 1, tile), F32), pltpu.VMEM((2, 1, tile), F32), pltpu.VMEM((2, vdim, tile), F32)],
        compiler_params=_cparams("parallel", "parallel", "parallel", "arbitrary"),
    )(qk3, qk3, vt, bias_tiles, lambda_qk, subln_g.reshape(vdim, 1))


def _sigmoid(x):
    return 1.0 / (1.0 + jnp.exp(-x))


def _merge_kernel(pm_ref, att_ref, gp_ref, ga_ref, wp_ref, wa_ref, o_ref):
    pool_out = jnp.dot(pm_ref[...], wp_ref[...], preferred_element_type=F32)
    attn_out = jnp.dot(att_ref[...], wa_ref[...], preferred_element_type=F32)
    o_ref[...] = (_sigmoid(gp_ref[...]) * pool_out + _sigmoid(ga_ref[...]) * attn_out).astype(o_ref.dtype)


def _merge(pm, att, gates, w_pool_out, w_attn_out):
    n, pwid = pm.shape
    aw = att.shape[1]
    d = w_pool_out.shape[1]
    tm = min(TM_MERGE, n)
    return pl.pallas_call(
        _merge_kernel,
        grid=(n // tm,),
        in_specs=[
            pl.BlockSpec((tm, pwid), lambda i: (i, 0)),
            pl.BlockSpec((tm, aw), lambda i: (i, 0)),
            pl.BlockSpec((tm, d), lambda i: (i, 0)),
            pl.BlockSpec((tm, d), lambda i: (i, 1)),
            pl.BlockSpec((pwid, d), lambda i: (0, 0)),
            pl.BlockSpec((aw, d), lambda i: (0, 0)),
        ],
        out_specs=pl.BlockSpec((tm, d), lambda i: (i, 0)),
        out_shape=jax.ShapeDtypeStruct((n, d), BF16),
        compiler_params=_cparams("parallel"),
    )(pm, att, gates, gates, w_pool_out, w_attn_out)


def _oproj_kernel(m_ref, x_ref, wo_ref, g2_ref, wr_ref, h_ref, t_ref, aff_ref):
    h = x_ref[...] + jnp.dot(m_ref[...], wo_ref[...], preferred_element_type=F32)
    h_ref[...] = h
    ms = jnp.mean(h * h, axis=-1, keepdims=True)
    t = h * lax.rsqrt(ms + EPS) * g2_ref[...]
    t_ref[...] = t
    logits = lax.dot_general(wr_ref[...], t, (((1,), (1,)), ((), ())), preferred_element_type=F32,
                             precision=lax.Precision.HIGHEST)
    e = jnp.exp(logits - jnp.max(logits, axis=0, keepdims=True))
    aff_ref[...] = e / jnp.sum(e, axis=0, keepdims=True)


def _oproj_router(merged, x2d, w_o, norm2_g, w_router_t):
    n, d = x2d.shape
    ne = w_router_t.shape[0]
    tm = min(TM_OPROJ, n)
    return pl.pallas_call(
        _oproj_kernel,
        grid=(n // tm,),
        in_specs=[
            pl.BlockSpec((tm, d), lambda i: (i, 0)),
            pl.BlockSpec((tm, d), lambda i: (i, 0)),
            pl.BlockSpec((d, d), lambda i: (0, 0)),
            pl.BlockSpec((1, d), lambda i: (0, 0)),
            pl.BlockSpec((ne, d), lambda i: (0, 0)),
        ],
        out_specs=[
            pl.BlockSpec((tm, d), lambda i: (i, 0)),
            pl.BlockSpec((tm, d), lambda i: (i, 0)),
            pl.BlockSpec((ne, tm), lambda i: (0, i)),
        ],
        out_shape=[
            jax.ShapeDtypeStruct((n, d), F32),
            jax.ShapeDtypeStruct((n, d), F32),
            jax.ShapeDtypeStruct((ne, n), F32),
        ],
        compiler_params=_cparams("parallel"),
    )(merged, x2d, w_o, norm2_g.reshape(1, d), w_router_t)


def _route_kernel(a_ref, idx_ref, gate_ref, *, cap):
    a = a_ref[0]
    nc, wid = a.shape
    keys = lax.bitcast_convert_type(a, I32)

    def search(bit, thr):
        cand = thr | jnp.left_shift(jnp.int32(1), 30 - bit)
        cnt = jnp.sum(jnp.sum((keys >= cand).astype(I32), axis=1, keepdims=True), axis=0, keepdims=True)
        return jnp.where(cnt >= cap, cand, thr)

    thr = lax.fori_loop(0, 31, search, jnp.zeros((1, 1), I32))
    gt = keys > thr
    eq = keys == thr
    n_gt = jnp.sum(jnp.sum(gt.astype(F32), axis=1, keepdims=True), axis=0, keepdims=True)
    need = cap - n_gt

    upper = (lax.broadcasted_iota(I32, (wid, wid), 0) <= lax.broadcasted_iota(I32, (wid, wid), 1)).astype(BF16)
    lower = (lax.broadcasted_iota(I32, (nc, nc), 1) < lax.broadcasted_iota(I32, (nc, nc), 0)).astype(BF16)

    def prefix(mask):
        inc = jnp.dot(mask.astype(BF16), upper, preferred_element_type=F32)
        tot = inc[:, wid - 1:wid]
        off = jnp.dot(lower, jnp.broadcast_to(tot, (nc, LANES)).astype(BF16), preferred_element_type=F32)[:, 0:1]
        return inc, tot, off

    inc_e, _, off_e = prefix(eq)
    tie_rank = off_e + inc_e - eq.astype(F32)
    sel = gt | (eq & (tie_rank < need))
    inc_s, tot_s, off_s = prefix(sel)

    slot = lax.broadcasted_iota(I32, (1, cap), 1).astype(F32)
    chunk_end = off_s + tot_s
    k_row = jnp.sum((chunk_end <= slot).astype(F32), axis=0, keepdims=True)
    onehot = lax.broadcasted_iota(I32, (nc, cap), 0).astype(F32) == k_row
    off_k = jnp.sum(jnp.where(onehot, off_s, 0.0), axis=0, keepdims=True)
    rank = slot - off_k
    onehot_b = onehot.astype(BF16)
    inc_of_slot = jnp.dot(inc_s.T.astype(BF16), onehot_b, preferred_element_type=F32)
    local = jnp.sum((inc_of_slot <= rank).astype(F32), axis=0, keepdims=True)
    idx_ref[0] = (k_row * wid + local).astype(I32)

    at = a.T
    a1 = at.astype(BF16)
    r1 = at - a1.astype(F32)
    a2 = r1.astype(BF16)
    a3 = (r1 - a2.astype(F32)).astype(BF16)
    aff_of_slot = ((jnp.dot(a1, onehot_b, preferred_element_type=F32) + jnp.dot(a2, onehot_b, preferred_element_type=F32))
                   + jnp.dot(a3, onehot_b, preferred_element_type=F32))
    pick = lax.broadcasted_iota(I32, (wid, cap), 0).astype(F32) == local
    gate_ref[0] = jnp.sum(jnp.where(pick, aff_of_slot, 0.0), axis=0, keepdims=True)


def _route(aff_t, cap):
    ne, n = aff_t.shape
    wid = ROUTE_CHUNK
    assert n % wid == 0
    nc = n // wid
    a3 = aff_t.reshape(ne, nc, wid)
    kern = functools.partial(_route_kernel, cap=cap)
    return pl.pallas_call(
        kern,
        grid=(ne,),
        in_specs=[pl.BlockSpec((1, nc, wid), lambda e: (e, 0, 0))],
        out_specs=[pl.BlockSpec((1, 1, cap), lambda e: (e, 0, 0)), pl.BlockSpec((1, 1, cap), lambda e: (e, 0, 0))],
        out_shape=[jax.ShapeDtypeStruct((ne, 1, cap), I32), jax.ShapeDtypeStruct((ne, 1, cap), F32)],
        compiler_params=_cparams("parallel"),
    )(a3)


def _row_copy(src_hbm, tok, buf, r, sem):
    return pltpu.make_async_copy(src_hbm.at[pl.ds(tok, 1)], buf.at[pl.ds(r, 1)], sem)


def _gather_kernel(idx_ref, t_hbm, o_ref, buf, sem, *, rows):
    def issue(r, carry):
        _row_copy(t_hbm, idx_ref[0, 0, r], buf, r, sem).start()
        return carry

    def drain(r, carry):
        _row_copy(t_hbm, idx_ref[0, 0, r], buf, r, sem).wait()
        return carry

    lax.fori_loop(0, rows, issue, 0)
    lax.fori_loop(0, rows, drain, 0)
    o_ref[0] = buf[...].astype(o_ref.dtype)


def _gather_rows(t, idx):
    n, d = t.shape
    ne, _, cap = idx.shape
    rows = min(R_GATHER, cap)
    steps = cap // rows
    idx3 = idx.reshape(ne * steps, 1, rows)
    kern = functools.partial(_gather_kernel, rows=rows)
    return pl.pallas_call(
        kern,
        grid=(ne, steps),
        in_specs=[
            pl.BlockSpec((1, 1, rows), lambda e, c: (e * steps + c, 0, 0), memory_space=pltpu.SMEM),
            pl.BlockSpec(memory_space=pl.ANY),
        ],
        out_specs=pl.BlockSpec((1, rows, d), lambda e, c: (e, c, 0)),
        out_shape=jax.ShapeDtypeStruct((ne, cap, d), BF16),
        scratch_shapes=[pltpu.VMEM((rows, d), F32), pltpu.SemaphoreType.DMA(())],
        compiler_params=_cparams("arbitrary", "arbitrary"),
    )(idx3, t)


def _ffn_kernel(x_ref, wg_ref, wu_ref, wd_ref, gate_ref, o_ref, acc, *, nf):
    f = pl.program_id(2)

    @pl.when(f == 0)
    def _():
        acc[...] = jnp.zeros(acc.shape, F32)

    x = x_ref[0]
    a = jnp.dot(x, wg_ref[0], preferred_element_type=F32)
    b = jnp.dot(x, wu_ref[0], preferred_element_type=F32)
    hid = (a * _sigmoid(a)) * b
    acc[...] += jnp.dot(hid.astype(BF16), wd_ref[0], preferred_element_type=F32)

    @pl.when(f == nf - 1)
    def _():
        o_ref[0] = acc[...] * gate_ref[0]


def _expert_ffn(xe, w_gate, w_up, w_down, gate_col):
    ne, cap, d = xe.shape
    dff = w_gate.shape[2]
    tm = min(TM_FFN, cap)
    tf = min(TF_FFN, dff)
    nf = dff // tf
    kern = functools.partial(_ffn_kernel, nf=nf)
    return pl.pallas_call(
        kern,
        grid=(ne, cap // tm, nf),
        in_specs=[
            pl.BlockSpec((1, tm, d), lambda e, c, f: (e, c, 0)),
            pl.BlockSpec((1, d, tf), lambda e, c, f: (e, 0, f)),
            pl.BlockSpec((1, d, tf), lambda e, c, f: (e, 0, f)),
            pl.BlockSpec((1, tf, d), lambda e, c, f: (e, f, 0)),
            pl.BlockSpec((1, tm, 1), lambda e, c, f: (e, c, 0)),
        ],
        out_specs=pl.BlockSpec((1, tm, d), lambda e, c, f: (e, c, 0)),
        out_shape=jax.ShapeDtypeStruct((ne, cap, d), F32),
        scratch_shapes=[pltpu.VMEM((tm, d), F32)],
        compiler_params=_cparams("parallel", "parallel", "arbitrary"),
    )(xe, w_gate, w_up, w_down, gate_col)


def _scatter_kernel(idx_ref, ye_ref, h_hbm, y_hbm, buf, sem, *, rows):
    del h_hbm

    def fetch(r, carry):
        _row_copy(y_hbm, idx_ref[0, 0, r], buf, r, sem).start()
        return carry

    def fetched(r, carry):
        _row_copy(y_hbm, idx_ref[0, 0, r], buf, r, sem).wait()
        return carry

    def put(r, carry):
        tok = idx_ref[0, 0, r]
        pltpu.make_async_copy(buf.at[pl.ds(r, 1)], y_hbm.at[pl.ds(tok, 1)], sem).start()
        return carry

    def put_done(r, carry):
        tok = idx_ref[0, 0, r]
        pltpu.make_async_copy(buf.at[pl.ds(r, 1)], y_hbm.at[pl.ds(tok, 1)], sem).wait()
        return carry

    lax.fori_loop(0, rows, fetch, 0)
    lax.fori_loop(0, rows, fetched, 0)
    buf[...] = buf[...] + ye_ref[0]
    lax.fori_loop(0, rows, put, 0)
    lax.fori_loop(0, rows, put_done, 0)


def _scatter_add(h, ye, idx):
    n, d = h.shape
    ne, _, cap = idx.shape
    rows = min(R_GATHER, cap)
    steps = cap // rows
    idx3 = idx.reshape(ne * steps, 1, rows)
    kern = functools.partial(_scatter_kernel, rows=rows)
    return pl.pallas_call(
        kern,
        grid=(ne, steps),
        in_specs=[
            pl.BlockSpec((1, 1, rows), lambda e, c: (e * steps + c, 0, 0), memory_space=pltpu.SMEM),
            pl.BlockSpec((1, rows, d), lambda e, c: (e, c, 0)),
            pl.BlockSpec(memory_space=pl.ANY),
        ],
        out_specs=pl.BlockSpec(memory_space=pl.ANY),
        out_shape=jax.ShapeDtypeStruct((n, d), F32),
        input_output_aliases={2: 0},
        scratch_shapes=[pltpu.VMEM((rows, d), F32), pltpu.SemaphoreType.DMA(())],
        compiler_params=_cparams("arbitrary", "arbitrary"),
    )(idx3, ye, h)


def _prepare_weights(l, norm1_g, w_in, pool_w, pool_scale, w_pool_out, q_norm_g, k_norm_g, rel_bias, lambda_qk, subln_g,
                     w_attn_out, w_o, norm2_g, w_router, w_gate, w_up, w_down):
    head_dim = q_norm_g.shape[1]
    nh = rel_bias.shape[1]
    vdim = subln_g.shape[1]
    pool_width = pool_scale.shape[1]
    qk_width = nh * 2 * head_dim
    attn_width = nh * vdim
    o1, o2, o3 = pool_width, pool_width + 2 * qk_width, pool_width + 2 * qk_width + attn_width
    w = w_in[l]
    reps = qk_width // head_dim
    qk_gain = jnp.concatenate([jnp.tile(q_norm_g[l] * (head_dim ** -0.5), reps), jnp.tile(k_norm_g[l], reps)])
    return dict(
        head_dim=head_dim, nh=nh, vdim=vdim, pool_width=pool_width, qk_width=qk_width,
        lam_init=0.8 - 0.6 * math.exp(-0.3 * l),
        norm1_g=norm1_g[l],
        w_uqkg=jnp.concatenate([w[:, :o2], w[:, o3:]], axis=1).astype(BF16),
        w_vt=w[:, o2:o3].T.astype(BF16),
        qk_gain=qk_gain.reshape(1, -1).astype(F32),
        pool_w=pool_w[l].astype(BF16), pool_scale=pool_scale[l],
        w_pool_out=w_pool_out[l].astype(BF16), w_attn_out=w_attn_out[l].astype(BF16), w_o=w_o[l].astype(BF16),
        lambda_qk=lambda_qk[l], subln_g=subln_g[l], norm2_g=norm2_g[l],
        w_router_t=w_router[l].T,
        w_gate=w_gate[l].astype(BF16), w_up=w_up[l].astype(BF16), w_down=w_down[l].astype(BF16),
    )


def _layer(x, p, bias_tiles):
    b, s, d = x.shape
    n = b * s
    x2d = x.reshape(n, d)
    xn = _rmsnorm(x2d, p["norm1_g"])
    u, qk, gates = _inproj(xn, p["w_uqkg"], p["qk_gain"], p["head_dim"], p["pool_width"], p["qk_width"])
    vt = _v_transposed(xn, p["w_vt"])
    pm = _pool_mixer(u.reshape(b, s, -1), p["pool_w"], p["pool_scale"]).reshape(n, -1)
    att = _diff_attention(qk.reshape(b, s, -1), vt, bias_tiles, p["lambda_qk"], p["subln_g"], p["head_dim"],
                          p["lam_init"]).reshape(n, -1)
    merged = _merge(pm, att, gates, p["w_pool_out"], p["w_attn_out"])
    h, t, aff_t = _oproj_router(merged, x2d, p["w_o"], p["norm2_g"], p["w_router_t"])
    ne = aff_t.shape[0]
    cap = max(1, (CAPACITY_FACTOR * n) // ne)
    idx, gate = _route(aff_t, cap)
    xe = _gather_rows(t, idx)
    ye = _expert_ffn(xe, p["w_gate"], p["w_up"], p["w_down"], gate.reshape(ne, cap, 1))
    y = _scatter_add(h, ye, idx)
    return y.reshape(b, s, d)


def kernel(x_prompt, x_sample, norm1_g, w_in, pool_w, pool_scale, w_pool_out, q_norm_g, k_norm_g, rel_bias, lambda_qk,
           subln_g, w_attn_out, w_o, norm2_g, w_router, w_gate, w_up, w_down):
    tile = min(T_ATTN, x_prompt.shape[1], x_sample.shape[1])
    bias_tiles = _bias_tiles(rel_bias, tile)
    hp, hs = x_prompt, x_sample
    for l in range(norm1_g.shape[0]):
        p = _prepare_weights(l, norm1_g, w_in, pool_w, pool_scale, w_pool_out, q_norm_g, k_norm_g, rel_bias, lambda_qk,
                             subln_g, w_attn_out, w_o, norm2_g, w_router, w_gate, w_up, w_down)
        hp = _layer(hp, p, bias_tiles)
        hs = _layer(hs, p, bias_tiles)
    return (hp, hs)
```

```python
import functools
import math

import jax
import jax.numpy as jnp
from jax import lax
from jax.experimental import pallas as pl
from jax.experimental.pallas import tpu as pltpu

F32, BF16, I32 = jnp.float32, jnp.bfloat16, jnp.int32

EPS = 1e-6
POOL_WINDOWS = (2, 4, 8, 16)
MAX_DISTANCE = 128
CAPACITY_FACTOR = 2
LOG2E = math.log2(math.e)

V7X_VMEM_BYTES = 64 * 1024 * 1024
VMEM_LIMIT_BYTES = V7X_VMEM_BYTES - 8 * 1024 * 1024
LANES = 128
MXU_DIM = 256

TM_NORM = 512
TM_INPROJ = 512
TN_INPROJ = 1024
TS_POOL = 512
POOL_PAD = 128
T_ATTN = 512
TM_MERGE = 256
TM_OPROJ = 256
ROUTE_CHUNK = 256
R_GATHER = 256
DMA_UNROLL = 8
TM_FFN = 1024
TF_FFN = 512


def _cparams(*sem):
    return pltpu.CompilerParams(dimension_semantics=sem, vmem_limit_bytes=VMEM_LIMIT_BYTES)


def _split_bf16(a):
    hi = a.astype(BF16)
    lo = (a - hi.astype(F32)).astype(BF16)
    return hi, lo


def _rmsnorm_kernel(x_ref, g_ref, o_ref):
    x = x_ref[...]
    ms = jnp.mean(x * x, axis=-1, keepdims=True)
    o_ref[...] = (x * lax.rsqrt(ms + EPS) * g_ref[...]).astype(o_ref.dtype)


def _rmsnorm(x2d, g):
    n, d = x2d.shape
    tm = min(TM_NORM, n)
    return pl.pallas_call(
        _rmsnorm_kernel,
        grid=(n // tm,),
        in_specs=[pl.BlockSpec((tm, d), lambda i: (i, 0)), pl.BlockSpec((1, d), lambda i: (0, 0))],
        out_specs=pl.BlockSpec((tm, d), lambda i: (i, 0)),
        out_shape=jax.ShapeDtypeStruct((n, d), BF16),
        compiler_params=_cparams("parallel"),
    )(x2d, g.reshape(1, d))


def _inproj_kernel(x_ref, w_ref, gain_ref, bd_ref, u_ref, qk_ref, g_ref, *, n_qk_tiles, head_dim):
    j = pl.program_id(1)
    acc = jnp.dot(x_ref[...], w_ref[...], preferred_element_type=F32)

    @pl.when(j == 0)
    def _():
        u_ref[...] = acc

    @pl.when((j >= 1) & (j <= n_qk_tiles))
    def _():
        bd = bd_ref[...]
        for c0 in range(0, acc.shape[1], MXU_DIM):
            a = acc[:, c0:c0 + MXU_DIM]
            hi, lo = _split_bf16(a * a)
            ss = jnp.dot(hi, bd, preferred_element_type=F32) + jnp.dot(lo, bd, preferred_element_type=F32)
            y = a * lax.rsqrt(ss * (1.0 / head_dim) + EPS) * gain_ref[:, c0:c0 + MXU_DIM]
            qk_ref[:, c0:c0 + MXU_DIM] = y.astype(qk_ref.dtype)

    @pl.when(j > n_qk_tiles)
    def _():
        g_ref[...] = acc


def _inproj(xn, w_uqkg, qk_gain, head_dim, pool_width, qk_width):
    n, d = xn.shape
    tn = TN_INPROJ
    assert pool_width == tn and qk_width % tn == 0
    tm = min(TM_INPROJ, n)
    n_qk_tiles = 2 * qk_width // tn
    n_gate_tiles = (w_uqkg.shape[1] - pool_width - 2 * qk_width) // tn
    n_col = 1 + n_qk_tiles + n_gate_tiles
    rows = lax.broadcasted_iota(I32, (MXU_DIM, MXU_DIM), 0) // head_dim
    cols = lax.broadcasted_iota(I32, (MXU_DIM, MXU_DIM), 1) // head_dim
    bd = (rows == cols).astype(BF16)
    kern = functools.partial(_inproj_kernel, n_qk_tiles=n_qk_tiles, head_dim=head_dim)
    return pl.pallas_call(
        kern,
        grid=(n // tm, n_col),
        in_specs=[
            pl.BlockSpec((tm, d), lambda i, j: (i, 0)),
            pl.BlockSpec((d, tn), lambda i, j: (0, j)),
            pl.BlockSpec((1, tn), lambda i, j: (0, jnp.clip(j - 1, 0, n_qk_tiles - 1))),
            pl.BlockSpec((MXU_DIM, MXU_DIM), lambda i, j: (0, 0)),
        ],
        out_specs=[
            pl.BlockSpec((tm, tn), lambda i, j: (i, 0)),
            pl.BlockSpec((tm, tn), lambda i, j: (i, jnp.clip(j - 1, 0, n_qk_tiles - 1))),
            pl.BlockSpec((tm, tn), lambda i, j: (i, jnp.clip(j - 1 - n_qk_tiles, 0, n_gate_tiles - 1))),
        ],
        out_shape=[
            jax.ShapeDtypeStruct((n, pool_width), F32),
            jax.ShapeDtypeStruct((n, 2 * qk_width), BF16),
            jax.ShapeDtypeStruct((n, n_gate_tiles * tn), F32),
        ],
        compiler_params=_cparams("parallel", "arbitrary"),
    )(xn, w_uqkg, qk_gain, bd)


def _vt_kernel(w_ref, x_ref, o_ref):
    o_ref[...] = lax.dot_general(w_ref[...], x_ref[...], (((1,), (1,)), ((), ())),
                                 preferred_element_type=F32).astype(o_ref.dtype)


def _v_transposed(xn, w_vt):
    n, d = xn.shape
    aw = w_vt.shape[0]
    tm = min(TM_INPROJ, n)
    return pl.pallas_call(
        _vt_kernel,
        grid=(n // tm,),
        in_specs=[pl.BlockSpec((aw, d), lambda i: (0, 0)), pl.BlockSpec((tm, d), lambda i: (i, 0))],
        out_specs=pl.BlockSpec((aw, tm), lambda i: (0, i)),
        out_shape=jax.ShapeDtypeStruct((aw, n), BF16),
        compiler_params=_cparams("parallel"),
    )(w_vt, xn)


def _pool_kernel(cur_ref, prev_ref, next_ref, pw_ref, ps_ref, o_ref, ext_hi, ext_lo, *, seq, ts, group):
    i = pl.program_id(1)
    pad = POOL_PAD
    halo = prev_ref.shape[1]
    cur = cur_ref[0]
    width = cur.shape[1]
    zeros = jnp.zeros((pad - halo, width), BF16)
    for ext, part in ((ext_hi, 0), (ext_lo, 1)):
        ext[0:pad - halo, :] = zeros
        ext[pad - halo:pad, :] = _split_bf16(prev_ref[0])[part]
        ext[pad:pad + ts, :] = _split_bf16(cur)[part]
        ext[pad + ts:pad + ts + halo, :] = _split_bf16(next_ref[0])[part]
        ext[pad + ts + halo:, :] = zeros
    t = i * ts + lax.broadcasted_iota(I32, (ts, ts + 2 * pad), 0)
    p = i * ts - pad + lax.broadcasted_iota(I32, (ts, ts + 2 * pad), 1)
    t_col = i * ts + lax.broadcasted_iota(I32, (ts, 1), 0)
    for gi, w in enumerate(POOL_WINDOWS):
        lo = jnp.maximum(t - w // 2, 0)
        hi = jnp.minimum(t + (w - w // 2), seq)
        band = ((p >= lo) & (p < hi)).astype(BF16)
        cnt = (jnp.minimum(t_col + (w - w // 2), seq) - jnp.maximum(t_col - w // 2, 0)).astype(F32)
        ch = slice(gi * group, (gi + 1) * group)
        wsum = (jnp.dot(band, ext_hi[:, ch], preferred_element_type=F32)
                + jnp.dot(band, ext_lo[:, ch], preferred_element_type=F32))
        pooled = wsum / cnt - cur[:, ch]
        y = jnp.dot(pooled.astype(BF16), pw_ref[gi], preferred_element_type=F32) * ps_ref[:, ch]
        o_ref[0, :, ch] = y.astype(o_ref.dtype)


def _pool_mixer(u3, pool_w, pool_scale):
    b, s, pwid = u3.shape
    ng, group, _ = pool_w.shape
    assert ng == len(POOL_WINDOWS) and ng * group == pwid
    ts = min(TS_POOL, s)
    halo = 16
    assert max(POOL_WINDOWS) // 2 <= halo and s % ts == 0 and ts % halo == 0
    nblk = ts // halo
    last = s // halo - 1
    kern = functools.partial(_pool_kernel, seq=s, ts=ts, group=group)
    return pl.pallas_call(
        kern,
        grid=(b, s // ts),
        in_specs=[
            pl.BlockSpec((1, ts, pwid), lambda bi, i: (bi, i, 0)),
            pl.BlockSpec((1, halo, pwid), lambda bi, i: (bi, jnp.maximum(i * nblk - 1, 0), 0)),
            pl.BlockSpec((1, halo, pwid), lambda bi, i: (bi, jnp.minimum((i + 1) * nblk, last), 0)),
            pl.BlockSpec((ng, group, group), lambda bi, i: (0, 0, 0)),
            pl.BlockSpec((1, pwid), lambda bi, i: (0, 0)),
        ],
        out_specs=pl.BlockSpec((1, ts, pwid), lambda bi, i: (bi, i, 0)),
        out_shape=jax.ShapeDtypeStruct((b, s, pwid), BF16),
        scratch_shapes=[pltpu.VMEM((ts + 2 * POOL_PAD, pwid), BF16), pltpu.VMEM((ts + 2 * POOL_PAD, pwid), BF16)],
        compiler_params=_cparams("parallel", "parallel"),
    )(u3, u3, u3, pool_w, pool_scale.reshape(1, pwid))


def _bucket_thresholds(num_buckets):
    half = num_buckets // 2
    max_exact = half // 2
    ratio = MAX_DISTANCE // max_exact
    assert ratio * max_exact == MAX_DISTANCE
    steps = half - max_exact
    thr = []
    for jj in range(1, steps):
        n = max_exact
        while n ** steps < (max_exact ** steps) * (ratio ** jj):
            n += 1
        thr.append(n)
    return half, max_exact, thr


def _bias_kernel(tab_ref, o_ref, *, tile, num_buckets):
    h = pl.program_id(0)
    delta = pl.program_id(1) - 2
    half, max_exact, thr = _bucket_thresholds(num_buckets)
    kk = lax.broadcasted_iota(I32, (tile, tile), 0)
    qq = lax.broadcasted_iota(I32, (tile, tile), 1)
    rel = delta * tile + kk - qq
    n = jnp.abs(rel)
    large = jnp.full((tile, tile), max_exact, I32)
    for th in thr:
        large = large + (n >= th).astype(I32)
    bucket = jnp.where(n < max_exact, n, large) + jnp.where(rel > 0, half, 0)
    out = jnp.zeros((tile, tile), F32)
    for bkt in range(num_buckets):
        out = jnp.where(bucket == bkt, tab_ref[bkt, h], out)
    o_ref[0, 0] = out * LOG2E


def _bias_tiles(rel_bias, tile):
    nb, nh = rel_bias.shape
    assert tile >= MAX_DISTANCE
    kern = functools.partial(_bias_kernel, tile=tile, num_buckets=nb)
    return pl.pallas_call(
        kern,
        grid=(nh, 5),
        in_specs=[pl.BlockSpec(memory_space=pltpu.SMEM)],
        out_specs=pl.BlockSpec((1, 1, tile, tile), lambda h, dd: (h, dd, 0, 0)),
        out_shape=jax.ShapeDtypeStruct((nh, 5, tile, tile), F32),
        compiler_params=_cparams("parallel", "parallel"),
    )(rel_bias)


def _attn_kernel(tab_ref, q_ref, k_ref, vt_ref, b_ref, lqk_ref, sg_ref, o_ref,
                 s_sc, mb_sc, off_sc, m_sc, acc_sc, *, head_dim, lam_init, nblk):
    h = pl.program_id(1)
    i = pl.program_id(2)
    tile = q_ref.shape[1]
    vdim = vt_ref.shape[0]
    nsteps = nblk // 2
    q = q_ref[0]
    lane = lax.broadcasted_iota(I32, (tile, q.shape[1]), 1)
    ones = jnp.ones((acc_sc.shape[1] - vdim, tile), BF16)
    nb = tab_ref.shape[0]

    def start(blk):
        return blk * tile if isinstance(blk, int) else pl.multiple_of(blk * tile, tile)

    def scores(slot, blk, far):
        k = k_ref[0, pl.ds(start(blk), tile), :]
        if far:
            off = jnp.where(blk < i, tab_ref[nb // 2 - 1, h], tab_ref[nb - 1, h]) * LOG2E
        else:
            off = 0.0
            bias = b_ref[0, jnp.clip(blk - i, -2, 2) + 2]
        off_sc[slot] = jnp.full((1, tile), off, F32)
        for c in range(2):
            kc = jnp.where((lane >= c * head_dim) & (lane < (c + 1) * head_dim), k, jnp.zeros_like(k))
            s = lax.dot_general(kc, q, (((1,), (1,)), ((), ())), preferred_element_type=F32)
            if not far:
                s = s + bias
            s_sc[slot, c] = s
            mb_sc[slot, c] = jnp.max(s, axis=0, keepdims=True) + off

    def consume(slot, blk):
        vt = vt_ref[:, pl.ds(start(blk), tile)]
        vt_ones = jnp.concatenate([vt, ones], axis=0)
        for c in range(2):
            m_prev = m_sc[c]
            m_new = jnp.maximum(m_prev, mb_sc[slot, c])
            alpha = jnp.exp2(m_prev - m_new)
            p = jnp.exp2(s_sc[slot, c] - (m_new - off_sc[slot]))
            acc_sc[c] = alpha * acc_sc[c] + jnp.dot(vt_ones, p.astype(BF16), preferred_element_type=F32)
            m_sc[c] = m_new

    m_sc[...] = jnp.full(m_sc.shape, -1e30, F32)
    acc_sc[...] = jnp.zeros(acc_sc.shape, F32)
    scores(0, 0, False)

    def pair(j, carry):
        blk_b = 2 * j + 1
        blk_a = 2 * j + 2
        both_far = (jnp.abs(blk_b - i) >= 2) & (jnp.abs(blk_a - i) >= 2)

        def step(far):
            scores(1, blk_b, far)
            consume(0, 2 * j)
            scores(0, blk_a, far)
            consume(1, blk_b)

        @pl.when(both_far)
        def _():
            step(True)

        @pl.when(jnp.logical_not(both_far))
        def _():
            step(False)

        return carry

    lax.fori_loop(0, nsteps - 1, pair, 0)
    scores(1, nblk - 1, False)
    consume(0, nblk - 2)
    consume(1, nblk - 1)

    lq = lqk_ref[...]
    lam = (jnp.exp(jnp.sum(lq[0:1] * lq[1:2], axis=1, keepdims=True))
           - jnp.exp(jnp.sum(lq[2:3] * lq[3:4], axis=1, keepdims=True)) + lam_init)
    a0 = acc_sc[0]
    a1 = acc_sc[1]
    o = a0[:vdim] / a0[vdim:vdim + 1] - lam * (a1[:vdim] / a1[vdim:vdim + 1])
    ms = jnp.mean(o * o, axis=0, keepdims=True)
    y = o * lax.rsqrt(ms + EPS) * sg_ref[...] * (1.0 - lam_init)
    o_ref[0] = y.T.astype(o_ref.dtype)


def _diff_attention(qk3, vt, bias_tiles, rel_bias, lambda_qk, subln_g, head_dim, lam_init):
    b, s, two_qk = qk3.shape
    nh = bias_tiles.shape[0]
    tile = bias_tiles.shape[2]
    vdim = vt.shape[0] // nh
    assert vdim == 2 * head_dim == LANES and two_qk == 2 * nh * LANES and s % (2 * tile) == 0
    nblk = s // tile
    ones_rows = 16
    kern = functools.partial(_attn_kernel, head_dim=head_dim, lam_init=lam_init, nblk=nblk)
    return pl.pallas_call(
        kern,
        grid=(b, nh, nblk),
        in_specs=[
            pl.BlockSpec(memory_space=pltpu.SMEM),
            pl.BlockSpec((1, tile, LANES), lambda bi, h, i: (bi, i, h)),
            pl.BlockSpec((1, s, LANES), lambda bi, h, i: (bi, 0, nh + h)),
            pl.BlockSpec((vdim, s), lambda bi, h, i: (h, bi)),
            pl.BlockSpec((1, 5, tile, tile), lambda bi, h, i: (h, 0, 0, 0)),
            pl.BlockSpec(lambda_qk.shape, lambda bi, h, i: (0, 0)),
            pl.BlockSpec((vdim, 1), lambda bi, h, i: (0, 0)),
        ],
        out_specs=pl.BlockSpec((1, tile, vdim), lambda bi, h, i: (bi, i, h)),
        out_shape=jax.ShapeDtypeStruct((b, s, nh * vdim), BF16),
        scratch_shapes=[
            pltpu.VMEM((2, 2, tile, tile), F32),
            pltpu.VMEM((2, 2, 1, tile), F32),
            pltpu.VMEM((2, 1, tile), F32),
            pltpu.VMEM((2, 1, tile), F32),
            pltpu.VMEM((2, vdim + ones_rows, tile), F32),
        ],
        compiler_params=_cparams("parallel", "parallel", "parallel"),
    )(rel_bias, qk3, qk3, vt, bias_tiles, lambda_qk, subln_g.reshape(vdim, 1))


def _sigmoid(x):
    return 1.0 / (1.0 + jnp.exp(-x))


def _merge_kernel(pm_ref, att_ref, gp_ref, ga_ref, wp_ref, wa_ref, o_ref):
    pool_out = jnp.dot(pm_ref[...], wp_ref[...], preferred_element_type=F32)
    attn_out = jnp.dot(att_ref[...], wa_ref[...], preferred_element_type=F32)
    o_ref[...] = (_sigmoid(gp_ref[...]) * pool_out + _sigmoid(ga_ref[...]) * attn_out).astype(o_ref.dtype)


def _merge(pm, att, gates, w_pool_out, w_attn_out):
    n, pwid = pm.shape
    aw = att.shape[1]
    d = w_pool_out.shape[1]
    tm = min(TM_MERGE, n)
    return pl.pallas_call(
        _merge_kernel,
        grid=(n // tm,),
        in_specs=[
            pl.BlockSpec((tm, pwid), lambda i: (i, 0)),
            pl.BlockSpec((tm, aw), lambda i: (i, 0)),
            pl.BlockSpec((tm, d), lambda i: (i, 0)),
            pl.BlockSpec((tm, d), lambda i: (i, 1)),
            pl.BlockSpec((pwid, d), lambda i: (0, 0)),
            pl.BlockSpec((aw, d), lambda i: (0, 0)),
        ],
        out_specs=pl.BlockSpec((tm, d), lambda i: (i, 0)),
        out_shape=jax.ShapeDtypeStruct((n, d), BF16),
        compiler_params=_cparams("parallel"),
    )(pm, att, gates, gates, w_pool_out, w_attn_out)


def _oproj_kernel(m_ref, x_ref, wo_ref, g2_ref, wr_ref, h_ref, t_ref, aff_ref):
    h = x_ref[...] + jnp.dot(m_ref[...], wo_ref[...], preferred_element_type=F32)
    h_ref[...] = h
    ms = jnp.mean(h * h, axis=-1, keepdims=True)
    t = h * lax.rsqrt(ms + EPS) * g2_ref[...]
    t_ref[...] = t
    t_hi, t_lo = _split_bf16(t)
    w2 = wr_ref[...]
    p_hi = jnp.dot(t_hi, w2, preferred_element_type=F32)
    p_lo = jnp.dot(t_lo, w2, preferred_element_type=F32)
    logits = (p_hi[:, :LANES] + p_hi[:, LANES:]) + p_lo[:, :LANES]
    ne = aff_ref.shape[0]
    lt = logits.T[:ne]
    e = jnp.exp(lt - jnp.max(lt, axis=0, keepdims=True))
    aff_ref[...] = e / jnp.sum(e, axis=0, keepdims=True)


def _oproj_router(merged, x2d, w_o, norm2_g, w_router):
    n, d = x2d.shape
    ne = w_router.shape[1]
    assert ne <= LANES
    w_hi, w_lo = _split_bf16(w_router)
    w2 = jnp.zeros((d, 2 * LANES), BF16).at[:, :ne].set(w_hi).at[:, LANES:LANES + ne].set(w_lo)
    tm = min(TM_OPROJ, n)
    return pl.pallas_call(
        _oproj_kernel,
        grid=(n // tm,),
        in_specs=[
            pl.BlockSpec((tm, d), lambda i: (i, 0)),
            pl.BlockSpec((tm, d), lambda i: (i, 0)),
            pl.BlockSpec((d, d), lambda i: (0, 0)),
            pl.BlockSpec((1, d), lambda i: (0, 0)),
            pl.BlockSpec((d, 2 * LANES), lambda i: (0, 0)),
        ],
        out_specs=[
            pl.BlockSpec((tm, d), lambda i: (i, 0)),
            pl.BlockSpec((tm, d), lambda i: (i, 0)),
            pl.BlockSpec((ne, tm), lambda i: (0, i)),
        ],
        out_shape=[
            jax.ShapeDtypeStruct((n, d), F32),
            jax.ShapeDtypeStruct((n, d), F32),
            jax.ShapeDtypeStruct((ne, n), F32),
        ],
        compiler_params=_cparams("parallel"),
    )(merged, x2d, w_o, norm2_g.reshape(1, d), w2)


def _route_kernel(a_ref, idx_ref, gate_ref, *, cap):
    a = a_ref[0]
    nc, wid = a.shape
    keys = lax.bitcast_convert_type(a, I32)

    def search(bit, thr):
        cand = thr | jnp.left_shift(jnp.int32(1), 30 - bit)
        cnt = jnp.sum(jnp.sum((keys >= cand).astype(I32), axis=1, keepdims=True), axis=0, keepdims=True)
        return jnp.where(cnt >= cap, cand, thr)

    thr = lax.fori_loop(0, 31, search, jnp.zeros((1, 1), I32))
    gt = keys > thr
    eq = keys == thr
    n_gt = jnp.sum(jnp.sum(gt.astype(F32), axis=1, keepdims=True), axis=0, keepdims=True)
    need = cap - n_gt

    upper = (lax.broadcasted_iota(I32, (wid, wid), 0) <= lax.broadcasted_iota(I32, (wid, wid), 1)).astype(BF16)
    lower = (lax.broadcasted_iota(I32, (nc, nc), 1) < lax.broadcasted_iota(I32, (nc, nc), 0)).astype(BF16)

    def prefix(mask):
        inc = jnp.dot(mask.astype(BF16), upper, preferred_element_type=F32)
        tot = inc[:, wid - 1:wid]
        off = jnp.dot(lower, jnp.broadcast_to(tot, (nc, LANES)).astype(BF16), preferred_element_type=F32)[:, 0:1]
        return inc, tot, off

    inc_e, _, off_e = prefix(eq)
    tie_rank = off_e + inc_e - eq.astype(F32)
    sel = gt | (eq & (tie_rank < need))
    inc_s, tot_s, off_s = prefix(sel)

    slot = lax.broadcasted_iota(I32, (1, cap), 1).astype(F32)
    chunk_end = off_s + tot_s
    k_row = jnp.sum((chunk_end <= slot).astype(F32), axis=0, keepdims=True)
    onehot = lax.broadcasted_iota(I32, (nc, cap), 0).astype(F32) == k_row
    off_k = jnp.sum(jnp.where(onehot, off_s, 0.0), axis=0, keepdims=True)
    rank = slot - off_k
    onehot_b = onehot.astype(BF16)
    inc_of_slot = jnp.dot(inc_s.T.astype(BF16), onehot_b, preferred_element_type=F32)
    local = jnp.sum((inc_of_slot <= rank).astype(F32), axis=0, keepdims=True)
    idx_ref[0] = (k_row * wid + local).astype(I32)

    at = a.T
    a1 = at.astype(BF16)
    r1 = at - a1.astype(F32)
    a2 = r1.astype(BF16)
    a3 = (r1 - a2.astype(F32)).astype(BF16)
    aff_of_slot = ((jnp.dot(a1, onehot_b, preferred_element_type=F32) + jnp.dot(a2, onehot_b, preferred_element_type=F32))
                   + jnp.dot(a3, onehot_b, preferred_element_type=F32))
    pick = lax.broadcasted_iota(I32, (wid, cap), 0).astype(F32) == local
    gate_ref[0] = jnp.sum(jnp.where(pick, aff_of_slot, 0.0), axis=0, keepdims=True)


def _route(aff_t, cap):
    ne, n = aff_t.shape
    wid = ROUTE_CHUNK
    assert n % wid == 0
    nc = n // wid
    a3 = aff_t.reshape(ne, nc, wid)
    kern = functools.partial(_route_kernel, cap=cap)
    return pl.pallas_call(
        kern,
        grid=(ne,),
        in_specs=[pl.BlockSpec((1, nc, wid), lambda e: (e, 0, 0))],
        out_specs=[pl.BlockSpec((1, 1, cap), lambda e: (e, 0, 0)), pl.BlockSpec((1, 1, cap), lambda e: (e, 0, 0))],
        out_shape=[jax.ShapeDtypeStruct((ne, 1, cap), I32), jax.ShapeDtypeStruct((ne, 1, cap), F32)],
        compiler_params=_cparams("parallel"),
    )(a3)


def _row_copy(src_hbm, tok, buf, r, sem):
    return pltpu.make_async_copy(src_hbm.at[pl.ds(tok, 1)], buf.at[pl.ds(r, 1)], sem)


def _gather_kernel(idx_ref, t_hbm, o_ref, buf, sem, *, rows):
    def issue(r, carry):
        _row_copy(t_hbm, idx_ref[0, 0, r], buf, r, sem).start()
        return carry

    def drain(r, carry):
        _row_copy(t_hbm, idx_ref[0, 0, r], buf, r, sem).wait()
        return carry

    lax.fori_loop(0, rows, issue, 0, unroll=DMA_UNROLL)
    lax.fori_loop(0, rows, drain, 0, unroll=DMA_UNROLL)
    o_ref[0] = buf[...].astype(o_ref.dtype)


def _gather_rows(t, idx):
    n, d = t.shape
    ne, _, cap = idx.shape
    rows = min(R_GATHER, cap)
    steps = cap // rows
    idx3 = idx.reshape(ne * steps, 1, rows)
    kern = functools.partial(_gather_kernel, rows=rows)
    return pl.pallas_call(
        kern,
        grid=(ne, steps),
        in_specs=[
            pl.BlockSpec((1, 1, rows), lambda e, c: (e * steps + c, 0, 0), memory_space=pltpu.SMEM),
            pl.BlockSpec(memory_space=pl.ANY),
        ],
        out_specs=pl.BlockSpec((1, rows, d), lambda e, c: (e, c, 0)),
        out_shape=jax.ShapeDtypeStruct((ne, cap, d), BF16),
        scratch_shapes=[pltpu.VMEM((rows, d), F32), pltpu.SemaphoreType.DMA(())],
        compiler_params=_cparams("arbitrary", "arbitrary"),
    )(idx3, t)


def _ffn_kernel(x_ref, wg_ref, wu_ref, wd_ref, gate_ref, o_ref, acc, *, nf):
    f = pl.program_id(2)

    @pl.when(f == 0)
    def _():
        acc[...] = jnp.zeros(acc.shape, F32)

    x = x_ref[0]
    a = jnp.dot(x, wg_ref[0], preferred_element_type=F32)
    b = jnp.dot(x, wu_ref[0], preferred_element_type=F32)
    hid = (a * _sigmoid(a)) * b
    acc[...] += jnp.dot(hid.astype(BF16), wd_ref[0], preferred_element_type=F32)

    @pl.when(f == nf - 1)
    def _():
        o_ref[0] = acc[...] * gate_ref[0]


def _expert_ffn(xe, w_gate, w_up, w_down, gate_col):
    ne, cap, d = xe.shape
    dff = w_gate.shape[2]
    tm = min(TM_FFN, cap)
    tf = min(TF_FFN, dff)
    nf = dff // tf
    kern = functools.partial(_ffn_kernel, nf=nf)
    return pl.pallas_call(
        kern,
        grid=(ne, cap // tm, nf),
        in_specs=[
            pl.BlockSpec((1, tm, d), lambda e, c, f: (e, c, 0)),
            pl.BlockSpec((1, d, tf), lambda e, c, f: (e, 0, f)),
            pl.BlockSpec((1, d, tf), lambda e, c, f: (e, 0, f)),
            pl.BlockSpec((1, tf, d), lambda e, c, f: (e, f, 0)),
            pl.BlockSpec((1, tm, 1), lambda e, c, f: (e, c, 0)),
        ],
        out_specs=pl.BlockSpec((1, tm, d), lambda e, c, f: (e, c, 0)),
        out_shape=jax.ShapeDtypeStruct((ne, cap, d), F32),
        scratch_shapes=[pltpu.VMEM((tm, d), F32)],
        compiler_params=_cparams("parallel", "parallel", "arbitrary"),
    )(xe, w_gate, w_up, w_down, gate_col)


def _scatter_kernel(idx_ref, ye_ref, h_hbm, y_hbm, buf, sem, *, rows):
    del h_hbm

    def fetch(r, carry):
        _row_copy(y_hbm, idx_ref[0, 0, r], buf, r, sem).start()
        return carry

    def fetched(r, carry):
        _row_copy(y_hbm, idx_ref[0, 0, r], buf, r, sem).wait()
        return carry

    def put(r, carry):
        tok = idx_ref[0, 0, r]
        pltpu.make_async_copy(buf.at[pl.ds(r, 1)], y_hbm.at[pl.ds(tok, 1)], sem).start()
        return carry

    def put_done(r, carry):
        tok = idx_ref[0, 0, r]
        pltpu.make_async_copy(buf.at[pl.ds(r, 1)], y_hbm.at[pl.ds(tok, 1)], sem).wait()
        return carry

    lax.fori_loop(0, rows, fetch, 0, unroll=DMA_UNROLL)
    lax.fori_loop(0, rows, fetched, 0, unroll=DMA_UNROLL)
    buf[...] = buf[...] + ye_ref[0]
    lax.fori_loop(0, rows, put, 0, unroll=DMA_UNROLL)
    lax.fori_loop(0, rows, put_done, 0, unroll=DMA_UNROLL)


def _scatter_add(h, ye, idx):
    n, d = h.shape
    ne, _, cap = idx.shape
    rows = min(R_GATHER, cap)
    steps = cap // rows
    idx3 = idx.reshape(ne * steps, 1, rows)
    kern = functools.partial(_scatter_kernel, rows=rows)
    return pl.pallas_call(
        kern,
        grid=(ne, steps),
        in_specs=[
            pl.BlockSpec((1, 1, rows), lambda e, c: (e * steps + c, 0, 0), memory_space=pltpu.SMEM),
            pl.BlockSpec((1, rows, d), lambda e, c: (e, c, 0)),
            pl.BlockSpec(memory_space=pl.ANY),
        ],
        out_specs=pl.BlockSpec(memory_space=pl.ANY),
        out_shape=jax.ShapeDtypeStruct((n, d), F32),
        input_output_aliases={2: 0},
        scratch_shapes=[pltpu.VMEM((rows, d), F32), pltpu.SemaphoreType.DMA(())],
        compiler_params=_cparams("arbitrary", "arbitrary"),
    )(idx3, ye, h)


def _prepare_weights(l, norm1_g, w_in, pool_w, pool_scale, w_pool_out, q_norm_g, k_norm_g, rel_bias, lambda_qk, subln_g,
                     w_attn_out, w_o, norm2_g, w_router, w_gate, w_up, w_down):
    head_dim = q_norm_g.shape[1]
    nh = rel_bias.shape[1]
    vdim = subln_g.shape[1]
    pool_width = pool_scale.shape[1]
    qk_width = nh * 2 * head_dim
    attn_width = nh * vdim
    o1, o2, o3 = pool_width, pool_width + 2 * qk_width, pool_width + 2 * qk_width + attn_width
    w = w_in[l]
    reps = qk_width // head_dim
    qk_gain = jnp.concatenate([jnp.tile(q_norm_g[l] * (head_dim ** -0.5) * LOG2E, reps), jnp.tile(k_norm_g[l], reps)])
    return dict(
        head_dim=head_dim, nh=nh, vdim=vdim, pool_width=pool_width, qk_width=qk_width,
        lam_init=0.8 - 0.6 * math.exp(-0.3 * l),
        norm1_g=norm1_g[l],
        w_uqkg=jnp.concatenate([w[:, :o2], w[:, o3:]], axis=1).astype(BF16),
        w_vt=w[:, o2:o3].T.astype(BF16),
        qk_gain=qk_gain.reshape(1, -1).astype(F32),
        pool_w=pool_w[l].astype(BF16), pool_scale=pool_scale[l],
        w_pool_out=w_pool_out[l].astype(BF16), w_attn_out=w_attn_out[l].astype(BF16), w_o=w_o[l].astype(BF16),
        rel_bias=rel_bias, lambda_qk=lambda_qk[l], subln_g=subln_g[l], norm2_g=norm2_g[l],
        w_router=w_router[l],
        w_gate=w_gate[l].astype(BF16), w_up=w_up[l].astype(BF16), w_down=w_down[l].astype(BF16),
    )


def _layer(x, p, bias_tiles):
    b, s, d = x.shape
    n = b * s
    x2d = x.reshape(n, d)
    xn = _rmsnorm(x2d, p["norm1_g"])
    u, qk, gates = _inproj(xn, p["w_uqkg"], p["qk_gain"], p["head_dim"], p["pool_width"], p["qk_width"])
    vt = _v_transposed(xn, p["w_vt"])
    pm = _pool_mixer(u.reshape(b, s, -1), p["pool_w"], p["pool_scale"]).reshape(n, -1)
    att = _diff_attention(qk.reshape(b, s, -1), vt, bias_tiles, p["rel_bias"], p["lambda_qk"], p["subln_g"],
                          p["head_dim"], p["lam_init"]).reshape(n, -1)
    merged = _merge(pm, att, gates, p["w_pool_out"], p["w_attn_out"])
    h, t, aff_t = _oproj_router(merged, x2d, p["w_o"], p["norm2_g"], p["w_router"])
    ne = aff_t.shape[0]
    cap = max(1, (CAPACITY_FACTOR * n) // ne)
    idx, gate = _route(aff_t, cap)
    xe = _gather_rows(t, idx)
    ye = _expert_ffn(xe, p["w_gate"], p["w_up"], p["w_down"], gate.reshape(ne, cap, 1))
    y = _scatter_add(h, ye, idx)
    return y.reshape(b, s, d)


def kernel(x_prompt, x_sample, norm1_g, w_in, pool_w, pool_scale, w_pool_out, q_norm_g, k_norm_g, rel_bias, lambda_qk,
           subln_g, w_attn_out, w_o, norm2_g, w_router, w_gate, w_up, w_down):
    tile = min(T_ATTN, x_prompt.shape[1], x_sample.shape[1])
    bias_tiles = _bias_tiles(rel_bias, tile)
    hp, hs = x_prompt, x_sample
    for l in range(norm1_g.shape[0]):
        p = _prepare_weights(l, norm1_g, w_in, pool_w, pool_scale, w_pool_out, q_norm_g, k_norm_g, rel_bias, lambda_qk,
                             subln_g, w_attn_out, w_o, norm2_g, w_router, w_gate, w_up, w_down)
        hp = _layer(hp, p, bias_tiles)
        hs = _layer(hs, p, bias_tiles)
    return (hp, hs)
```

```python
import functools
import math

import jax
import jax.numpy as jnp
from jax import lax
from jax.experimental import pallas as pl
from jax.experimental.pallas import tpu as pltpu

F32, BF16, I32 = jnp.float32, jnp.bfloat16, jnp.int32

EPS = 1e-6
POOL_WINDOWS = (2, 4, 8, 16)
MAX_DISTANCE = 128
CAPACITY_FACTOR = 2
LOG2E = math.log2(math.e)

V7X_VMEM_BYTES = 64 * 1024 * 1024
VMEM_LIMIT_BYTES = V7X_VMEM_BYTES - 8 * 1024 * 1024
LANES = 128
MXU_DIM = 256

TM_NORM = 512
TM_INPROJ = 1024
TN_INPROJ = 1024
TS_POOL = 512
POOL_PAD = 128
T_ATTN = 512
TM_MERGE = 256
TM_OPROJ = 256
ROUTE_CHUNK = 256
R_GATHER = 512
DMA_UNROLL = 8
TM_FFN = 1024
TF_FFN = 512


def _cparams(*sem):
    return pltpu.CompilerParams(dimension_semantics=sem, vmem_limit_bytes=VMEM_LIMIT_BYTES)


def _split_bf16(a):
    hi = a.astype(BF16)
    lo = (a - hi.astype(F32)).astype(BF16)
    return hi, lo


def _rmsnorm_kernel(x_ref, g_ref, o_ref):
    x = x_ref[...]
    ms = jnp.mean(x * x, axis=-1, keepdims=True)
    o_ref[...] = (x * lax.rsqrt(ms + EPS) * g_ref[...]).astype(o_ref.dtype)


def _rmsnorm(x2d, g):
    n, d = x2d.shape
    tm = min(TM_NORM, n)
    return pl.pallas_call(
        _rmsnorm_kernel,
        grid=(n // tm,),
        in_specs=[pl.BlockSpec((tm, d), lambda i: (i, 0)), pl.BlockSpec((1, d), lambda i: (0, 0))],
        out_specs=pl.BlockSpec((tm, d), lambda i: (i, 0)),
        out_shape=jax.ShapeDtypeStruct((n, d), BF16),
        compiler_params=_cparams("parallel"),
    )(x2d, g.reshape(1, d))


def _inproj_kernel(x_ref, w_ref, gain_ref, bd_ref, u_ref, qk_ref, g_ref, *, n_qk_tiles, head_dim):
    j = pl.program_id(1)
    acc = jnp.dot(x_ref[...], w_ref[...], preferred_element_type=F32)

    @pl.when(j == 0)
    def _():
        u_ref[...] = acc

    @pl.when((j >= 1) & (j <= n_qk_tiles))
    def _():
        bd = bd_ref[...]
        for c0 in range(0, acc.shape[1], MXU_DIM):
            a = acc[:, c0:c0 + MXU_DIM]
            hi, lo = _split_bf16(a * a)
            ss = jnp.dot(hi, bd, preferred_element_type=F32) + jnp.dot(lo, bd, preferred_element_type=F32)
            y = a * lax.rsqrt(ss * (1.0 / head_dim) + EPS) * gain_ref[:, c0:c0 + MXU_DIM]
            qk_ref[:, c0:c0 + MXU_DIM] = y.astype(qk_ref.dtype)

    @pl.when(j > n_qk_tiles)
    def _():
        g_ref[...] = acc


def _inproj(xn, w_uqkg, qk_gain, head_dim, pool_width, qk_width):
    n, d = xn.shape
    tn = TN_INPROJ
    assert pool_width == tn and qk_width % tn == 0
    tm = min(TM_INPROJ, n)
    n_qk_tiles = 2 * qk_width // tn
    n_gate_tiles = (w_uqkg.shape[1] - pool_width - 2 * qk_width) // tn
    n_col = 1 + n_qk_tiles + n_gate_tiles
    rows = lax.broadcasted_iota(I32, (MXU_DIM, MXU_DIM), 0) // head_dim
    cols = lax.broadcasted_iota(I32, (MXU_DIM, MXU_DIM), 1) // head_dim
    bd = (rows == cols).astype(BF16)
    kern = functools.partial(_inproj_kernel, n_qk_tiles=n_qk_tiles, head_dim=head_dim)
    return pl.pallas_call(
        kern,
        grid=(n // tm, n_col),
        in_specs=[
            pl.BlockSpec((tm, d), lambda i, j: (i, 0)),
            pl.BlockSpec((d, tn), lambda i, j: (0, j)),
            pl.BlockSpec((1, tn), lambda i, j: (0, jnp.clip(j - 1, 0, n_qk_tiles - 1))),
            pl.BlockSpec((MXU_DIM, MXU_DIM), lambda i, j: (0, 0)),
        ],
        out_specs=[
            pl.BlockSpec((tm, tn), lambda i, j: (i, 0)),
            pl.BlockSpec((tm, tn), lambda i, j: (i, jnp.clip(j - 1, 0, n_qk_tiles - 1))),
            pl.BlockSpec((tm, tn), lambda i, j: (i, jnp.clip(j - 1 - n_qk_tiles, 0, n_gate_tiles - 1))),
        ],
        out_shape=[
            jax.ShapeDtypeStruct((n, pool_width), F32),
            jax.ShapeDtypeStruct((n, 2 * qk_width), BF16),
            jax.ShapeDtypeStruct((n, n_gate_tiles * tn), F32),
        ],
        compiler_params=_cparams("parallel", "arbitrary"),
    )(xn, w_uqkg, qk_gain, bd)


def _vt_kernel(w_ref, x_ref, o_ref):
    o_ref[...] = lax.dot_general(w_ref[...], x_ref[...], (((1,), (1,)), ((), ())),
                                 preferred_element_type=F32).astype(o_ref.dtype)


def _v_transposed(xn, w_vt):
    n, d = xn.shape
    aw = w_vt.shape[0]
    tm = min(TM_INPROJ, n)
    return pl.pallas_call(
        _vt_kernel,
        grid=(n // tm,),
        in_specs=[pl.BlockSpec((aw, d), lambda i: (0, 0)), pl.BlockSpec((tm, d), lambda i: (i, 0))],
        out_specs=pl.BlockSpec((aw, tm), lambda i: (0, i)),
        out_shape=jax.ShapeDtypeStruct((aw, n), BF16),
        compiler_params=_cparams("parallel"),
    )(w_vt, xn)


def _pool_kernel(cur_ref, prev_ref, next_ref, pw_ref, ps_ref, o_ref, ext_hi, ext_lo, *, seq, ts, group):
    i = pl.program_id(1)
    pad = POOL_PAD
    halo = prev_ref.shape[1]
    cur = cur_ref[0]
    width = cur.shape[1]
    zeros = jnp.zeros((pad - halo, width), BF16)
    for ext, part in ((ext_hi, 0), (ext_lo, 1)):
        ext[0:pad - halo, :] = zeros
        ext[pad - halo:pad, :] = _split_bf16(prev_ref[0])[part]
        ext[pad:pad + ts, :] = _split_bf16(cur)[part]
        ext[pad + ts:pad + ts + halo, :] = _split_bf16(next_ref[0])[part]
        ext[pad + ts + halo:, :] = zeros
    t = i * ts + lax.broadcasted_iota(I32, (ts, ts + 2 * pad), 0)
    p = i * ts - pad + lax.broadcasted_iota(I32, (ts, ts + 2 * pad), 1)
    t_col = i * ts + lax.broadcasted_iota(I32, (ts, 1), 0)
    for gi, w in enumerate(POOL_WINDOWS):
        lo = jnp.maximum(t - w // 2, 0)
        hi = jnp.minimum(t + (w - w // 2), seq)
        band = ((p >= lo) & (p < hi)).astype(BF16)
        cnt = (jnp.minimum(t_col + (w - w // 2), seq) - jnp.maximum(t_col - w // 2, 0)).astype(F32)
        ch = slice(gi * group, (gi + 1) * group)
        wsum = (jnp.dot(band, ext_hi[:, ch], preferred_element_type=F32)
                + jnp.dot(band, ext_lo[:, ch], preferred_element_type=F32))
        pooled = wsum / cnt - cur[:, ch]
        y = jnp.dot(pooled.astype(BF16), pw_ref[gi], preferred_element_type=F32) * ps_ref[:, ch]
        o_ref[0, :, ch] = y.astype(o_ref.dtype)


def _pool_mixer(u3, pool_w, pool_scale):
    b, s, pwid = u3.shape
    ng, group, _ = pool_w.shape
    assert ng == len(POOL_WINDOWS) and ng * group == pwid
    ts = min(TS_POOL, s)
    halo = 16
    assert max(POOL_WINDOWS) // 2 <= halo and s % ts == 0 and ts % halo == 0
    nblk = ts // halo
    last = s // halo - 1
    kern = functools.partial(_pool_kernel, seq=s, ts=ts, group=group)
    return pl.pallas_call(
        kern,
        grid=(b, s // ts),
        in_specs=[
            pl.BlockSpec((1, ts, pwid), lambda bi, i: (bi, i, 0)),
            pl.BlockSpec((1, halo, pwid), lambda bi, i: (bi, jnp.maximum(i * nblk - 1, 0), 0)),
            pl.BlockSpec((1, halo, pwid), lambda bi, i: (bi, jnp.minimum((i + 1) * nblk, last), 0)),
            pl.BlockSpec((ng, group, group), lambda bi, i: (0, 0, 0)),
            pl.BlockSpec((1, pwid), lambda bi, i: (0, 0)),
        ],
        out_specs=pl.BlockSpec((1, ts, pwid), lambda bi, i: (bi, i, 0)),
        out_shape=jax.ShapeDtypeStruct((b, s, pwid), BF16),
        scratch_shapes=[pltpu.VMEM((ts + 2 * POOL_PAD, pwid), BF16), pltpu.VMEM((ts + 2 * POOL_PAD, pwid), BF16)],
        compiler_params=_cparams("parallel", "parallel"),
    )(u3, u3, u3, pool_w, pool_scale.reshape(1, pwid))


def _bucket_thresholds(num_buckets):
    half = num_buckets // 2
    max_exact = half // 2
    ratio = MAX_DISTANCE // max_exact
    assert ratio * max_exact == MAX_DISTANCE
    steps = half - max_exact
    thr = []
    for jj in range(1, steps):
        n = max_exact
        while n ** steps < (max_exact ** steps) * (ratio ** jj):
            n += 1
        thr.append(n)
    return half, max_exact, thr


def _bias_kernel(tab_ref, o_ref, *, tile, num_buckets):
    h = pl.program_id(0)
    delta = pl.program_id(1) - 2
    half, max_exact, thr = _bucket_thresholds(num_buckets)
    kk = lax.broadcasted_iota(I32, (tile, tile), 0)
    qq = lax.broadcasted_iota(I32, (tile, tile), 1)
    rel = delta * tile + kk - qq
    n = jnp.abs(rel)
    large = jnp.full((tile, tile), max_exact, I32)
    for th in thr:
        large = large + (n >= th).astype(I32)
    bucket = jnp.where(n < max_exact, n, large) + jnp.where(rel > 0, half, 0)
    out = jnp.zeros((tile, tile), F32)
    for bkt in range(num_buckets):
        out = jnp.where(bucket == bkt, tab_ref[bkt, h], out)
    o_ref[0, 0] = out * LOG2E


def _bias_tiles(rel_bias, tile):
    nb, nh = rel_bias.shape
    assert tile >= MAX_DISTANCE
    kern = functools.partial(_bias_kernel, tile=tile, num_buckets=nb)
    return pl.pallas_call(
        kern,
        grid=(nh, 5),
        in_specs=[pl.BlockSpec(memory_space=pltpu.SMEM)],
        out_specs=pl.BlockSpec((1, 1, tile, tile), lambda h, dd: (h, dd, 0, 0)),
        out_shape=jax.ShapeDtypeStruct((nh, 5, tile, tile), F32),
        compiler_params=_cparams("parallel", "parallel"),
    )(rel_bias)


def _attn_kernel(tab_ref, q_ref, k_ref, vt_ref, b_ref, lqk_ref, sg_ref, o_ref,
                 s_sc, mb_sc, off_sc, m_sc, acc_sc, *, head_dim, lam_init, nblk):
    h = pl.program_id(1)
    i = pl.program_id(2)
    tile = q_ref.shape[1]
    vdim = vt_ref.shape[0]
    nsteps = nblk // 2
    q = q_ref[0]
    lane = lax.broadcasted_iota(I32, (tile, q.shape[1]), 1)
    ones = jnp.ones((acc_sc.shape[1] - vdim, tile), BF16)
    nb = tab_ref.shape[0]

    def start(blk):
        return blk * tile if isinstance(blk, int) else pl.multiple_of(blk * tile, tile)

    def scores(slot, blk, far):
        k = k_ref[0, pl.ds(start(blk), tile), :]
        if far:
            off = jnp.where(blk < i, tab_ref[nb // 2 - 1, h], tab_ref[nb - 1, h]) * LOG2E
        else:
            off = 0.0
            bias = b_ref[0, jnp.clip(blk - i, -2, 2) + 2]
        off_sc[slot] = jnp.full((1, tile), off, F32)
        for c in range(2):
            kc = jnp.where((lane >= c * head_dim) & (lane < (c + 1) * head_dim), k, jnp.zeros_like(k))
            s = lax.dot_general(kc, q, (((1,), (1,)), ((), ())), preferred_element_type=F32)
            if not far:
                s = s + bias
            s_sc[slot, c] = s
            mb_sc[slot, c] = jnp.max(s, axis=0, keepdims=True) + off

    def consume(slot, blk):
        vt = vt_ref[:, pl.ds(start(blk), tile)]
        vt_ones = jnp.concatenate([vt, ones], axis=0)
        for c in range(2):
            m_prev = m_sc[c]
            m_new = jnp.maximum(m_prev, mb_sc[slot, c])
            alpha = jnp.exp2(m_prev - m_new)
            p = jnp.exp2(s_sc[slot, c] - (m_new - off_sc[slot]))
            acc_sc[c] = alpha * acc_sc[c] + jnp.dot(vt_ones, p.astype(BF16), preferred_element_type=F32)
            m_sc[c] = m_new

    m_sc[...] = jnp.full(m_sc.shape, -1e30, F32)
    acc_sc[...] = jnp.zeros(acc_sc.shape, F32)
    scores(0, 0, False)

    def pair(j, carry):
        blk_b = 2 * j + 1
        blk_a = 2 * j + 2
        both_far = (jnp.abs(blk_b - i) >= 2) & (jnp.abs(blk_a - i) >= 2)

        def step(far):
            scores(1, blk_b, far)
            consume(0, 2 * j)
            scores(0, blk_a, far)
            consume(1, blk_b)

        @pl.when(both_far)
        def _():
            step(True)

        @pl.when(jnp.logical_not(both_far))
        def _():
            step(False)

        return carry

    lax.fori_loop(0, nsteps - 1, pair, 0)
    scores(1, nblk - 1, False)
    consume(0, nblk - 2)
    consume(1, nblk - 1)

    lq = lqk_ref[...]
    lam = (jnp.exp(jnp.sum(lq[0:1] * lq[1:2], axis=1, keepdims=True))
           - jnp.exp(jnp.sum(lq[2:3] * lq[3:4], axis=1, keepdims=True)) + lam_init)
    a0 = acc_sc[0]
    a1 = acc_sc[1]
    o = a0[:vdim] / a0[vdim:vdim + 1] - lam * (a1[:vdim] / a1[vdim:vdim + 1])
    ms = jnp.mean(o * o, axis=0, keepdims=True)
    y = o * lax.rsqrt(ms + EPS) * sg_ref[...] * (1.0 - lam_init)
    o_ref[0] = y.T.astype(o_ref.dtype)


def _diff_attention(qk3, vt, bias_tiles, rel_bias, lambda_qk, subln_g, head_dim, lam_init):
    b, s, two_qk = qk3.shape
    nh = bias_tiles.shape[0]
    tile = bias_tiles.shape[2]
    vdim = vt.shape[0] // nh
    assert vdim == 2 * head_dim == LANES and two_qk == 2 * nh * LANES and s % (2 * tile) == 0
    nblk = s // tile
    ones_rows = 16
    kern = functools.partial(_attn_kernel, head_dim=head_dim, lam_init=lam_init, nblk=nblk)
    return pl.pallas_call(
        kern,
        grid=(b, nh, nblk),
        in_specs=[
            pl.BlockSpec(memory_space=pltpu.SMEM),
            pl.BlockSpec((1, tile, LANES), lambda bi, h, i: (bi, i, h)),
            pl.BlockSpec((1, s, LANES), lambda bi, h, i: (bi, 0, nh + h)),
            pl.BlockSpec((vdim, s), lambda bi, h, i: (h, bi)),
            pl.BlockSpec((1, 5, tile, tile), lambda bi, h, i: (h, 0, 0, 0)),
            pl.BlockSpec(lambda_qk.shape, lambda bi, h, i: (0, 0)),
            pl.BlockSpec((vdim, 1), lambda bi, h, i: (0, 0)),
        ],
        out_specs=pl.BlockSpec((1, tile, vdim), lambda bi, h, i: (bi, i, h)),
        out_shape=jax.ShapeDtypeStruct((b, s, nh * vdim), BF16),
        scratch_shapes=[
            pltpu.VMEM((2, 2, tile, tile), F32),
            pltpu.VMEM((2, 2, 1, tile), F32),
            pltpu.VMEM((2, 1, tile), F32),
            pltpu.VMEM((2, 1, tile), F32),
            pltpu.VMEM((2, vdim + ones_rows, tile), F32),
        ],
        compiler_params=_cparams("parallel", "parallel", "parallel"),
    )(rel_bias, qk3, qk3, vt, bias_tiles, lambda_qk, subln_g.reshape(vdim, 1))


def _sigmoid(x):
    return 1.0 / (1.0 + jnp.exp(-x))


def _merge_kernel(pm_ref, att_ref, gp_ref, ga_ref, wp_ref, wa_ref, o_ref):
    pool_out = jnp.dot(pm_ref[...], wp_ref[...], preferred_element_type=F32)
    attn_out = jnp.dot(att_ref[...], wa_ref[...], preferred_element_type=F32)
    o_ref[...] = (_sigmoid(gp_ref[...]) * pool_out + _sigmoid(ga_ref[...]) * attn_out).astype(o_ref.dtype)


def _merge(pm, att, gates, w_pool_out, w_attn_out):
    n, pwid = pm.shape
    aw = att.shape[1]
    d = w_pool_out.shape[1]
    tm = min(TM_MERGE, n)
    return pl.pallas_call(
        _merge_kernel,
        grid=(n // tm,),
        in_specs=[
            pl.BlockSpec((tm, pwid), lambda i: (i, 0)),
            pl.BlockSpec((tm, aw), lambda i: (i, 0)),
            pl.BlockSpec((tm, d), lambda i: (i, 0)),
            pl.BlockSpec((tm, d), lambda i: (i, 1)),
            pl.BlockSpec((pwid, d), lambda i: (0, 0)),
            pl.BlockSpec((aw, d), lambda i: (0, 0)),
        ],
        out_specs=pl.BlockSpec((tm, d), lambda i: (i, 0)),
        out_shape=jax.ShapeDtypeStruct((n, d), BF16),
        compiler_params=_cparams("parallel"),
    )(pm, att, gates, gates, w_pool_out, w_attn_out)


def _oproj_kernel(m_ref, x_ref, wo_ref, g2_ref, wr_ref, h_ref, t_ref, aff_ref):
    h = x_ref[...] + jnp.dot(m_ref[...], wo_ref[...], preferred_element_type=F32)
    h_ref[...] = h
    ms = jnp.mean(h * h, axis=-1, keepdims=True)
    t = h * lax.rsqrt(ms + EPS) * g2_ref[...]
    t_ref[...] = t
    t_hi, t_lo = _split_bf16(t)
    w2 = wr_ref[...]
    p_hi = jnp.dot(t_hi, w2, preferred_element_type=F32)
    p_lo = jnp.dot(t_lo, w2, preferred_element_type=F32)
    logits = (p_hi[:, :LANES] + p_hi[:, LANES:]) + p_lo[:, :LANES]
    ne = aff_ref.shape[0]
    lt = logits.T[:ne]
    e = jnp.exp(lt - jnp.max(lt, axis=0, keepdims=True))
    aff_ref[...] = e / jnp.sum(e, axis=0, keepdims=True)


def _oproj_router(merged, x2d, w_o, norm2_g, w_router):
    n, d = x2d.shape
    ne = w_router.shape[1]
    assert ne <= LANES
    w_hi, w_lo = _split_bf16(w_router)
    w2 = jnp.zeros((d, 2 * LANES), BF16).at[:, :ne].set(w_hi).at[:, LANES:LANES + ne].set(w_lo)
    tm = min(TM_OPROJ, n)
    return pl.pallas_call(
        _oproj_kernel,
        grid=(n // tm,),
        in_specs=[
            pl.BlockSpec((tm, d), lambda i: (i, 0)),
            pl.BlockSpec((tm, d), lambda i: (i, 0)),
            pl.BlockSpec((d, d), lambda i: (0, 0)),
            pl.BlockSpec((1, d), lambda i: (0, 0)),
            pl.BlockSpec((d, 2 * LANES), lambda i: (0, 0)),
        ],
        out_specs=[
            pl.BlockSpec((tm, d), lambda i: (i, 0)),
            pl.BlockSpec((tm, d), lambda i: (i, 0)),
            pl.BlockSpec((ne, tm), lambda i: (0, i)),
        ],
        out_shape=[
            jax.ShapeDtypeStruct((n, d), F32),
            jax.ShapeDtypeStruct((n, d), F32),
            jax.ShapeDtypeStruct((ne, n), F32),
        ],
        compiler_params=_cparams("parallel"),
    )(merged, x2d, w_o, norm2_g.reshape(1, d), w2)


def _route_kernel(a_ref, idx_ref, gate_ref, *, cap):
    a = a_ref[0]
    nc, wid = a.shape
    keys = lax.bitcast_convert_type(a, I32)

    def search(bit, thr):
        cand = thr | jnp.left_shift(jnp.int32(1), 30 - bit)
        cnt = jnp.sum(jnp.sum((keys >= cand).astype(I32), axis=1, keepdims=True), axis=0, keepdims=True)
        return jnp.where(cnt >= cap, cand, thr)

    thr = lax.fori_loop(0, 31, search, jnp.zeros((1, 1), I32))
    gt = keys > thr
    eq = keys == thr
    n_gt = jnp.sum(jnp.sum(gt.astype(F32), axis=1, keepdims=True), axis=0, keepdims=True)
    need = cap - n_gt

    upper = (lax.broadcasted_iota(I32, (wid, wid), 0) <= lax.broadcasted_iota(I32, (wid, wid), 1)).astype(BF16)
    lower = (lax.broadcasted_iota(I32, (nc, nc), 1) < lax.broadcasted_iota(I32, (nc, nc), 0)).astype(BF16)

    def prefix(mask):
        inc = jnp.dot(mask.astype(BF16), upper, preferred_element_type=F32)
        tot = inc[:, wid - 1:wid]
        off = jnp.dot(lower, jnp.broadcast_to(tot, (nc, LANES)).astype(BF16), preferred_element_type=F32)[:, 0:1]
        return inc, tot, off

    inc_e, _, off_e = prefix(eq)
    tie_rank = off_e + inc_e - eq.astype(F32)
    sel = gt | (eq & (tie_rank < need))
    inc_s, tot_s, off_s = prefix(sel)

    slot = lax.broadcasted_iota(I32, (1, cap), 1).astype(F32)
    chunk_end = off_s + tot_s
    k_row = jnp.sum((chunk_end <= slot).astype(F32), axis=0, keepdims=True)
    onehot = lax.broadcasted_iota(I32, (nc, cap), 0).astype(F32) == k_row
    off_k = jnp.sum(jnp.where(onehot, off_s, 0.0), axis=0, keepdims=True)
    rank = slot - off_k
    onehot_b = onehot.astype(BF16)
    inc_of_slot = jnp.dot(inc_s.T.astype(BF16), onehot_b, preferred_element_type=F32)
    local = jnp.sum((inc_of_slot <= rank).astype(F32), axis=0, keepdims=True)
    idx_ref[0] = (k_row * wid + local).astype(I32)

    at = a.T
    a1 = at.astype(BF16)
    r1 = at - a1.astype(F32)
    a2 = r1.astype(BF16)
    a3 = (r1 - a2.astype(F32)).astype(BF16)
    aff_of_slot = ((jnp.dot(a1, onehot_b, preferred_element_type=F32) + jnp.dot(a2, onehot_b, preferred_element_type=F32))
                   + jnp.dot(a3, onehot_b, preferred_element_type=F32))
    pick = lax.broadcasted_iota(I32, (wid, cap), 0).astype(F32) == local
    gate_ref[0] = jnp.sum(jnp.where(pick, aff_of_slot, 0.0), axis=0, keepdims=True)


def _route(aff_t, cap):
    ne, n = aff_t.shape
    wid = ROUTE_CHUNK
    assert n % wid == 0
    nc = n // wid
    a3 = aff_t.reshape(ne, nc, wid)
    kern = functools.partial(_route_kernel, cap=cap)
    return pl.pallas_call(
        kern,
        grid=(ne,),
        in_specs=[pl.BlockSpec((1, nc, wid), lambda e: (e, 0, 0))],
        out_specs=[pl.BlockSpec((1, 1, cap), lambda e: (e, 0, 0)), pl.BlockSpec((1, 1, cap), lambda e: (e, 0, 0))],
        out_shape=[jax.ShapeDtypeStruct((ne, 1, cap), I32), jax.ShapeDtypeStruct((ne, 1, cap), F32)],
        compiler_params=_cparams("parallel"),
    )(a3)


def _row_copy(src_hbm, tok, buf, r, sem):
    return pltpu.make_async_copy(src_hbm.at[pl.ds(tok, 1)], buf.at[pl.ds(r, 1)], sem)


def _gather_kernel(idx_ref, idx_next_ref, t_hbm, o_ref, buf, sem, *, rows, total):
    step = pl.program_id(0) * pl.num_programs(1) + pl.program_id(1)
    slot = step % 2

    def fetch(iref, s):
        def body(r, carry):
            _row_copy(t_hbm, iref[0, 0, r], buf.at[s], r, sem.at[s]).start()
            return carry
        lax.fori_loop(0, rows, body, 0, unroll=DMA_UNROLL)

    @pl.when(step == 0)
    def _():
        fetch(idx_ref, 0)

    @pl.when(step + 1 < total)
    def _():
        fetch(idx_next_ref, 1 - slot)

    def fetched(r, carry):
        _row_copy(t_hbm, 0, buf.at[slot], r, sem.at[slot]).wait()
        return carry

    lax.fori_loop(0, rows, fetched, 0, unroll=DMA_UNROLL)
    o_ref[0] = buf[slot].astype(o_ref.dtype)


def _gather_rows(t, idx):
    n, d = t.shape
    ne, _, cap = idx.shape
    rows = min(R_GATHER, cap)
    steps = cap // rows
    total = ne * steps
    idx3 = idx.reshape(total, 1, rows)
    kern = functools.partial(_gather_kernel, rows=rows, total=total)
    return pl.pallas_call(
        kern,
        grid=(ne, steps),
        in_specs=[
            pl.BlockSpec((1, 1, rows), lambda e, c: (e * steps + c, 0, 0), memory_space=pltpu.SMEM),
            pl.BlockSpec((1, 1, rows), lambda e, c: (jnp.minimum(e * steps + c + 1, total - 1), 0, 0),
                         memory_space=pltpu.SMEM),
            pl.BlockSpec(memory_space=pl.ANY),
        ],
        out_specs=pl.BlockSpec((1, rows, d), lambda e, c: (e, c, 0)),
        out_shape=jax.ShapeDtypeStruct((ne, cap, d), BF16),
        scratch_shapes=[pltpu.VMEM((2, rows, d), F32), pltpu.SemaphoreType.DMA((2,))],
        compiler_params=_cparams("arbitrary", "arbitrary"),
    )(idx3, idx3, t)


def _ffn_kernel(x_ref, wg_ref, wu_ref, wd_ref, gate_ref, o_ref, acc, *, nf):
    f = pl.program_id(2)

    @pl.when(f == 0)
    def _():
        acc[...] = jnp.zeros(acc.shape, F32)

    x = x_ref[0]
    a = jnp.dot(x, wg_ref[0], preferred_element_type=F32)
    b = jnp.dot(x, wu_ref[0], preferred_element_type=F32)
    hid = (a * _sigmoid(a)) * b
    acc[...] += jnp.dot(hid.astype(BF16), wd_ref[0], preferred_element_type=F32)

    @pl.when(f == nf - 1)
    def _():
        o_ref[0] = acc[...] * gate_ref[0]


def _expert_ffn(xe, w_gate, w_up, w_down, gate_col):
    ne, cap, d = xe.shape
    dff = w_gate.shape[2]
    tm = min(TM_FFN, cap)
    tf = min(TF_FFN, dff)
    nf = dff // tf
    kern = functools.partial(_ffn_kernel, nf=nf)
    return pl.pallas_call(
        kern,
        grid=(ne, cap // tm, nf),
        in_specs=[
            pl.BlockSpec((1, tm, d), lambda e, c, f: (e, c, 0)),
            pl.BlockSpec((1, d, tf), lambda e, c, f: (e, 0, f)),
            pl.BlockSpec((1, d, tf), lambda e, c, f: (e, 0, f)),
            pl.BlockSpec((1, tf, d), lambda e, c, f: (e, f, 0)),
            pl.BlockSpec((1, tm, 1), lambda e, c, f: (e, c, 0)),
        ],
        out_specs=pl.BlockSpec((1, tm, d), lambda e, c, f: (e, c, 0)),
        out_shape=jax.ShapeDtypeStruct((ne, cap, d), F32),
        scratch_shapes=[pltpu.VMEM((tm, d), F32)],
        compiler_params=_cparams("parallel", "parallel", "arbitrary"),
    )(xe, w_gate, w_up, w_down, gate_col)


def _scatter_kernel(idx_ref, idx_next_ref, ye_ref, h_hbm, y_hbm, buf, sem_in, sem_out, *, rows, steps):
    del h_hbm
    c = pl.program_id(1)
    slot = c % 2

    def fetch(iref, s):
        def body(r, carry):
            _row_copy(y_hbm, iref[0, 0, r], buf.at[s], r, sem_in.at[s]).start()
            return carry
        lax.fori_loop(0, rows, body, 0, unroll=DMA_UNROLL)

    def row_put(tok, s, r):
        return pltpu.make_async_copy(buf.at[s].at[pl.ds(r, 1)], y_hbm.at[pl.ds(tok, 1)], sem_out.at[s])

    def put_done(s):
        def body(r, carry):
            row_put(0, s, r).wait()
            return carry
        lax.fori_loop(0, rows, body, 0, unroll=DMA_UNROLL)

    @pl.when(c == 0)
    def _():
        fetch(idx_ref, 0)

    @pl.when(c >= 1)
    def _():
        put_done(1 - slot)

    @pl.when(c + 1 < steps)
    def _():
        fetch(idx_next_ref, 1 - slot)

    def fetched(r, carry):
        _row_copy(y_hbm, 0, buf.at[slot], r, sem_in.at[slot]).wait()
        return carry

    lax.fori_loop(0, rows, fetched, 0, unroll=DMA_UNROLL)
    buf[slot] = buf[slot] + ye_ref[0]

    def put(r, carry):
        row_put(idx_ref[0, 0, r], slot, r).start()
        return carry

    lax.fori_loop(0, rows, put, 0, unroll=DMA_UNROLL)

    @pl.when(c == steps - 1)
    def _():
        put_done(slot)


def _scatter_add(h, ye, idx):
    n, d = h.shape
    ne, _, cap = idx.shape
    rows = min(R_GATHER, cap)
    steps = cap // rows
    total = ne * steps
    idx3 = idx.reshape(total, 1, rows)
    kern = functools.partial(_scatter_kernel, rows=rows, steps=steps)
    return pl.pallas_call(
        kern,
        grid=(ne, steps),
        in_specs=[
            pl.BlockSpec((1, 1, rows), lambda e, c: (e * steps + c, 0, 0), memory_space=pltpu.SMEM),
            pl.BlockSpec((1, 1, rows), lambda e, c: (jnp.minimum(e * steps + c + 1, total - 1), 0, 0),
                         memory_space=pltpu.SMEM),
            pl.BlockSpec((1, rows, d), lambda e, c: (e, c, 0)),
            pl.BlockSpec(memory_space=pl.ANY),
        ],
        out_specs=pl.BlockSpec(memory_space=pl.ANY),
        out_shape=jax.ShapeDtypeStruct((n, d), F32),
        input_output_aliases={3: 0},
        scratch_shapes=[pltpu.VMEM((2, rows, d), F32), pltpu.SemaphoreType.DMA((2,)), pltpu.SemaphoreType.DMA((2,))],
        compiler_params=_cparams("arbitrary", "arbitrary"),
    )(idx3, idx3, ye, h)


def _prepare_weights(l, norm1_g, w_in, pool_w, pool_scale, w_pool_out, q_norm_g, k_norm_g, rel_bias, lambda_qk, subln_g,
                     w_attn_out, w_o, norm2_g, w_router, w_gate, w_up, w_down):
    head_dim = q_norm_g.shape[1]
    nh = rel_bias.shape[1]
    vdim = subln_g.shape[1]
    pool_width = pool_scale.shape[1]
    qk_width = nh * 2 * head_dim
    attn_width = nh * vdim
    o1, o2, o3 = pool_width, pool_width + 2 * qk_width, pool_width + 2 * qk_width + attn_width
    w = w_in[l]
    reps = qk_width // head_dim
    qk_gain = jnp.concatenate([jnp.tile(q_norm_g[l] * (head_dim ** -0.5) * LOG2E, reps), jnp.tile(k_norm_g[l], reps)])
    return dict(
        head_dim=head_dim, nh=nh, vdim=vdim, pool_width=pool_width, qk_width=qk_width,
        lam_init=0.8 - 0.6 * math.exp(-0.3 * l),
        norm1_g=norm1_g[l],
        w_uqkg=jnp.concatenate([w[:, :o2], w[:, o3:]], axis=1).astype(BF16),
        w_vt=w[:, o2:o3].T.astype(BF16),
        qk_gain=qk_gain.reshape(1, -1).astype(F32),
        pool_w=pool_w[l].astype(BF16), pool_scale=pool_scale[l],
        w_pool_out=w_pool_out[l].astype(BF16), w_attn_out=w_attn_out[l].astype(BF16), w_o=w_o[l].astype(BF16),
        rel_bias=rel_bias, lambda_qk=lambda_qk[l], subln_g=subln_g[l], norm2_g=norm2_g[l],
        w_router=w_router[l],
        w_gate=w_gate[l].astype(BF16), w_up=w_up[l].astype(BF16), w_down=w_down[l].astype(BF16),
    )


def _layer(x, p, bias_tiles):
    b, s, d = x.shape
    n = b * s
    x2d = x.reshape(n, d)
    xn = _rmsnorm(x2d, p["norm1_g"])
    u, qk, gates = _inproj(xn, p["w_uqkg"], p["qk_gain"], p["head_dim"], p["pool_width"], p["qk_width"])
    vt = _v_transposed(xn, p["w_vt"])
    pm = _pool_mixer(u.reshape(b, s, -1), p["pool_w"], p["pool_scale"]).reshape(n, -1)
    att = _diff_attention(qk.reshape(b, s, -1), vt, bias_tiles, p["rel_bias"], p["lambda_qk"], p["subln_g"],
                          p["head_dim"], p["lam_init"]).reshape(n, -1)
    merged = _merge(pm, att, gates, p["w_pool_out"], p["w_attn_out"])
    h, t, aff_t = _oproj_router(merged, x2d, p["w_o"], p["norm2_g"], p["w_router"])
    ne = aff_t.shape[0]
    cap = max(1, (CAPACITY_FACTOR * n) // ne)
    idx, gate = _route(aff_t, cap)
    xe = _gather_rows(t, idx)
    ye = _expert_ffn(xe, p["w_gate"], p["w_up"], p["w_down"], gate.reshape(ne, cap, 1))
    y = _scatter_add(h, ye, idx)
    return y.reshape(b, s, d)


def kernel(x_prompt, x_sample, norm1_g, w_in, pool_w, pool_scale, w_pool_out, q_norm_g, k_norm_g, rel_bias, lambda_qk,
           subln_g, w_attn_out, w_o, norm2_g, w_router, w_gate, w_up, w_down):
    tile = min(T_ATTN, x_prompt.shape[1], x_sample.shape[1])
    bias_tiles = _bias_tiles(rel_bias, tile)
    hp, hs = x_prompt, x_sample
    for l in range(norm1_g.shape[0]):
        p = _prepare_weights(l, norm1_g, w_in, pool_w, pool_scale, w_pool_out, q_norm_g, k_norm_g, rel_bias, lambda_qk,
                             subln_g, w_attn_out, w_o, norm2_g, w_router, w_gate, w_up, w_down)
        hp = _layer(hp, p, bias_tiles)
        hs = _layer(hs, p, bias_tiles)
    return (hp, hs)
```

```python
import functools
import math

import jax
import jax.numpy as jnp
from jax import lax
from jax.experimental import pallas as pl
from jax.experimental.pallas import tpu as pltpu
from jax.experimental.pallas import tpu_sc as plsc

F32, BF16, I32, U32 = jnp.float32, jnp.bfloat16, jnp.int32, jnp.uint32

EPS = 1e-6
POOL_WINDOWS = (2, 4, 8, 16)
MAX_DISTANCE = 128
CAPACITY_FACTOR = 2
LOG2E = math.log2(math.e)

V7X_VMEM_BYTES = 64 * 1024 * 1024
VMEM_LIMIT_BYTES = V7X_VMEM_BYTES - 8 * 1024 * 1024
LANES = 128
MXU_DIM = 256

TM_NORM = 512
TM_INPROJ = 1024
TN_INPROJ = 1024
TS_POOL = 512
POOL_PAD = 128
T_ATTN = 512
PAIRS_PER_TRIP = 2
TM_MERGE = 256
TM_OPROJ = 256
ROUTE_CHUNK = 256
R_GATHER = 512
DMA_UNROLL = 8
SC_PIECE = 256
SC_WINDOW = 128
TM_FFN = 1024
TF_FFN = 512


def _cparams(*sem):
    return pltpu.CompilerParams(dimension_semantics=sem, vmem_limit_bytes=VMEM_LIMIT_BYTES)


def _split_bf16(a):
    hi = a.astype(BF16)
    lo = (a - hi.astype(F32)).astype(BF16)
    return hi, lo


def _rmsnorm_kernel(x_ref, g_ref, o_ref):
    x = x_ref[...]
    ms = jnp.mean(x * x, axis=-1, keepdims=True)
    o_ref[...] = (x * lax.rsqrt(ms + EPS) * g_ref[...]).astype(o_ref.dtype)


def _rmsnorm(x2d, g):
    n, d = x2d.shape
    tm = min(TM_NORM, n)
    return pl.pallas_call(
        _rmsnorm_kernel,
        grid=(n // tm,),
        in_specs=[pl.BlockSpec((tm, d), lambda i: (i, 0)), pl.BlockSpec((1, d), lambda i: (0, 0))],
        out_specs=pl.BlockSpec((tm, d), lambda i: (i, 0)),
        out_shape=jax.ShapeDtypeStruct((n, d), BF16),
        compiler_params=_cparams("parallel"),
    )(x2d, g.reshape(1, d))


def _inproj_kernel(x_ref, w_ref, gain_ref, bd_ref, u_ref, qk_ref, g_ref, *, n_qk_tiles, head_dim):
    j = pl.program_id(1)
    acc = jnp.dot(x_ref[...], w_ref[...], preferred_element_type=F32)

    @pl.when(j == 0)
    def _():
        u_ref[...] = acc

    @pl.when((j >= 1) & (j <= n_qk_tiles))
    def _():
        bd = bd_ref[...]
        for c0 in range(0, acc.shape[1], MXU_DIM):
            a = acc[:, c0:c0 + MXU_DIM]
            hi, lo = _split_bf16(a * a)
            ss = jnp.dot(hi, bd, preferred_element_type=F32) + jnp.dot(lo, bd, preferred_element_type=F32)
            y = a * lax.rsqrt(ss * (1.0 / head_dim) + EPS) * gain_ref[:, c0:c0 + MXU_DIM]
            qk_ref[:, c0:c0 + MXU_DIM] = y.astype(qk_ref.dtype)

    @pl.when(j > n_qk_tiles)
    def _():
        g_ref[...] = acc


def _inproj(xn, w_uqkg, qk_gain, head_dim, pool_width, qk_width):
    n, d = xn.shape
    tn = TN_INPROJ
    assert pool_width == tn and qk_width % tn == 0
    tm = min(TM_INPROJ, n)
    n_qk_tiles = 2 * qk_width // tn
    n_gate_tiles = (w_uqkg.shape[1] - pool_width - 2 * qk_width) // tn
    n_col = 1 + n_qk_tiles + n_gate_tiles
    rows = lax.broadcasted_iota(I32, (MXU_DIM, MXU_DIM), 0) // head_dim
    cols = lax.broadcasted_iota(I32, (MXU_DIM, MXU_DIM), 1) // head_dim
    bd = (rows == cols).astype(BF16)
    kern = functools.partial(_inproj_kernel, n_qk_tiles=n_qk_tiles, head_dim=head_dim)
    return pl.pallas_call(
        kern,
        grid=(n // tm, n_col),
        in_specs=[
            pl.BlockSpec((tm, d), lambda i, j: (i, 0)),
            pl.BlockSpec((d, tn), lambda i, j: (0, j)),
            pl.BlockSpec((1, tn), lambda i, j: (0, jnp.clip(j - 1, 0, n_qk_tiles - 1))),
            pl.BlockSpec((MXU_DIM, MXU_DIM), lambda i, j: (0, 0)),
        ],
        out_specs=[
            pl.BlockSpec((tm, tn), lambda i, j: (i, 0)),
            pl.BlockSpec((tm, tn), lambda i, j: (i, jnp.clip(j - 1, 0, n_qk_tiles - 1))),
            pl.BlockSpec((tm, tn), lambda i, j: (i, jnp.clip(j - 1 - n_qk_tiles, 0, n_gate_tiles - 1))),
        ],
        out_shape=[
            jax.ShapeDtypeStruct((n, pool_width), F32),
            jax.ShapeDtypeStruct((n, 2 * qk_width), BF16),
            jax.ShapeDtypeStruct((n, n_gate_tiles * tn), F32),
        ],
        compiler_params=_cparams("parallel", "arbitrary"),
    )(xn, w_uqkg, qk_gain, bd)


def _vt_kernel(w_ref, x_ref, o_ref):
    o_ref[...] = lax.dot_general(w_ref[...], x_ref[...], (((1,), (1,)), ((), ())),
                                 preferred_element_type=F32).astype(o_ref.dtype)


def _v_transposed(xn, w_vt):
    n, d = xn.shape
    aw = w_vt.shape[0]
    tm = min(TM_INPROJ, n)
    return pl.pallas_call(
        _vt_kernel,
        grid=(n // tm,),
        in_specs=[pl.BlockSpec((aw, d), lambda i: (0, 0)), pl.BlockSpec((tm, d), lambda i: (i, 0))],
        out_specs=pl.BlockSpec((aw, tm), lambda i: (0, i)),
        out_shape=jax.ShapeDtypeStruct((aw, n), BF16),
        compiler_params=_cparams("parallel"),
    )(w_vt, xn)


def _pool_kernel(cur_ref, prev_ref, next_ref, pw_ref, ps_ref, o_ref, ext_hi, ext_lo, *, seq, ts, group):
    i = pl.program_id(1)
    pad = POOL_PAD
    halo = prev_ref.shape[1]
    cur = cur_ref[0]
    width = cur.shape[1]
    zeros = jnp.zeros((pad - halo, width), BF16)
    for ext, part in ((ext_hi, 0), (ext_lo, 1)):
        ext[0:pad - halo, :] = zeros
        ext[pad - halo:pad, :] = _split_bf16(prev_ref[0])[part]
        ext[pad:pad + ts, :] = _split_bf16(cur)[part]
        ext[pad + ts:pad + ts + halo, :] = _split_bf16(next_ref[0])[part]
        ext[pad + ts + halo:, :] = zeros
    t = i * ts + lax.broadcasted_iota(I32, (ts, ts + 2 * pad), 0)
    p = i * ts - pad + lax.broadcasted_iota(I32, (ts, ts + 2 * pad), 1)
    t_col = i * ts + lax.broadcasted_iota(I32, (ts, 1), 0)
    for gi, w in enumerate(POOL_WINDOWS):
        lo = jnp.maximum(t - w // 2, 0)
        hi = jnp.minimum(t + (w - w // 2), seq)
        band = ((p >= lo) & (p < hi)).astype(BF16)
        cnt = (jnp.minimum(t_col + (w - w // 2), seq) - jnp.maximum(t_col - w // 2, 0)).astype(F32)
        ch = slice(gi * group, (gi + 1) * group)
        wsum = (jnp.dot(band, ext_hi[:, ch], preferred_element_type=F32)
                + jnp.dot(band, ext_lo[:, ch], preferred_element_type=F32))
        pooled = wsum / cnt - cur[:, ch]
        y = jnp.dot(pooled.astype(BF16), pw_ref[gi], preferred_element_type=F32) * ps_ref[:, ch]
        o_ref[0, :, ch] = y.astype(o_ref.dtype)


def _pool_mixer(u3, pool_w, pool_scale):
    b, s, pwid = u3.shape
    ng, group, _ = pool_w.shape
    assert ng == len(POOL_WINDOWS) and ng * group == pwid
    ts = min(TS_POOL, s)
    halo = 16
    assert max(POOL_WINDOWS) // 2 <= halo and s % ts == 0 and ts % halo == 0
    nblk = ts // halo
    last = s // halo - 1
    kern = functools.partial(_pool_kernel, seq=s, ts=ts, group=group)
    return pl.pallas_call(
        kern,
        grid=(b, s // ts),
        in_specs=[
            pl.BlockSpec((1, ts, pwid), lambda bi, i: (bi, i, 0)),
            pl.BlockSpec((1, halo, pwid), lambda bi, i: (bi, jnp.maximum(i * nblk - 1, 0), 0)),
            pl.BlockSpec((1, halo, pwid), lambda bi, i: (bi, jnp.minimum((i + 1) * nblk, last), 0)),
            pl.BlockSpec((ng, group, group), lambda bi, i: (0, 0, 0)),
            pl.BlockSpec((1, pwid), lambda bi, i: (0, 0)),
        ],
        out_specs=pl.BlockSpec((1, ts, pwid), lambda bi, i: (bi, i, 0)),
        out_shape=jax.ShapeDtypeStruct((b, s, pwid), BF16),
        scratch_shapes=[pltpu.VMEM((ts + 2 * POOL_PAD, pwid), BF16), pltpu.VMEM((ts + 2 * POOL_PAD, pwid), BF16)],
        compiler_params=_cparams("parallel", "parallel"),
    )(u3, u3, u3, pool_w, pool_scale.reshape(1, pwid))


def _bucket_thresholds(num_buckets):
    half = num_buckets // 2
    max_exact = half // 2
    ratio = MAX_DISTANCE // max_exact
    assert ratio * max_exact == MAX_DISTANCE
    steps = half - max_exact
    thr = []
    for jj in range(1, steps):
        n = max_exact
        while n ** steps < (max_exact ** steps) * (ratio ** jj):
            n += 1
        thr.append(n)
    return half, max_exact, thr


def _bias_kernel(tab_ref, o_ref, *, tile, num_buckets):
    h = pl.program_id(0)
    delta = pl.program_id(1) - 2
    half, max_exact, thr = _bucket_thresholds(num_buckets)
    kk = lax.broadcasted_iota(I32, (tile, tile), 0)
    qq = lax.broadcasted_iota(I32, (tile, tile), 1)
    rel = delta * tile + kk - qq
    n = jnp.abs(rel)
    large = jnp.full((tile, tile), max_exact, I32)
    for th in thr:
        large = large + (n >= th).astype(I32)
    bucket = jnp.where(n < max_exact, n, large) + jnp.where(rel > 0, half, 0)
    out = jnp.zeros((tile, tile), F32)
    for bkt in range(num_buckets):
        out = jnp.where(bucket == bkt, tab_ref[bkt, h], out)
    o_ref[0, 0] = out * LOG2E


def _bias_tiles(rel_bias, tile):
    nb, nh = rel_bias.shape
    assert tile >= MAX_DISTANCE
    kern = functools.partial(_bias_kernel, tile=tile, num_buckets=nb)
    return pl.pallas_call(
        kern,
        grid=(nh, 5),
        in_specs=[pl.BlockSpec(memory_space=pltpu.SMEM)],
        out_specs=pl.BlockSpec((1, 1, tile, tile), lambda h, dd: (h, dd, 0, 0)),
        out_shape=jax.ShapeDtypeStruct((nh, 5, tile, tile), F32),
        compiler_params=_cparams("parallel", "parallel"),
    )(rel_bias)


def _attn_kernel(tab_ref, q_ref, k_ref, vt_ref, b_ref, lqk_ref, sg_ref, o_ref,
                 s_sc, mb_sc, off_sc, m_sc, acc_sc, *, head_dim, lam_init, nblk):
    h = pl.program_id(1)
    i = pl.program_id(2)
    tile = q_ref.shape[1]
    vdim = vt_ref.shape[0]
    nsteps = nblk // 2
    q = q_ref[0]
    lane = lax.broadcasted_iota(I32, (tile, q.shape[1]), 1)
    ones = jnp.ones((acc_sc.shape[1] - vdim, tile), BF16)
    nb = tab_ref.shape[0]

    def start(blk):
        return blk * tile if isinstance(blk, int) else pl.multiple_of(blk * tile, tile)

    def scores(slot, blk, far):
        k = k_ref[0, pl.ds(start(blk), tile), :]
        if far:
            off = jnp.where(blk < i, tab_ref[nb // 2 - 1, h], tab_ref[nb - 1, h]) * LOG2E
        else:
            off = 0.0
            bias = b_ref[0, jnp.clip(blk - i, -2, 2) + 2]
        off_sc[slot] = jnp.full((1, tile), off, F32)
        for c in range(2):
            kc = jnp.where((lane >= c * head_dim) & (lane < (c + 1) * head_dim), k, jnp.zeros_like(k))
            s = lax.dot_general(kc, q, (((1,), (1,)), ((), ())), preferred_element_type=F32)
            if not far:
                s = s + bias
            s_sc[slot, c] = s
            mb_sc[slot, c] = jnp.max(s, axis=0, keepdims=True) + off

    def consume(slot, blk):
        vt = vt_ref[:, pl.ds(start(blk), tile)]
        vt_ones = jnp.concatenate([vt, ones], axis=0)
        for c in range(2):
            m_prev = m_sc[c]
            m_new = jnp.maximum(m_prev, mb_sc[slot, c])
            alpha = jnp.exp2(m_prev - m_new)
            p = jnp.exp2(s_sc[slot, c] - (m_new - off_sc[slot]))
            acc_sc[c] = alpha * acc_sc[c] + jnp.dot(vt_ones, p.astype(BF16), preferred_element_type=F32)
            m_sc[c] = m_new

    m_sc[...] = jnp.full(m_sc.shape, -1e30, F32)
    acc_sc[...] = jnp.zeros(acc_sc.shape, F32)
    scores(0, 0, False)

    def pairs(j0, npairs):
        first = 2 * j0 + 1
        last = 2 * (j0 + npairs)
        all_far = (first - i >= 2) | (i - last >= 2)

        def run(far):
            for jj in range(npairs):
                j = j0 + jj
                scores(1, 2 * j + 1, far)
                consume(0, 2 * j)
                scores(0, 2 * j + 2, far)
                consume(1, 2 * j + 1)

        @pl.when(all_far)
        def _():
            run(True)

        @pl.when(jnp.logical_not(all_far))
        def _():
            run(False)

    group = PAIRS_PER_TRIP
    ngroups = (nsteps - 1) // group

    def trip(g, carry):
        pairs(g * group, group)
        return carry

    lax.fori_loop(0, ngroups, trip, 0)
    if (nsteps - 1) % group:
        pairs(ngroups * group, (nsteps - 1) % group)
    scores(1, nblk - 1, False)
    consume(0, nblk - 2)
    consume(1, nblk - 1)

    lq = lqk_ref[...]
    lam = (jnp.exp(jnp.sum(lq[0:1] * lq[1:2], axis=1, keepdims=True))
           - jnp.exp(jnp.sum(lq[2:3] * lq[3:4], axis=1, keepdims=True)) + lam_init)
    a0 = acc_sc[0]
    a1 = acc_sc[1]
    o = a0[:vdim] / a0[vdim:vdim + 1] - lam * (a1[:vdim] / a1[vdim:vdim + 1])
    ms = jnp.mean(o * o, axis=0, keepdims=True)
    y = o * lax.rsqrt(ms + EPS) * sg_ref[...] * (1.0 - lam_init)
    o_ref[0] = y.T.astype(o_ref.dtype)


def _diff_attention(qk3, vt, bias_tiles, rel_bias, lambda_qk, subln_g, head_dim, lam_init):
    b, s, two_qk = qk3.shape
    nh = bias_tiles.shape[0]
    tile = bias_tiles.shape[2]
    vdim = vt.shape[0] // nh
    assert vdim == 2 * head_dim == LANES and two_qk == 2 * nh * LANES and s % (2 * tile) == 0
    nblk = s // tile
    ones_rows = 16
    kern = functools.partial(_attn_kernel, head_dim=head_dim, lam_init=lam_init, nblk=nblk)
    return pl.pallas_call(
        kern,
        grid=(b, nh, nblk),
        in_specs=[
            pl.BlockSpec(memory_space=pltpu.SMEM),
            pl.BlockSpec((1, tile, LANES), lambda bi, h, i: (bi, i, h)),
            pl.BlockSpec((1, s, LANES), lambda bi, h, i: (bi, 0, nh + h)),
            pl.BlockSpec((vdim, s), lambda bi, h, i: (h, bi)),
            pl.BlockSpec((1, 5, tile, tile), lambda bi, h, i: (h, 0, 0, 0)),
            pl.BlockSpec(lambda_qk.shape, lambda bi, h, i: (0, 0)),
            pl.BlockSpec((vdim, 1), lambda bi, h, i: (0, 0)),
        ],
        out_specs=pl.BlockSpec((1, tile, vdim), lambda bi, h, i: (bi, i, h)),
        out_shape=jax.ShapeDtypeStruct((b, s, nh * vdim), BF16),
        scratch_shapes=[
            pltpu.VMEM((2, 2, tile, tile), F32),
            pltpu.VMEM((2, 2, 1, tile), F32),
            pltpu.VMEM((2, 1, tile), F32),
            pltpu.VMEM((2, 1, tile), F32),
            pltpu.VMEM((2, vdim + ones_rows, tile), F32),
        ],
        compiler_params=_cparams("parallel", "parallel", "parallel"),
    )(rel_bias, qk3, qk3, vt, bias_tiles, lambda_qk, subln_g.reshape(vdim, 1))


def _sigmoid(x):
    return 1.0 / (1.0 + jnp.exp(-x))


def _merge_kernel(pm_ref, att_ref, gp_ref, ga_ref, wp_ref, wa_ref, o_ref):
    pool_out = jnp.dot(pm_ref[...], wp_ref[...], preferred_element_type=F32)
    attn_out = jnp.dot(att_ref[...], wa_ref[...], preferred_element_type=F32)
    o_ref[...] = (_sigmoid(gp_ref[...]) * pool_out + _sigmoid(ga_ref[...]) * attn_out).astype(o_ref.dtype)


def _merge(pm, att, gates, w_pool_out, w_attn_out):
    n, pwid = pm.shape
    aw = att.shape[1]
    d = w_pool_out.shape[1]
    tm = min(TM_MERGE, n)
    return pl.pallas_call(
        _merge_kernel,
        grid=(n // tm,),
        in_specs=[
            pl.BlockSpec((tm, pwid), lambda i: (i, 0)),
            pl.BlockSpec((tm, aw), lambda i: (i, 0)),
            pl.BlockSpec((tm, d), lambda i: (i, 0)),
            pl.BlockSpec((tm, d), lambda i: (i, 1)),
            pl.BlockSpec((pwid, d), lambda i: (0, 0)),
            pl.BlockSpec((aw, d), lambda i: (0, 0)),
        ],
        out_specs=pl.BlockSpec((tm, d), lambda i: (i, 0)),
        out_shape=jax.ShapeDtypeStruct((n, d), BF16),
        compiler_params=_cparams("parallel"),
    )(pm, att, gates, gates, w_pool_out, w_attn_out)


def _oproj_kernel(m_ref, x_ref, wo_ref, g2_ref, wr_ref, h_ref, t_ref, aff_ref):
    h = x_ref[...] + jnp.dot(m_ref[...], wo_ref[...], preferred_element_type=F32)
    h_ref[...] = h
    ms = jnp.mean(h * h, axis=-1, keepdims=True)
    t = h * lax.rsqrt(ms + EPS) * g2_ref[...]
    half = t.shape[1] // 2
    bits = lax.bitcast_convert_type(t.astype(BF16).astype(F32), U32)
    t_ref[...] = bits[:, :half] | (bits[:, half:] >> 16)
    t_hi, t_lo = _split_bf16(t)
    w2 = wr_ref[...]
    p_hi = jnp.dot(t_hi, w2, preferred_element_type=F32)
    p_lo = jnp.dot(t_lo, w2, preferred_element_type=F32)
    logits = (p_hi[:, :LANES] + p_hi[:, LANES:]) + p_lo[:, :LANES]
    ne = aff_ref.shape[0]
    lt = logits.T[:ne]
    e = jnp.exp(lt - jnp.max(lt, axis=0, keepdims=True))
    aff_ref[...] = e / jnp.sum(e, axis=0, keepdims=True)


def _oproj_router(merged, x2d, w_o, norm2_g, w_router):
    n, d = x2d.shape
    ne = w_router.shape[1]
    assert ne <= LANES
    w_hi, w_lo = _split_bf16(w_router)
    w2 = jnp.zeros((d, 2 * LANES), BF16).at[:, :ne].set(w_hi).at[:, LANES:LANES + ne].set(w_lo)
    tm = min(TM_OPROJ, n)
    return pl.pallas_call(
        _oproj_kernel,
        grid=(n // tm,),
        in_specs=[
            pl.BlockSpec((tm, d), lambda i: (i, 0)),
            pl.BlockSpec((tm, d), lambda i: (i, 0)),
            pl.BlockSpec((d, d), lambda i: (0, 0)),
            pl.BlockSpec((1, d), lambda i: (0, 0)),
            pl.BlockSpec((d, 2 * LANES), lambda i: (0, 0)),
        ],
        out_specs=[
            pl.BlockSpec((tm, d), lambda i: (i, 0)),
            pl.BlockSpec((tm, d // 2), lambda i: (i, 0)),
            pl.BlockSpec((ne, tm), lambda i: (0, i)),
        ],
        out_shape=[
            jax.ShapeDtypeStruct((n, d), F32),
            jax.ShapeDtypeStruct((n, d // 2), U32),
            jax.ShapeDtypeStruct((ne, n), F32),
        ],
        compiler_params=_cparams("parallel"),
    )(merged, x2d, w_o, norm2_g.reshape(1, d), w2)


def _route_kernel(a_ref, idx_ref, gate_ref, *, cap):
    a = a_ref[0]
    nc, wid = a.shape
    keys = lax.bitcast_convert_type(a, I32)

    def search(bit, thr):
        cand = thr | jnp.left_shift(jnp.int32(1), 30 - bit)
        cnt = jnp.sum(jnp.sum((keys >= cand).astype(I32), axis=1, keepdims=True), axis=0, keepdims=True)
        return jnp.where(cnt >= cap, cand, thr)

    thr = lax.fori_loop(0, 31, search, jnp.zeros((1, 1), I32))
    gt = keys > thr
    eq = keys == thr
    n_gt = jnp.sum(jnp.sum(gt.astype(F32), axis=1, keepdims=True), axis=0, keepdims=True)
    need = cap - n_gt

    upper = (lax.broadcasted_iota(I32, (wid, wid), 0) <= lax.broadcasted_iota(I32, (wid, wid), 1)).astype(BF16)
    lower = (lax.broadcasted_iota(I32, (nc, nc), 1) < lax.broadcasted_iota(I32, (nc, nc), 0)).astype(BF16)

    def prefix(mask):
        inc = jnp.dot(mask.astype(BF16), upper, preferred_element_type=F32)
        tot = inc[:, wid - 1:wid]
        off = jnp.dot(lower, jnp.broadcast_to(tot, (nc, LANES)).astype(BF16), preferred_element_type=F32)[:, 0:1]
        return inc, tot, off

    inc_e, _, off_e = prefix(eq)
    tie_rank = off_e + inc_e - eq.astype(F32)
    sel = gt | (eq & (tie_rank < need))
    inc_s, tot_s, off_s = prefix(sel)

    slot = lax.broadcasted_iota(I32, (1, cap), 1).astype(F32)
    chunk_end = off_s + tot_s
    k_row = jnp.sum((chunk_end <= slot).astype(F32), axis=0, keepdims=True)
    onehot = lax.broadcasted_iota(I32, (nc, cap), 0).astype(F32) == k_row
    off_k = jnp.sum(jnp.where(onehot, off_s, 0.0), axis=0, keepdims=True)
    rank = slot - off_k
    onehot_b = onehot.astype(BF16)
    inc_of_slot = jnp.dot(inc_s.T.astype(BF16), onehot_b, preferred_element_type=F32)
    local = jnp.sum((inc_of_slot <= rank).astype(F32), axis=0, keepdims=True)
    idx_ref[0] = (k_row * wid + local).astype(I32)

    at = a.T
    a1 = at.astype(BF16)
    r1 = at - a1.astype(F32)
    a2 = r1.astype(BF16)
    a3 = (r1 - a2.astype(F32)).astype(BF16)
    aff_of_slot = ((jnp.dot(a1, onehot_b, preferred_element_type=F32) + jnp.dot(a2, onehot_b, preferred_element_type=F32))
                   + jnp.dot(a3, onehot_b, preferred_element_type=F32))
    pick = lax.broadcasted_iota(I32, (wid, cap), 0).astype(F32) == local
    gate_ref[0] = jnp.sum(jnp.where(pick, aff_of_slot, 0.0), axis=0, keepdims=True)


def _route(aff_t, cap):
    ne, n = aff_t.shape
    wid = ROUTE_CHUNK
    assert n % wid == 0
    nc = n // wid
    a3 = aff_t.reshape(ne, nc, wid)
    kern = functools.partial(_route_kernel, cap=cap)
    return pl.pallas_call(
        kern,
        grid=(ne,),
        in_specs=[pl.BlockSpec((1, nc, wid), lambda e: (e, 0, 0))],
        out_specs=[pl.BlockSpec((1, 1, cap), lambda e: (e, 0, 0)), pl.BlockSpec((1, 1, cap), lambda e: (e, 0, 0))],
        out_shape=[jax.ShapeDtypeStruct((ne, 1, cap), I32), jax.ShapeDtypeStruct((ne, 1, cap), F32)],
        compiler_params=_cparams("parallel"),
    )(a3)


def _row_copy(src_hbm, tok, buf, r, sem):
    return pltpu.make_async_copy(src_hbm.at[pl.ds(tok, 1)], buf.at[pl.ds(r, 1)], sem)


def _gather_kernel(idx_ref, idx_next_ref, t_hbm, o_ref, buf, sem, *, rows, total):
    step = pl.program_id(0) * pl.num_programs(1) + pl.program_id(1)
    slot = step % 2

    def fetch(iref, s):
        def body(r, carry):
            _row_copy(t_hbm, iref[0, 0, r], buf.at[s], r, sem.at[s]).start()
            return carry
        lax.fori_loop(0, rows, body, 0, unroll=DMA_UNROLL)

    @pl.when(step == 0)
    def _():
        fetch(idx_ref, 0)

    @pl.when(step + 1 < total)
    def _():
        fetch(idx_next_ref, 1 - slot)

    def fetched(r, carry):
        _row_copy(t_hbm, 0, buf.at[slot], r, sem.at[slot]).wait()
        return carry

    lax.fori_loop(0, rows, fetched, 0, unroll=DMA_UNROLL)
    o_ref[0] = buf[slot]


def _gather_rows(t, idx):
    n, d = t.shape
    ne, _, cap = idx.shape
    rows = min(R_GATHER, cap)
    steps = cap // rows
    total = ne * steps
    idx3 = idx.reshape(total, 1, rows)
    kern = functools.partial(_gather_kernel, rows=rows, total=total)
    return pl.pallas_call(
        kern,
        grid=(ne, steps),
        in_specs=[
            pl.BlockSpec((1, 1, rows), lambda e, c: (e * steps + c, 0, 0), memory_space=pltpu.SMEM),
            pl.BlockSpec((1, 1, rows), lambda e, c: (jnp.minimum(e * steps + c + 1, total - 1), 0, 0),
                         memory_space=pltpu.SMEM),
            pl.BlockSpec(memory_space=pl.ANY),
        ],
        out_specs=pl.BlockSpec((1, rows, d), lambda e, c: (e, c, 0)),
        out_shape=jax.ShapeDtypeStruct((ne, cap, d), t.dtype),
        scratch_shapes=[pltpu.VMEM((2, rows, d), t.dtype), pltpu.SemaphoreType.DMA((2,))],
        compiler_params=_cparams("arbitrary", "arbitrary"),
    )(idx3, idx3, t)


def _gather_rows_sc(t, idx):
    n, w = t.shape
    ne, _, cap = idx.shape
    ppr = w // SC_PIECE
    total = ne * cap * ppr
    piece_idx = (idx.reshape(-1, 1) * ppr + jnp.arange(ppr, dtype=I32)).reshape(1, total)
    pieces = t.reshape(n * ppr, SC_PIECE)
    mesh = plsc.VectorSubcoreMesh(core_axis_name="core", subcore_axis_name="subcore")

    @pl.kernel(out_type=jax.ShapeDtypeStruct((total, SC_PIECE), t.dtype), mesh=mesh)
    def gather(x_hbm, i_hbm, o_hbm):
        def body(i_vmem, o_vmem):
            pltpu.sync_copy(x_hbm.at[i_vmem.at[0]], o_vmem)

        pltpu.emit_pipeline(
            body,
            grid=(total // SC_WINDOW,),
            in_specs=[pl.BlockSpec((1, SC_WINDOW), index_map=lambda i: (0, i))],
            out_specs=[pl.BlockSpec((SC_WINDOW, SC_PIECE), index_map=lambda i: (i, 0))],
            core_axis_name=("core", "subcore"),
            dimension_semantics=(pltpu.PARALLEL,),
        )(i_hbm, o_hbm)

    return gather(pieces, piece_idx).reshape(ne, cap, w)


def _ffn_kernel(x_ref, wg_ref, wu_ref, wd_ref, gate_ref, o_ref, acc, xb, *, nf):
    f = pl.program_id(2)

    @pl.when(f == 0)
    def _():
        acc[...] = jnp.zeros(acc.shape, F32)
        words = x_ref[0]
        half = words.shape[1]
        xb[:, :half] = lax.bitcast_convert_type(words & jnp.uint32(0xFFFF0000), F32).astype(BF16)
        xb[:, half:] = lax.bitcast_convert_type(words << 16, F32).astype(BF16)

    x = xb[...]
    a = jnp.dot(x, wg_ref[0], preferred_element_type=F32)
    b = jnp.dot(x, wu_ref[0], preferred_element_type=F32)
    hid = (a * _sigmoid(a)) * b
    acc[...] += jnp.dot(hid.astype(BF16), wd_ref[0], preferred_element_type=F32)

    @pl.when(f == nf - 1)
    def _():
        o_ref[0] = acc[...] * gate_ref[0]


def _expert_ffn(xe, w_gate, w_up, w_down, gate_col):
    ne, cap, words = xe.shape
    d = 2 * words
    dff = w_gate.shape[2]
    tm = min(TM_FFN, cap)
    tf = min(TF_FFN, dff)
    nf = dff // tf
    kern = functools.partial(_ffn_kernel, nf=nf)
    return pl.pallas_call(
        kern,
        grid=(ne, cap // tm, nf),
        in_specs=[
            pl.BlockSpec((1, tm, words), lambda e, c, f: (e, c, 0)),
            pl.BlockSpec((1, d, tf), lambda e, c, f: (e, 0, f)),
            pl.BlockSpec((1, d, tf), lambda e, c, f: (e, 0, f)),
            pl.BlockSpec((1, tf, d), lambda e, c, f: (e, f, 0)),
            pl.BlockSpec((1, tm, 1), lambda e, c, f: (e, c, 0)),
        ],
        out_specs=pl.BlockSpec((1, tm, d), lambda e, c, f: (e, c, 0)),
        out_shape=jax.ShapeDtypeStruct((ne, cap, d), F32),
        scratch_shapes=[pltpu.VMEM((tm, d), F32), pltpu.VMEM((tm, d), BF16)],
        compiler_params=_cparams("parallel", "parallel", "arbitrary"),
    )(xe, w_gate, w_up, w_down, gate_col)


def _scatter_kernel(idx_ref, idx_next_ref, ye_ref, h_hbm, y_hbm, buf, sem_in, sem_out, *, rows, steps):
    del h_hbm
    c = pl.program_id(1)
    slot = c % 2

    def fetch(iref, s):
        def body(r, carry):
            _row_copy(y_hbm, iref[0, 0, r], buf.at[s], r, sem_in.at[s]).start()
            return carry
        lax.fori_loop(0, rows, body, 0, unroll=DMA_UNROLL)

    def row_put(tok, s, r):
        return pltpu.make_async_copy(buf.at[s].at[pl.ds(r, 1)], y_hbm.at[pl.ds(tok, 1)], sem_out.at[s])

    def put_done(s):
        def body(r, carry):
            row_put(0, s, r).wait()
            return carry
        lax.fori_loop(0, rows, body, 0, unroll=DMA_UNROLL)

    @pl.when(c == 0)
    def _():
        fetch(idx_ref, 0)

    @pl.when(c >= 1)
    def _():
        put_done(1 - slot)

    @pl.when(c + 1 < steps)
    def _():
        fetch(idx_next_ref, 1 - slot)

    def fetched(r, carry):
        _row_copy(y_hbm, 0, buf.at[slot], r, sem_in.at[slot]).wait()
        return carry

    lax.fori_loop(0, rows, fetched, 0, unroll=DMA_UNROLL)
    buf[slot] = buf[slot] + ye_ref[0]

    def put(r, carry):
        row_put(idx_ref[0, 0, r], slot, r).start()
        return carry

    lax.fori_loop(0, rows, put, 0, unroll=DMA_UNROLL)

    @pl.when(c == steps - 1)
    def _():
        put_done(slot)


def _scatter_add(h, ye, idx):
    n, d = h.shape
    ne, _, cap = idx.shape
    rows = min(R_GATHER, cap)
    steps = cap // rows
    total = ne * steps
    idx3 = idx.reshape(total, 1, rows)
    kern = functools.partial(_scatter_kernel, rows=rows, steps=steps)
    return pl.pallas_call(
        kern,
        grid=(ne, steps),
        in_specs=[
            pl.BlockSpec((1, 1, rows), lambda e, c: (e * steps + c, 0, 0), memory_space=pltpu.SMEM),
            pl.BlockSpec((1, 1, rows), lambda e, c: (jnp.minimum(e * steps + c + 1, total - 1), 0, 0),
                         memory_space=pltpu.SMEM),
            pl.BlockSpec((1, rows, d), lambda e, c: (e, c, 0)),
            pl.BlockSpec(memory_space=pl.ANY),
        ],
        out_specs=pl.BlockSpec(memory_space=pl.ANY),
        out_shape=jax.ShapeDtypeStruct((n, d), F32),
        input_output_aliases={3: 0},
        scratch_shapes=[pltpu.VMEM((2, rows, d), F32), pltpu.SemaphoreType.DMA((2,)), pltpu.SemaphoreType.DMA((2,))],
        compiler_params=_cparams("arbitrary", "arbitrary"),
    )(idx3, idx3, ye, h)


def _prepare_weights(l, norm1_g, w_in, pool_w, pool_scale, w_pool_out, q_norm_g, k_norm_g, rel_bias, lambda_qk, subln_g,
                     w_attn_out, w_o, norm2_g, w_router, w_gate, w_up, w_down):
    head_dim = q_norm_g.shape[1]
    nh = rel_bias.shape[1]
    vdim = subln_g.shape[1]
    pool_width = pool_scale.shape[1]
    qk_width = nh * 2 * head_dim
    attn_width = nh * vdim
    o1, o2, o3 = pool_width, pool_width + 2 * qk_width, pool_width + 2 * qk_width + attn_width
    w = w_in[l]
    reps = qk_width // head_dim
    qk_gain = jnp.concatenate([jnp.tile(q_norm_g[l] * (head_dim ** -0.5) * LOG2E, reps), jnp.tile(k_norm_g[l], reps)])
    return dict(
        head_dim=head_dim, nh=nh, vdim=vdim, pool_width=pool_width, qk_width=qk_width,
        lam_init=0.8 - 0.6 * math.exp(-0.3 * l),
        norm1_g=norm1_g[l],
        w_uqkg=jnp.concatenate([w[:, :o2], w[:, o3:]], axis=1).astype(BF16),
        w_vt=w[:, o2:o3].T.astype(BF16),
        qk_gain=qk_gain.reshape(1, -1).astype(F32),
        pool_w=pool_w[l].astype(BF16), pool_scale=pool_scale[l],
        w_pool_out=w_pool_out[l].astype(BF16), w_attn_out=w_attn_out[l].astype(BF16), w_o=w_o[l].astype(BF16),
        rel_bias=rel_bias, lambda_qk=lambda_qk[l], subln_g=subln_g[l], norm2_g=norm2_g[l],
        w_router=w_router[l],
        w_gate=w_gate[l].astype(BF16), w_up=w_up[l].astype(BF16), w_down=w_down[l].astype(BF16),
    )


def _layer(x, p, bias_tiles):
    b, s, d = x.shape
    n = b * s
    x2d = x.reshape(n, d)
    xn = _rmsnorm(x2d, p["norm1_g"])
    u, qk, gates = _inproj(xn, p["w_uqkg"], p["qk_gain"], p["head_dim"], p["pool_width"], p["qk_width"])
    vt = _v_transposed(xn, p["w_vt"])
    pm = _pool_mixer(u.reshape(b, s, -1), p["pool_w"], p["pool_scale"]).reshape(n, -1)
    att = _diff_attention(qk.reshape(b, s, -1), vt, bias_tiles, p["rel_bias"], p["lambda_qk"], p["subln_g"],
                          p["head_dim"], p["lam_init"]).reshape(n, -1)
    merged = _merge(pm, att, gates, p["w_pool_out"], p["w_attn_out"])
    h, t, aff_t = _oproj_router(merged, x2d, p["w_o"], p["norm2_g"], p["w_router"])
    ne = aff_t.shape[0]
    cap = max(1, (CAPACITY_FACTOR * n) // ne)
    idx, gate = _route(aff_t, cap)
    xe = _gather_rows_sc(t, idx)
    ye = _expert_ffn(xe, p["w_gate"], p["w_up"], p["w_down"], gate.reshape(ne, cap, 1))
    y = _scatter_add(h, ye, idx)
    return y.reshape(b, s, d)


def kernel(x_prompt, x_sample, norm1_g, w_in, pool_w, pool_scale, w_pool_out, q_norm_g, k_norm_g, rel_bias, lambda_qk,
           subln_g, w_attn_out, w_o, norm2_g, w_router, w_gate, w_up, w_down):
    tile = min(T_ATTN, x_prompt.shape[1], x_sample.shape[1])
    bias_tiles = _bias_tiles(rel_bias, tile)
    hp, hs = x_prompt, x_sample
    for l in range(norm1_g.shape[0]):
        p = _prepare_weights(l, norm1_g, w_in, pool_w, pool_scale, w_pool_out, q_norm_g, k_norm_g, rel_bias, lambda_qk,
                             subln_g, w_attn_out, w_o, norm2_g, w_router, w_gate, w_up, w_down)
        hp = _layer(hp, p, bias_tiles)
        hs = _layer(hs, p, bias_tiles)
    return (hp, hs)
```

```python
import functools
import math

import jax
import jax.numpy as jnp
from jax import lax
from jax.experimental import pallas as pl
from jax.experimental.pallas import tpu as pltpu
from jax.experimental.pallas import tpu_sc as plsc

F32, BF16, I32, U32 = jnp.float32, jnp.bfloat16, jnp.int32, jnp.uint32

EPS = 1e-6
POOL_WINDOWS = (2, 4, 8, 16)
MAX_DISTANCE = 128
CAPACITY_FACTOR = 2
LOG2E = math.log2(math.e)

V7X_VMEM_BYTES = 64 * 1024 * 1024
VMEM_LIMIT_BYTES = V7X_VMEM_BYTES - 8 * 1024 * 1024
LANES = 128
MXU_DIM = 256

TM_NORM = 512
TM_INPROJ = 1024
TN_INPROJ = 1024
TS_POOL = 512
POOL_PAD = 128
T_ATTN = 512
PAIRS_PER_TRIP = 2
TM_MERGE = 256
TM_OPROJ = 256
ROUTE_CHUNK = 256
R_GATHER = 512
DMA_UNROLL = 8
SC_PIECE = 256
SC_WINDOW = 128
TM_FFN = 1024
TF_FFN = 512


def _cparams(*sem):
    return pltpu.CompilerParams(dimension_semantics=sem, vmem_limit_bytes=VMEM_LIMIT_BYTES)


def _split_bf16(a):
    hi = a.astype(BF16)
    lo = (a - hi.astype(F32)).astype(BF16)
    return hi, lo


def _rmsnorm_kernel(x_ref, g_ref, o_ref):
    x = x_ref[...]
    ms = jnp.mean(x * x, axis=-1, keepdims=True)
    o_ref[...] = (x * lax.rsqrt(ms + EPS) * g_ref[...]).astype(o_ref.dtype)


def _rmsnorm(x2d, g):
    n, d = x2d.shape
    tm = min(TM_NORM, n)
    return pl.pallas_call(
        _rmsnorm_kernel,
        grid=(n // tm,),
        in_specs=[pl.BlockSpec((tm, d), lambda i: (i, 0)), pl.BlockSpec((1, d), lambda i: (0, 0))],
        out_specs=pl.BlockSpec((tm, d), lambda i: (i, 0)),
        out_shape=jax.ShapeDtypeStruct((n, d), BF16),
        compiler_params=_cparams("parallel"),
    )(x2d, g.reshape(1, d))


def _inproj_kernel(x_ref, w_ref, gain_ref, bd_ref, u_ref, qk_ref, g_ref, *, n_qk_tiles, head_dim):
    j = pl.program_id(1)
    acc = jnp.dot(x_ref[...], w_ref[...], preferred_element_type=F32)

    @pl.when(j == 0)
    def _():
        u_ref[...] = acc

    @pl.when((j >= 1) & (j <= n_qk_tiles))
    def _():
        bd = bd_ref[...]
        for c0 in range(0, acc.shape[1], MXU_DIM):
            a = acc[:, c0:c0 + MXU_DIM]
            hi, lo = _split_bf16(a * a)
            ss = jnp.dot(hi, bd, preferred_element_type=F32) + jnp.dot(lo, bd, preferred_element_type=F32)
            y = a * lax.rsqrt(ss * (1.0 / head_dim) + EPS) * gain_ref[:, c0:c0 + MXU_DIM]
            qk_ref[:, c0:c0 + MXU_DIM] = y.astype(qk_ref.dtype)

    @pl.when(j > n_qk_tiles)
    def _():
        g_ref[...] = acc


def _inproj(xn, w_uqkg, qk_gain, head_dim, pool_width, qk_width):
    n, d = xn.shape
    tn = TN_INPROJ
    assert pool_width == tn and qk_width % tn == 0
    tm = min(TM_INPROJ, n)
    n_qk_tiles = 2 * qk_width // tn
    n_gate_tiles = (w_uqkg.shape[1] - pool_width - 2 * qk_width) // tn
    n_col = 1 + n_qk_tiles + n_gate_tiles
    rows = lax.broadcasted_iota(I32, (MXU_DIM, MXU_DIM), 0) // head_dim
    cols = lax.broadcasted_iota(I32, (MXU_DIM, MXU_DIM), 1) // head_dim
    bd = (rows == cols).astype(BF16)
    kern = functools.partial(_inproj_kernel, n_qk_tiles=n_qk_tiles, head_dim=head_dim)
    return pl.pallas_call(
        kern,
        grid=(n // tm, n_col),
        in_specs=[
            pl.BlockSpec((tm, d), lambda i, j: (i, 0)),
            pl.BlockSpec((d, tn), lambda i, j: (0, j)),
            pl.BlockSpec((1, tn), lambda i, j: (0, jnp.clip(j - 1, 0, n_qk_tiles - 1))),
            pl.BlockSpec((MXU_DIM, MXU_DIM), lambda i, j: (0, 0)),
        ],
        out_specs=[
            pl.BlockSpec((tm, tn), lambda i, j: (i, 0)),
            pl.BlockSpec((tm, tn), lambda i, j: (i, jnp.clip(j - 1, 0, n_qk_tiles - 1))),
            pl.BlockSpec((tm, tn), lambda i, j: (i, jnp.clip(j - 1 - n_qk_tiles, 0, n_gate_tiles - 1))),
        ],
        out_shape=[
            jax.ShapeDtypeStruct((n, pool_width), F32),
            jax.ShapeDtypeStruct((n, 2 * qk_width), BF16),
            jax.ShapeDtypeStruct((n, n_gate_tiles * tn), F32),
        ],
        compiler_params=_cparams("parallel", "arbitrary"),
    )(xn, w_uqkg, qk_gain, bd)


def _vt_kernel(w_ref, x_ref, o_ref):
    o_ref[...] = lax.dot_general(w_ref[...], x_ref[...], (((1,), (1,)), ((), ())),
                                 preferred_element_type=F32).astype(o_ref.dtype)


def _v_transposed(xn, w_vt):
    n, d = xn.shape
    aw = w_vt.shape[0]
    tm = min(TM_INPROJ, n)
    return pl.pallas_call(
        _vt_kernel,
        grid=(n // tm,),
        in_specs=[pl.BlockSpec((aw, d), lambda i: (0, 0)), pl.BlockSpec((tm, d), lambda i: (i, 0))],
        out_specs=pl.BlockSpec((aw, tm), lambda i: (0, i)),
        out_shape=jax.ShapeDtypeStruct((aw, n), BF16),
        compiler_params=_cparams("parallel"),
    )(w_vt, xn)


def _pool_kernel(cur_ref, prev_ref, next_ref, pw_ref, ps_ref, o_ref, ext_hi, ext_lo, *, seq, ts, group):
    i = pl.program_id(1)
    pad = POOL_PAD
    halo = prev_ref.shape[1]
    cur = cur_ref[0]
    width = cur.shape[1]
    zeros = jnp.zeros((pad - halo, width), BF16)
    for ext, part in ((ext_hi, 0), (ext_lo, 1)):
        ext[0:pad - halo, :] = zeros
        ext[pad - halo:pad, :] = _split_bf16(prev_ref[0])[part]
        ext[pad:pad + ts, :] = _split_bf16(cur)[part]
        ext[pad + ts:pad + ts + halo, :] = _split_bf16(next_ref[0])[part]
        ext[pad + ts + halo:, :] = zeros
    t = i * ts + lax.broadcasted_iota(I32, (ts, ts + 2 * pad), 0)
    p = i * ts - pad + lax.broadcasted_iota(I32, (ts, ts + 2 * pad), 1)
    t_col = i * ts + lax.broadcasted_iota(I32, (ts, 1), 0)
    for gi, w in enumerate(POOL_WINDOWS):
        lo = jnp.maximum(t - w // 2, 0)
        hi = jnp.minimum(t + (w - w // 2), seq)
        band = ((p >= lo) & (p < hi)).astype(BF16)
        cnt = (jnp.minimum(t_col + (w - w // 2), seq) - jnp.maximum(t_col - w // 2, 0)).astype(F32)
        ch = slice(gi * group, (gi + 1) * group)
        wsum = (jnp.dot(band, ext_hi[:, ch], preferred_element_type=F32)
                + jnp.dot(band, ext_lo[:, ch], preferred_element_type=F32))
        pooled = wsum / cnt - cur[:, ch]
        y = jnp.dot(pooled.astype(BF16), pw_ref[gi], preferred_element_type=F32) * ps_ref[:, ch]
        o_ref[0, :, ch] = y.astype(o_ref.dtype)


def _pool_mixer(u3, pool_w, pool_scale):
    b, s, pwid = u3.shape
    ng, group, _ = pool_w.shape
    assert ng == len(POOL_WINDOWS) and ng * group == pwid
    ts = min(TS_POOL, s)
    halo = 16
    assert max(POOL_WINDOWS) // 2 <= halo and s % ts == 0 and ts % halo == 0
    nblk = ts // halo
    last = s // halo - 1
    kern = functools.partial(_pool_kernel, seq=s, ts=ts, group=group)
    return pl.pallas_call(
        kern,
        grid=(b, s // ts),
        in_specs=[
            pl.BlockSpec((1, ts, pwid), lambda bi, i: (bi, i, 0)),
            pl.BlockSpec((1, halo, pwid), lambda bi, i: (bi, jnp.maximum(i * nblk - 1, 0), 0)),
            pl.BlockSpec((1, halo, pwid), lambda bi, i: (bi, jnp.minimum((i + 1) * nblk, last), 0)),
            pl.BlockSpec((ng, group, group), lambda bi, i: (0, 0, 0)),
            pl.BlockSpec((1, pwid), lambda bi, i: (0, 0)),
        ],
        out_specs=pl.BlockSpec((1, ts, pwid), lambda bi, i: (bi, i, 0)),
        out_shape=jax.ShapeDtypeStruct((b, s, pwid), BF16),
        scratch_shapes=[pltpu.VMEM((ts + 2 * POOL_PAD, pwid), BF16), pltpu.VMEM((ts + 2 * POOL_PAD, pwid), BF16)],
        compiler_params=_cparams("parallel", "parallel"),
    )(u3, u3, u3, pool_w, pool_scale.reshape(1, pwid))


def _bucket_thresholds(num_buckets):
    half = num_buckets // 2
    max_exact = half // 2
    ratio = MAX_DISTANCE // max_exact
    assert ratio * max_exact == MAX_DISTANCE
    steps = half - max_exact
    thr = []
    for jj in range(1, steps):
        n = max_exact
        while n ** steps < (max_exact ** steps) * (ratio ** jj):
            n += 1
        thr.append(n)
    return half, max_exact, thr


def _bias_kernel(tab_ref, o_ref, *, tile, num_buckets):
    h = pl.program_id(0)
    delta = pl.program_id(1) - 2
    half, max_exact, thr = _bucket_thresholds(num_buckets)
    kk = lax.broadcasted_iota(I32, (tile, tile), 0)
    qq = lax.broadcasted_iota(I32, (tile, tile), 1)
    rel = delta * tile + kk - qq
    n = jnp.abs(rel)
    large = jnp.full((tile, tile), max_exact, I32)
    for th in thr:
        large = large + (n >= th).astype(I32)
    bucket = jnp.where(n < max_exact, n, large) + jnp.where(rel > 0, half, 0)
    out = jnp.zeros((tile, tile), F32)
    for bkt in range(num_buckets):
        out = jnp.where(bucket == bkt, tab_ref[bkt, h], out)
    o_ref[0, 0] = out * LOG2E


def _bias_tiles(rel_bias, tile):
    nb, nh = rel_bias.shape
    assert tile >= MAX_DISTANCE
    kern = functools.partial(_bias_kernel, tile=tile, num_buckets=nb)
    return pl.pallas_call(
        kern,
        grid=(nh, 5),
        in_specs=[pl.BlockSpec(memory_space=pltpu.SMEM)],
        out_specs=pl.BlockSpec((1, 1, tile, tile), lambda h, dd: (h, dd, 0, 0)),
        out_shape=jax.ShapeDtypeStruct((nh, 5, tile, tile), F32),
        compiler_params=_cparams("parallel", "parallel"),
    )(rel_bias)


def _attn_kernel(tab_ref, q_ref, k_ref, vt_ref, b_ref, lqk_ref, sg_ref, o_ref,
                 s_sc, mb_sc, off_sc, m_sc, acc_sc, *, head_dim, lam_init, nblk):
    h = pl.program_id(1)
    i = pl.program_id(2)
    tile = q_ref.shape[1]
    vdim = vt_ref.shape[0]
    nsteps = nblk // 2
    q = q_ref[0]
    lane = lax.broadcasted_iota(I32, (tile, q.shape[1]), 1)
    ones = jnp.ones((acc_sc.shape[1] - vdim, tile), BF16)
    nb = tab_ref.shape[0]

    def start(blk):
        return blk * tile if isinstance(blk, int) else pl.multiple_of(blk * tile, tile)

    def scores(slot, blk, far):
        k = k_ref[0, pl.ds(start(blk), tile), :]
        if far:
            off = jnp.where(blk < i, tab_ref[nb // 2 - 1, h], tab_ref[nb - 1, h]) * LOG2E
        else:
            off = 0.0
            bias = b_ref[0, jnp.clip(blk - i, -2, 2) + 2]
        off_sc[slot] = jnp.full((1, tile), off, F32)
        for c in range(2):
            kc = jnp.where((lane >= c * head_dim) & (lane < (c + 1) * head_dim), k, jnp.zeros_like(k))
            s = lax.dot_general(kc, q, (((1,), (1,)), ((), ())), preferred_element_type=F32)
            if not far:
                s = s + bias
            s_sc[slot, c] = s
            mb_sc[slot, c] = jnp.max(s, axis=0, keepdims=True) + off

    def consume(slot, blk):
        vt = vt_ref[:, pl.ds(start(blk), tile)]
        vt_ones = jnp.concatenate([vt, ones], axis=0)
        for c in range(2):
            m_prev = m_sc[c]
            m_new = jnp.maximum(m_prev, mb_sc[slot, c])
            alpha = jnp.exp2(m_prev - m_new)
            p = jnp.exp2(s_sc[slot, c] - (m_new - off_sc[slot]))
            acc_sc[c] = alpha * acc_sc[c] + jnp.dot(vt_ones, p.astype(BF16), preferred_element_type=F32)
            m_sc[c] = m_new

    m_sc[...] = jnp.full(m_sc.shape, -1e30, F32)
    acc_sc[...] = jnp.zeros(acc_sc.shape, F32)
    scores(0, 0, False)

    def pairs(j0, npairs):
        first = 2 * j0 + 1
        last = 2 * (j0 + npairs)
        all_far = (first - i >= 2) | (i - last >= 2)

        def run(far):
            for jj in range(npairs):
                j = j0 + jj
                scores(1, 2 * j + 1, far)
                consume(0, 2 * j)
                scores(0, 2 * j + 2, far)
                consume(1, 2 * j + 1)

        @pl.when(all_far)
        def _():
            run(True)

        @pl.when(jnp.logical_not(all_far))
        def _():
            run(False)

    group = PAIRS_PER_TRIP
    ngroups = (nsteps - 1) // group

    def trip(g, carry):
        pairs(g * group, group)
        return carry

    lax.fori_loop(0, ngroups, trip, 0)
    if (nsteps - 1) % group:
        pairs(ngroups * group, (nsteps - 1) % group)
    scores(1, nblk - 1, False)
    consume(0, nblk - 2)
    consume(1, nblk - 1)

    lq = lqk_ref[...]
    lam = (jnp.exp(jnp.sum(lq[0:1] * lq[1:2], axis=1, keepdims=True))
           - jnp.exp(jnp.sum(lq[2:3] * lq[3:4], axis=1, keepdims=True)) + lam_init)
    a0 = acc_sc[0]
    a1 = acc_sc[1]
    o = a0[:vdim] / a0[vdim:vdim + 1] - lam * (a1[:vdim] / a1[vdim:vdim + 1])
    ms = jnp.mean(o * o, axis=0, keepdims=True)
    y = o * lax.rsqrt(ms + EPS) * sg_ref[...] * (1.0 - lam_init)
    o_ref[0] = y.T.astype(o_ref.dtype)


def _diff_attention(qk3, vt, bias_tiles, rel_bias, lambda_qk, subln_g, head_dim, lam_init):
    b, s, two_qk = qk3.shape
    nh = bias_tiles.shape[0]
    tile = bias_tiles.shape[2]
    vdim = vt.shape[0] // nh
    assert vdim == 2 * head_dim == LANES and two_qk == 2 * nh * LANES and s % (2 * tile) == 0
    nblk = s // tile
    ones_rows = 16
    kern = functools.partial(_attn_kernel, head_dim=head_dim, lam_init=lam_init, nblk=nblk)
    return pl.pallas_call(
        kern,
        grid=(b, nh, nblk),
        in_specs=[
            pl.BlockSpec(memory_space=pltpu.SMEM),
            pl.BlockSpec((1, tile, LANES), lambda bi, h, i: (bi, i, h)),
            pl.BlockSpec((1, s, LANES), lambda bi, h, i: (bi, 0, nh + h)),
            pl.BlockSpec((vdim, s), lambda bi, h, i: (h, bi)),
            pl.BlockSpec((1, 5, tile, tile), lambda bi, h, i: (h, 0, 0, 0)),
            pl.BlockSpec(lambda_qk.shape, lambda bi, h, i: (0, 0)),
            pl.BlockSpec((vdim, 1), lambda bi, h, i: (0, 0)),
        ],
        out_specs=pl.BlockSpec((1, tile, vdim), lambda bi, h, i: (bi, i, h)),
        out_shape=jax.ShapeDtypeStruct((b, s, nh * vdim), BF16),
        scratch_shapes=[
            pltpu.VMEM((2, 2, tile, tile), F32),
            pltpu.VMEM((2, 2, 1, tile), F32),
            pltpu.VMEM((2, 1, tile), F32),
            pltpu.VMEM((2, 1, tile), F32),
            pltpu.VMEM((2, vdim + ones_rows, tile), F32),
        ],
        compiler_params=_cparams("parallel", "parallel", "parallel"),
    )(rel_bias, qk3, qk3, vt, bias_tiles, lambda_qk, subln_g.reshape(vdim, 1))


def _sigmoid(x):
    return 1.0 / (1.0 + jnp.exp(-x))


def _merge_kernel(pm_ref, att_ref, gp_ref, ga_ref, wp_ref, wa_ref, o_ref):
    pool_out = jnp.dot(pm_ref[...], wp_ref[...], preferred_element_type=F32)
    attn_out = jnp.dot(att_ref[...], wa_ref[...], preferred_element_type=F32)
    o_ref[...] = (_sigmoid(gp_ref[...]) * pool_out + _sigmoid(ga_ref[...]) * attn_out).astype(o_ref.dtype)


def _merge(pm, att, gates, w_pool_out, w_attn_out):
    n, pwid = pm.shape
    aw = att.shape[1]
    d = w_pool_out.shape[1]
    tm = min(TM_MERGE, n)
    return pl.pallas_call(
        _merge_kernel,
        grid=(n // tm,),
        in_specs=[
            pl.BlockSpec((tm, pwid), lambda i: (i, 0)),
            pl.BlockSpec((tm, aw), lambda i: (i, 0)),
            pl.BlockSpec((tm, d), lambda i: (i, 0)),
            pl.BlockSpec((tm, d), lambda i: (i, 1)),
            pl.BlockSpec((pwid, d), lambda i: (0, 0)),
            pl.BlockSpec((aw, d), lambda i: (0, 0)),
        ],
        out_specs=pl.BlockSpec((tm, d), lambda i: (i, 0)),
        out_shape=jax.ShapeDtypeStruct((n, d), BF16),
        compiler_params=_cparams("parallel"),
    )(pm, att, gates, gates, w_pool_out, w_attn_out)


def _oproj_kernel(m_ref, x_ref, wo_ref, g2_ref, wr_ref, h_ref, t_ref, aff_ref):
    h = x_ref[...] + jnp.dot(m_ref[...], wo_ref[...], preferred_element_type=F32)
    h_ref[...] = h
    ms = jnp.mean(h * h, axis=-1, keepdims=True)
    t = h * lax.rsqrt(ms + EPS) * g2_ref[...]
    half = t.shape[1] // 2
    bits = lax.bitcast_convert_type(t.astype(BF16).astype(F32), U32)
    words = bits[:, :half] | (bits[:, half:] >> 16)
    for p in range(t_ref.shape[0]):
        t_ref[p] = words[:, p * SC_PIECE:(p + 1) * SC_PIECE]
    t_hi, t_lo = _split_bf16(t)
    w2 = wr_ref[...]
    p_hi = jnp.dot(t_hi, w2, preferred_element_type=F32)
    p_lo = jnp.dot(t_lo, w2, preferred_element_type=F32)
    logits = (p_hi[:, :LANES] + p_hi[:, LANES:]) + p_lo[:, :LANES]
    ne = aff_ref.shape[0]
    lt = logits.T[:ne]
    e = jnp.exp(lt - jnp.max(lt, axis=0, keepdims=True))
    aff_ref[...] = e / jnp.sum(e, axis=0, keepdims=True)


def _oproj_router(merged, x2d, w_o, norm2_g, w_router):
    n, d = x2d.shape
    ne = w_router.shape[1]
    assert ne <= LANES
    w_hi, w_lo = _split_bf16(w_router)
    w2 = jnp.zeros((d, 2 * LANES), BF16).at[:, :ne].set(w_hi).at[:, LANES:LANES + ne].set(w_lo)
    tm = min(TM_OPROJ, n)
    return pl.pallas_call(
        _oproj_kernel,
        grid=(n // tm,),
        in_specs=[
            pl.BlockSpec((tm, d), lambda i: (i, 0)),
            pl.BlockSpec((tm, d), lambda i: (i, 0)),
            pl.BlockSpec((d, d), lambda i: (0, 0)),
            pl.BlockSpec((1, d), lambda i: (0, 0)),
            pl.BlockSpec((d, 2 * LANES), lambda i: (0, 0)),
        ],
        out_specs=[
            pl.BlockSpec((tm, d), lambda i: (i, 0)),
            pl.BlockSpec((d // 2 // SC_PIECE, tm, SC_PIECE), lambda i: (0, i, 0)),
            pl.BlockSpec((ne, tm), lambda i: (0, i)),
        ],
        out_shape=[
            jax.ShapeDtypeStruct((n, d), F32),
            jax.ShapeDtypeStruct((d // 2 // SC_PIECE, n, SC_PIECE), U32),
            jax.ShapeDtypeStruct((ne, n), F32),
        ],
        compiler_params=_cparams("parallel"),
    )(merged, x2d, w_o, norm2_g.reshape(1, d), w2)


def _route_kernel(a_ref, idx_ref, gate_ref, *, cap):
    a = a_ref[0]
    nc, wid = a.shape
    keys = lax.bitcast_convert_type(a, I32)

    def search(bit, thr):
        cand = thr | jnp.left_shift(jnp.int32(1), 30 - bit)
        cnt = jnp.sum(jnp.sum((keys >= cand).astype(I32), axis=1, keepdims=True), axis=0, keepdims=True)
        return jnp.where(cnt >= cap, cand, thr)

    thr = lax.fori_loop(0, 31, search, jnp.zeros((1, 1), I32))
    gt = keys > thr
    eq = keys == thr
    n_gt = jnp.sum(jnp.sum(gt.astype(F32), axis=1, keepdims=True), axis=0, keepdims=True)
    need = cap - n_gt

    upper = (lax.broadcasted_iota(I32, (wid, wid), 0) <= lax.broadcasted_iota(I32, (wid, wid), 1)).astype(BF16)
    lower = (lax.broadcasted_iota(I32, (nc, nc), 1) < lax.broadcasted_iota(I32, (nc, nc), 0)).astype(BF16)

    def prefix(mask):
        inc = jnp.dot(mask.astype(BF16), upper, preferred_element_type=F32)
        tot = inc[:, wid - 1:wid]
        off = jnp.dot(lower, jnp.broadcast_to(tot, (nc, LANES)).astype(BF16), preferred_element_type=F32)[:, 0:1]
        return inc, tot, off

    inc_e, _, off_e = prefix(eq)
    tie_rank = off_e + inc_e - eq.astype(F32)
    sel = gt | (eq & (tie_rank < need))
    inc_s, tot_s, off_s = prefix(sel)

    slot = lax.broadcasted_iota(I32, (1, cap), 1).astype(F32)
    chunk_end = off_s + tot_s
    k_row = jnp.sum((chunk_end <= slot).astype(F32), axis=0, keepdims=True)
    onehot = lax.broadcasted_iota(I32, (nc, cap), 0).astype(F32) == k_row
    off_k = jnp.sum(jnp.where(onehot, off_s, 0.0), axis=0, keepdims=True)
    rank = slot - off_k
    onehot_b = onehot.astype(BF16)
    inc_of_slot = jnp.dot(inc_s.T.astype(BF16), onehot_b, preferred_element_type=F32)
    local = jnp.sum((inc_of_slot <= rank).astype(F32), axis=0, keepdims=True)
    idx_ref[0] = (k_row * wid + local).astype(I32)

    at = a.T
    a1 = at.astype(BF16)
    r1 = at - a1.astype(F32)
    a2 = r1.astype(BF16)
    a3 = (r1 - a2.astype(F32)).astype(BF16)
    aff_of_slot = ((jnp.dot(a1, onehot_b, preferred_element_type=F32) + jnp.dot(a2, onehot_b, preferred_element_type=F32))
                   + jnp.dot(a3, onehot_b, preferred_element_type=F32))
    pick = lax.broadcasted_iota(I32, (wid, cap), 0).astype(F32) == local
    gate_ref[0] = jnp.sum(jnp.where(pick, aff_of_slot, 0.0), axis=0, keepdims=True)


def _route(aff_t, cap):
    ne, n = aff_t.shape
    wid = ROUTE_CHUNK
    assert n % wid == 0
    nc = n // wid
    a3 = aff_t.reshape(ne, nc, wid)
    kern = functools.partial(_route_kernel, cap=cap)
    return pl.pallas_call(
        kern,
        grid=(ne,),
        in_specs=[pl.BlockSpec((1, nc, wid), lambda e: (e, 0, 0))],
        out_specs=[pl.BlockSpec((1, 1, cap), lambda e: (e, 0, 0)), pl.BlockSpec((1, 1, cap), lambda e: (e, 0, 0))],
        out_shape=[jax.ShapeDtypeStruct((ne, 1, cap), I32), jax.ShapeDtypeStruct((ne, 1, cap), F32)],
        compiler_params=_cparams("parallel"),
    )(a3)


def _row_copy(src_hbm, tok, buf, r, sem):
    return pltpu.make_async_copy(src_hbm.at[pl.ds(tok, 1)], buf.at[pl.ds(r, 1)], sem)


def _gather_rows_sc(t, idx):
    npieces, n, _ = t.shape
    ne, _, cap = idx.shape
    total = npieces * ne * cap
    piece_idx = (jnp.arange(npieces, dtype=I32)[:, None] * n + idx.reshape(1, -1)).reshape(1, total)
    pieces = t.reshape(npieces * n, SC_PIECE)
    mesh = plsc.VectorSubcoreMesh(core_axis_name="core", subcore_axis_name="subcore")

    @pl.kernel(out_type=jax.ShapeDtypeStruct((total, SC_PIECE), t.dtype), mesh=mesh)
    def gather(x_hbm, i_hbm, o_hbm):
        def body(i_vmem, o_vmem):
            pltpu.sync_copy(x_hbm.at[i_vmem.at[0]], o_vmem)

        pltpu.emit_pipeline(
            body,
            grid=(total // SC_WINDOW,),
            in_specs=[pl.BlockSpec((1, SC_WINDOW), index_map=lambda i: (0, i))],
            out_specs=[pl.BlockSpec((SC_WINDOW, SC_PIECE), index_map=lambda i: (i, 0))],
            core_axis_name=("core", "subcore"),
            dimension_semantics=(pltpu.PARALLEL,),
        )(i_hbm, o_hbm)

    return gather(pieces, piece_idx).reshape(npieces, ne, cap, SC_PIECE)


def _ffn_kernel(x_ref, wg_ref, wu_ref, wd_ref, gate_ref, o_ref, acc, xb, *, nf):
    f = pl.program_id(2)

    @pl.when(f == 0)
    def _():
        acc[...] = jnp.zeros(acc.shape, F32)
        npieces, _, _, piece = x_ref.shape
        half = npieces * piece
        for p in range(npieces):
            words = x_ref[p, 0]
            lo, hi = p * piece, (p + 1) * piece
            xb[:, lo:hi] = lax.bitcast_convert_type(words & jnp.uint32(0xFFFF0000), F32).astype(BF16)
            xb[:, half + lo:half + hi] = lax.bitcast_convert_type(words << 16, F32).astype(BF16)

    x = xb[...]
    a = jnp.dot(x, wg_ref[0], preferred_element_type=F32)
    b = jnp.dot(x, wu_ref[0], preferred_element_type=F32)
    hid = (a * _sigmoid(a)) * b
    acc[...] += jnp.dot(hid.astype(BF16), wd_ref[0], preferred_element_type=F32)

    @pl.when(f == nf - 1)
    def _():
        o_ref[0] = acc[...] * gate_ref[0]


def _expert_ffn(xe, w_gate, w_up, w_down, gate_col):
    npieces, ne, cap, piece = xe.shape
    d = 2 * npieces * piece
    dff = w_gate.shape[2]
    tm = min(TM_FFN, cap)
    tf = min(TF_FFN, dff)
    nf = dff // tf
    kern = functools.partial(_ffn_kernel, nf=nf)
    return pl.pallas_call(
        kern,
        grid=(ne, cap // tm, nf),
        in_specs=[
            pl.BlockSpec((npieces, 1, tm, piece), lambda e, c, f: (0, e, c, 0)),
            pl.BlockSpec((1, d, tf), lambda e, c, f: (e, 0, f)),
            pl.BlockSpec((1, d, tf), lambda e, c, f: (e, 0, f)),
            pl.BlockSpec((1, tf, d), lambda e, c, f: (e, f, 0)),
            pl.BlockSpec((1, tm, 1), lambda e, c, f: (e, c, 0)),
        ],
        out_specs=pl.BlockSpec((1, tm, d), lambda e, c, f: (e, c, 0)),
        out_shape=jax.ShapeDtypeStruct((ne, cap, d), F32),
        scratch_shapes=[pltpu.VMEM((tm, d), F32), pltpu.VMEM((tm, d), BF16)],
        compiler_params=_cparams("parallel", "parallel", "arbitrary"),
    )(xe, w_gate, w_up, w_down, gate_col)


def _scatter_kernel(idx_ref, idx_next_ref, ye_ref, h_hbm, y_hbm, buf, sem_in, sem_out, *, rows, steps):
    del h_hbm
    c = pl.program_id(1)
    slot = c % 2

    def fetch(iref, s):
        def body(r, carry):
            _row_copy(y_hbm, iref[0, 0, r], buf.at[s], r, sem_in.at[s]).start()
            return carry
        lax.fori_loop(0, rows, body, 0, unroll=DMA_UNROLL)

    def row_put(tok, s, r):
        return pltpu.make_async_copy(buf.at[s].at[pl.ds(r, 1)], y_hbm.at[pl.ds(tok, 1)], sem_out.at[s])

    def put_done(s):
        def body(r, carry):
            row_put(0, s, r).wait()
            return carry
        lax.fori_loop(0, rows, body, 0, unroll=DMA_UNROLL)

    @pl.when(c == 0)
    def _():
        fetch(idx_ref, 0)

    @pl.when(c >= 1)
    def _():
        put_done(1 - slot)

    @pl.when(c + 1 < steps)
    def _():
        fetch(idx_next_ref, 1 - slot)

    def fetched(r, carry):
        _row_copy(y_hbm, 0, buf.at[slot], r, sem_in.at[slot]).wait()
        return carry

    lax.fori_loop(0, rows, fetched, 0, unroll=DMA_UNROLL)
    buf[slot] = buf[slot] + ye_ref[0]

    def put(r, carry):
        row_put(idx_ref[0, 0, r], slot, r).start()
        return carry

    lax.fori_loop(0, rows, put, 0, unroll=DMA_UNROLL)

    @pl.when(c == steps - 1)
    def _():
        put_done(slot)


def _scatter_add(h, ye, idx):
    n, d = h.shape
    ne, _, cap = idx.shape
    rows = min(R_GATHER, cap)
    steps = cap // rows
    total = ne * steps
    idx3 = idx.reshape(total, 1, rows)
    kern = functools.partial(_scatter_kernel, rows=rows, steps=steps)
    return pl.pallas_call(
        kern,
        grid=(ne, steps),
        in_specs=[
            pl.BlockSpec((1, 1, rows), lambda e, c: (e * steps + c, 0, 0), memory_space=pltpu.SMEM),
            pl.BlockSpec((1, 1, rows), lambda e, c: (jnp.minimum(e * steps + c + 1, total - 1), 0, 0),
                         memory_space=pltpu.SMEM),
            pl.BlockSpec((1, rows, d), lambda e, c: (e, c, 0)),
            pl.BlockSpec(memory_space=pl.ANY),
        ],
        out_specs=pl.BlockSpec(memory_space=pl.ANY),
        out_shape=jax.ShapeDtypeStruct((n, d), F32),
        input_output_aliases={3: 0},
        scratch_shapes=[pltpu.VMEM((2, rows, d), F32), pltpu.SemaphoreType.DMA((2,)), pltpu.SemaphoreType.DMA((2,))],
        compiler_params=_cparams("arbitrary", "arbitrary"),
    )(idx3, idx3, ye, h)


def _prepare_weights(l, norm1_g, w_in, pool_w, pool_scale, w_pool_out, q_norm_g, k_norm_g, rel_bias, lambda_qk, subln_g,
                     w_attn_out, w_o, norm2_g, w_router, w_gate, w_up, w_down):
    head_dim = q_norm_g.shape[1]
    nh = rel_bias.shape[1]
    vdim = subln_g.shape[1]
    pool_width = pool_scale.shape[1]
    qk_width = nh * 2 * head_dim
    attn_width = nh * vdim
    o1, o2, o3 = pool_width, pool_width + 2 * qk_width, pool_width + 2 * qk_width + attn_width
    w = w_in[l]
    reps = qk_width // head_dim
    qk_gain = jnp.concatenate([jnp.tile(q_norm_g[l] * (head_dim ** -0.5) * LOG2E, reps), jnp.tile(k_norm_g[l], reps)])
    return dict(
        head_dim=head_dim, nh=nh, vdim=vdim, pool_width=pool_width, qk_width=qk_width,
        lam_init=0.8 - 0.6 * math.exp(-0.3 * l),
        norm1_g=norm1_g[l],
        w_uqkg=jnp.concatenate([w[:, :o2], w[:, o3:]], axis=1).astype(BF16),
        w_vt=w[:, o2:o3].T.astype(BF16),
        qk_gain=qk_gain.reshape(1, -1).astype(F32),
        pool_w=pool_w[l].astype(BF16), pool_scale=pool_scale[l],
        w_pool_out=w_pool_out[l].astype(BF16), w_attn_out=w_attn_out[l].astype(BF16), w_o=w_o[l].astype(BF16),
        rel_bias=rel_bias, lambda_qk=lambda_qk[l], subln_g=subln_g[l], norm2_g=norm2_g[l],
        w_router=w_router[l],
        w_gate=w_gate[l].astype(BF16), w_up=w_up[l].astype(BF16), w_down=w_down[l].astype(BF16),
    )


def _layer(x, p, bias_tiles):
    b, s, d = x.shape
    n = b * s
    x2d = x.reshape(n, d)
    xn = _rmsnorm(x2d, p["norm1_g"])
    u, qk, gates = _inproj(xn, p["w_uqkg"], p["qk_gain"], p["head_dim"], p["pool_width"], p["qk_width"])
    vt = _v_transposed(xn, p["w_vt"])
    pm = _pool_mixer(u.reshape(b, s, -1), p["pool_w"], p["pool_scale"]).reshape(n, -1)
    att = _diff_attention(qk.reshape(b, s, -1), vt, bias_tiles, p["rel_bias"], p["lambda_qk"], p["subln_g"],
                          p["head_dim"], p["lam_init"]).reshape(n, -1)
    merged = _merge(pm, att, gates, p["w_pool_out"], p["w_attn_out"])
    h, t, aff_t = _oproj_router(merged, x2d, p["w_o"], p["norm2_g"], p["w_router"])
    ne = aff_t.shape[0]
    cap = max(1, (CAPACITY_FACTOR * n) // ne)
    idx, gate = _route(aff_t, cap)
    xe = _gather_rows_sc(t, idx)
    ye = _expert_ffn(xe, p["w_gate"], p["w_up"], p["w_down"], gate.reshape(ne, cap, 1))
    y = _scatter_add(h, ye, idx)
    return y.reshape(b, s, d)


def kernel(x_prompt, x_sample, norm1_g, w_in, pool_w, pool_scale, w_pool_out, q_norm_g, k_norm_g, rel_bias, lambda_qk,
           subln_g, w_attn_out, w_o, norm2_g, w_router, w_gate, w_up, w_down):
    tile = min(T_ATTN, x_prompt.shape[1], x_sample.shape[1])
    bias_tiles = _bias_tiles(rel_bias, tile)
    hp, hs = x_prompt, x_sample
    for l in range(norm1_g.shape[0]):
        p = _prepare_weights(l, norm1_g, w_in, pool_w, pool_scale, w_pool_out, q_norm_g, k_norm_g, rel_bias, lambda_qk,
                             subln_g, w_attn_out, w_o, norm2_g, w_router, w_gate, w_up, w_down)
        hp = _layer(hp, p, bias_tiles)
        hs = _layer(hs, p, bias_tiles)
    return (hp, hs)
```

```python
import functools
import math

import jax
import jax.numpy as jnp
from jax import lax
from jax.experimental import pallas as pl
from jax.experimental.pallas import tpu as pltpu
from jax.experimental.pallas import tpu_sc as plsc

F32, BF16, I32, U32 = jnp.float32, jnp.bfloat16, jnp.int32, jnp.uint32

EPS = 1e-6
POOL_WINDOWS = (2, 4, 8, 16)
MAX_DISTANCE = 128
CAPACITY_FACTOR = 2
LOG2E = math.log2(math.e)

V7X_VMEM_BYTES = 64 * 1024 * 1024
VMEM_LIMIT_BYTES = V7X_VMEM_BYTES - 8 * 1024 * 1024
LANES = 128
MXU_DIM = 256

TM_NORM = 512
TM_INPROJ = 1024
TN_INPROJ = 1024
TS_POOL = 512
POOL_PAD = 128
T_ATTN = 512
PAIRS_PER_TRIP = 2
TM_MERGE = 256
TM_OPROJ = 256
ROUTE_CHUNK = 256
R_GATHER = 512
DMA_UNROLL = 8
SC_PIECE = 256
SC_WINDOW = 128
TM_FFN = 1024
TF_FFN = 256


def _cparams(*sem):
    return pltpu.CompilerParams(dimension_semantics=sem, vmem_limit_bytes=VMEM_LIMIT_BYTES)


def _split_bf16(a):
    hi = a.astype(BF16)
    lo = (a - hi.astype(F32)).astype(BF16)
    return hi, lo


def _rmsnorm_kernel(x_ref, g_ref, o_ref):
    x = x_ref[...]
    ms = jnp.mean(x * x, axis=-1, keepdims=True)
    o_ref[...] = (x * lax.rsqrt(ms + EPS) * g_ref[...]).astype(o_ref.dtype)


def _rmsnorm(x2d, g):
    n, d = x2d.shape
    tm = min(TM_NORM, n)
    return pl.pallas_call(
        _rmsnorm_kernel,
        grid=(n // tm,),
        in_specs=[pl.BlockSpec((tm, d), lambda i: (i, 0)), pl.BlockSpec((1, d), lambda i: (0, 0))],
        out_specs=pl.BlockSpec((tm, d), lambda i: (i, 0)),
        out_shape=jax.ShapeDtypeStruct((n, d), BF16),
        compiler_params=_cparams("parallel"),
    )(x2d, g.reshape(1, d))


def _inproj_kernel(x_ref, w_ref, gain_ref, bd_ref, u_ref, qk_ref, g_ref, *, n_qk_tiles, head_dim):
    j = pl.program_id(1)
    acc = jnp.dot(x_ref[...], w_ref[...], preferred_element_type=F32)

    @pl.when(j == 0)
    def _():
        u_ref[...] = acc

    @pl.when((j >= 1) & (j <= n_qk_tiles))
    def _():
        bd = bd_ref[...]
        for c0 in range(0, acc.shape[1], MXU_DIM):
            a = acc[:, c0:c0 + MXU_DIM]
            hi, lo = _split_bf16(a * a)
            ss = jnp.dot(hi, bd, preferred_element_type=F32) + jnp.dot(lo, bd, preferred_element_type=F32)
            y = a * lax.rsqrt(ss * (1.0 / head_dim) + EPS) * gain_ref[:, c0:c0 + MXU_DIM]
            qk_ref[:, c0:c0 + MXU_DIM] = y.astype(qk_ref.dtype)

    @pl.when(j > n_qk_tiles)
    def _():
        g_ref[...] = acc


def _inproj(xn, w_uqkg, qk_gain, head_dim, pool_width, qk_width):
    n, d = xn.shape
    tn = TN_INPROJ
    assert pool_width == tn and qk_width % tn == 0
    tm = min(TM_INPROJ, n)
    n_qk_tiles = 2 * qk_width // tn
    n_gate_tiles = (w_uqkg.shape[1] - pool_width - 2 * qk_width) // tn
    n_col = 1 + n_qk_tiles + n_gate_tiles
    rows = lax.broadcasted_iota(I32, (MXU_DIM, MXU_DIM), 0) // head_dim
    cols = lax.broadcasted_iota(I32, (MXU_DIM, MXU_DIM), 1) // head_dim
    bd = (rows == cols).astype(BF16)
    kern = functools.partial(_inproj_kernel, n_qk_tiles=n_qk_tiles, head_dim=head_dim)
    return pl.pallas_call(
        kern,
        grid=(n // tm, n_col),
        in_specs=[
            pl.BlockSpec((tm, d), lambda i, j: (i, 0)),
            pl.BlockSpec((d, tn), lambda i, j: (0, j)),
            pl.BlockSpec((1, tn), lambda i, j: (0, jnp.clip(j - 1, 0, n_qk_tiles - 1))),
            pl.BlockSpec((MXU_DIM, MXU_DIM), lambda i, j: (0, 0)),
        ],
        out_specs=[
            pl.BlockSpec((tm, tn), lambda i, j: (i, 0)),
            pl.BlockSpec((tm, tn), lambda i, j: (i, jnp.clip(j - 1, 0, n_qk_tiles - 1))),
            pl.BlockSpec((tm, tn), lambda i, j: (i, jnp.clip(j - 1 - n_qk_tiles, 0, n_gate_tiles - 1))),
        ],
        out_shape=[
            jax.ShapeDtypeStruct((n, pool_width), F32),
            jax.ShapeDtypeStruct((n, 2 * qk_width), BF16),
            jax.ShapeDtypeStruct((n, n_gate_tiles * tn), F32),
        ],
        compiler_params=_cparams("parallel", "arbitrary"),
    )(xn, w_uqkg, qk_gain, bd)


def _vt_kernel(w_ref, x_ref, o_ref):
    o_ref[...] = lax.dot_general(w_ref[...], x_ref[...], (((1,), (1,)), ((), ())),
                                 preferred_element_type=F32).astype(o_ref.dtype)


def _v_transposed(xn, w_vt):
    n, d = xn.shape
    aw = w_vt.shape[0]
    tm = min(TM_INPROJ, n)
    return pl.pallas_call(
        _vt_kernel,
        grid=(n // tm,),
        in_specs=[pl.BlockSpec((aw, d), lambda i: (0, 0)), pl.BlockSpec((tm, d), lambda i: (i, 0))],
        out_specs=pl.BlockSpec((aw, tm), lambda i: (0, i)),
        out_shape=jax.ShapeDtypeStruct((aw, n), BF16),
        compiler_params=_cparams("parallel"),
    )(w_vt, xn)


def _pool_kernel(cur_ref, prev_ref, next_ref, pw_ref, ps_ref, o_ref, ext_hi, ext_lo, *, seq, ts, group):
    i = pl.program_id(1)
    pad = POOL_PAD
    halo = prev_ref.shape[1]
    cur = cur_ref[0]
    width = cur.shape[1]
    zeros = jnp.zeros((pad - halo, width), BF16)
    for ext, part in ((ext_hi, 0), (ext_lo, 1)):
        ext[0:pad - halo, :] = zeros
        ext[pad - halo:pad, :] = _split_bf16(prev_ref[0])[part]
        ext[pad:pad + ts, :] = _split_bf16(cur)[part]
        ext[pad + ts:pad + ts + halo, :] = _split_bf16(next_ref[0])[part]
        ext[pad + ts + halo:, :] = zeros
    t = i * ts + lax.broadcasted_iota(I32, (ts, ts + 2 * pad), 0)
    p = i * ts - pad + lax.broadcasted_iota(I32, (ts, ts + 2 * pad), 1)
    t_col = i * ts + lax.broadcasted_iota(I32, (ts, 1), 0)
    for gi, w in enumerate(POOL_WINDOWS):
        lo = jnp.maximum(t - w // 2, 0)
        hi = jnp.minimum(t + (w - w // 2), seq)
        band = ((p >= lo) & (p < hi)).astype(BF16)
        cnt = (jnp.minimum(t_col + (w - w // 2), seq) - jnp.maximum(t_col - w // 2, 0)).astype(F32)
        ch = slice(gi * group, (gi + 1) * group)
        wsum = (jnp.dot(band, ext_hi[:, ch], preferred_element_type=F32)
                + jnp.dot(band, ext_lo[:, ch], preferred_element_type=F32))
        pooled = wsum / cnt - cur[:, ch]
        y = jnp.dot(pooled.astype(BF16), pw_ref[gi], preferred_element_type=F32) * ps_ref[:, ch]
        o_ref[0, :, ch] = y.astype(o_ref.dtype)


def _pool_mixer(u3, pool_w, pool_scale):
    b, s, pwid = u3.shape
    ng, group, _ = pool_w.shape
    assert ng == len(POOL_WINDOWS) and ng * group == pwid
    ts = min(TS_POOL, s)
    halo = 16
    assert max(POOL_WINDOWS) // 2 <= halo and s % ts == 0 and ts % halo == 0
    nblk = ts // halo
    last = s // halo - 1
    kern = functools.partial(_pool_kernel, seq=s, ts=ts, group=group)
    return pl.pallas_call(
        kern,
        grid=(b, s // ts),
        in_specs=[
            pl.BlockSpec((1, ts, pwid), lambda bi, i: (bi, i, 0)),
            pl.BlockSpec((1, halo, pwid), lambda bi, i: (bi, jnp.maximum(i * nblk - 1, 0), 0)),
            pl.BlockSpec((1, halo, pwid), lambda bi, i: (bi, jnp.minimum((i + 1) * nblk, last), 0)),
            pl.BlockSpec((ng, group, group), lambda bi, i: (0, 0, 0)),
            pl.BlockSpec((1, pwid), lambda bi, i: (0, 0)),
        ],
        out_specs=pl.BlockSpec((1, ts, pwid), lambda bi, i: (bi, i, 0)),
        out_shape=jax.ShapeDtypeStruct((b, s, pwid), BF16),
        scratch_shapes=[pltpu.VMEM((ts + 2 * POOL_PAD, pwid), BF16), pltpu.VMEM((ts + 2 * POOL_PAD, pwid), BF16)],
        compiler_params=_cparams("parallel", "parallel"),
    )(u3, u3, u3, pool_w, pool_scale.reshape(1, pwid))


def _bucket_thresholds(num_buckets):
    half = num_buckets // 2
    max_exact = half // 2
    ratio = MAX_DISTANCE // max_exact
    assert ratio * max_exact == MAX_DISTANCE
    steps = half - max_exact
    thr = []
    for jj in range(1, steps):
        n = max_exact
        while n ** steps < (max_exact ** steps) * (ratio ** jj):
            n += 1
        thr.append(n)
    return half, max_exact, thr


def _bias_kernel(tab_ref, o_ref, *, tile, num_buckets):
    h = pl.program_id(0)
    delta = pl.program_id(1) - 2
    half, max_exact, thr = _bucket_thresholds(num_buckets)
    kk = lax.broadcasted_iota(I32, (tile, tile), 0)
    qq = lax.broadcasted_iota(I32, (tile, tile), 1)
    rel = delta * tile + kk - qq
    n = jnp.abs(rel)
    large = jnp.full((tile, tile), max_exact, I32)
    for th in thr:
        large = large + (n >= th).astype(I32)
    bucket = jnp.where(n < max_exact, n, large) + jnp.where(rel > 0, half, 0)
    out = jnp.zeros((tile, tile), F32)
    for bkt in range(num_buckets):
        out = jnp.where(bucket == bkt, tab_ref[bkt, h], out)
    o_ref[0, 0] = out * LOG2E


def _bias_tiles(rel_bias, tile):
    nb, nh = rel_bias.shape
    assert tile >= MAX_DISTANCE
    kern = functools.partial(_bias_kernel, tile=tile, num_buckets=nb)
    return pl.pallas_call(
        kern,
        grid=(nh, 5),
        in_specs=[pl.BlockSpec(memory_space=pltpu.SMEM)],
        out_specs=pl.BlockSpec((1, 1, tile, tile), lambda h, dd: (h, dd, 0, 0)),
        out_shape=jax.ShapeDtypeStruct((nh, 5, tile, tile), F32),
        compiler_params=_cparams("parallel", "parallel"),
    )(rel_bias)


def _attn_kernel(tab_ref, q_ref, k_ref, vt_ref, b_ref, lqk_ref, sg_ref, o_ref,
                 s_sc, mb_sc, off_sc, m_sc, acc_sc, *, head_dim, lam_init, nblk):
    h = pl.program_id(1)
    i = pl.program_id(2)
    tile = q_ref.shape[1]
    vdim = vt_ref.shape[0]
    nsteps = nblk // 2
    q = q_ref[0]
    lane = lax.broadcasted_iota(I32, (tile, q.shape[1]), 1)
    ones = jnp.ones((acc_sc.shape[1] - vdim, tile), BF16)
    nb = tab_ref.shape[0]

    def start(blk):
        return blk * tile if isinstance(blk, int) else pl.multiple_of(blk * tile, tile)

    def scores(slot, blk, far):
        k = k_ref[0, pl.ds(start(blk), tile), :]
        if far:
            off = jnp.where(blk < i, tab_ref[nb // 2 - 1, h], tab_ref[nb - 1, h]) * LOG2E
        else:
            off = 0.0
            bias = b_ref[0, jnp.clip(blk - i, -2, 2) + 2]
        off_sc[slot] = jnp.full((1, tile), off, F32)
        for c in range(2):
            kc = jnp.where((lane >= c * head_dim) & (lane < (c + 1) * head_dim), k, jnp.zeros_like(k))
            s = lax.dot_general(kc, q, (((1,), (1,)), ((), ())), preferred_element_type=F32)
            if not far:
                s = s + bias
            s_sc[slot, c] = s
            mb_sc[slot, c] = jnp.max(s, axis=0, keepdims=True) + off

    def consume(slot, blk):
        vt = vt_ref[:, pl.ds(start(blk), tile)]
        vt_ones = jnp.concatenate([vt, ones], axis=0)
        for c in range(2):
            m_prev = m_sc[c]
            m_new = jnp.maximum(m_prev, mb_sc[slot, c])
            alpha = jnp.exp2(m_prev - m_new)
            p = jnp.exp2(s_sc[slot, c] - (m_new - off_sc[slot]))
            acc_sc[c] = alpha * acc_sc[c] + jnp.dot(vt_ones, p.astype(BF16), preferred_element_type=F32)
            m_sc[c] = m_new

    m_sc[...] = jnp.full(m_sc.shape, -1e30, F32)
    acc_sc[...] = jnp.zeros(acc_sc.shape, F32)
    scores(0, 0, False)

    def pairs(j0, npairs):
        first = 2 * j0 + 1
        last = 2 * (j0 + npairs)
        all_far = (first - i >= 2) | (i - last >= 2)

        def run(far):
            for jj in range(npairs):
                j = j0 + jj
                scores(1, 2 * j + 1, far)
                consume(0, 2 * j)
                scores(0, 2 * j + 2, far)
                consume(1, 2 * j + 1)

        @pl.when(all_far)
        def _():
            run(True)

        @pl.when(jnp.logical_not(all_far))
        def _():
            run(False)

    group = PAIRS_PER_TRIP
    ngroups = (nsteps - 1) // group

    def trip(g, carry):
        pairs(g * group, group)
        return carry

    lax.fori_loop(0, ngroups, trip, 0)
    if (nsteps - 1) % group:
        pairs(ngroups * group, (nsteps - 1) % group)
    scores(1, nblk - 1, False)
    consume(0, nblk - 2)
    consume(1, nblk - 1)

    lq = lqk_ref[...]
    lam = (jnp.exp(jnp.sum(lq[0:1] * lq[1:2], axis=1, keepdims=True))
           - jnp.exp(jnp.sum(lq[2:3] * lq[3:4], axis=1, keepdims=True)) + lam_init)
    a0 = acc_sc[0]
    a1 = acc_sc[1]
    o = a0[:vdim] / a0[vdim:vdim + 1] - lam * (a1[:vdim] / a1[vdim:vdim + 1])
    ms = jnp.mean(o * o, axis=0, keepdims=True)
    y = o * lax.rsqrt(ms + EPS) * sg_ref[...] * (1.0 - lam_init)
    o_ref[0] = y.T.astype(o_ref.dtype)


def _diff_attention(qk3, vt, bias_tiles, rel_bias, lambda_qk, subln_g, head_dim, lam_init):
    b, s, two_qk = qk3.shape
    nh = bias_tiles.shape[0]
    tile = bias_tiles.shape[2]
    vdim = vt.shape[0] // nh
    assert vdim == 2 * head_dim == LANES and two_qk == 2 * nh * LANES and s % (2 * tile) == 0
    nblk = s // tile
    ones_rows = 16
    kern = functools.partial(_attn_kernel, head_dim=head_dim, lam_init=lam_init, nblk=nblk)
    return pl.pallas_call(
        kern,
        grid=(b, nh, nblk),
        in_specs=[
            pl.BlockSpec(memory_space=pltpu.SMEM),
            pl.BlockSpec((1, tile, LANES), lambda bi, h, i: (bi, i, h)),
            pl.BlockSpec((1, s, LANES), lambda bi, h, i: (bi, 0, nh + h)),
            pl.BlockSpec((vdim, s), lambda bi, h, i: (h, bi)),
            pl.BlockSpec((1, 5, tile, tile), lambda bi, h, i: (h, 0, 0, 0)),
            pl.BlockSpec(lambda_qk.shape, lambda bi, h, i: (0, 0)),
            pl.BlockSpec((vdim, 1), lambda bi, h, i: (0, 0)),
        ],
        out_specs=pl.BlockSpec((1, tile, vdim), lambda bi, h, i: (bi, i, h)),
        out_shape=jax.ShapeDtypeStruct((b, s, nh * vdim), BF16),
        scratch_shapes=[
            pltpu.VMEM((2, 2, tile, tile), F32),
            pltpu.VMEM((2, 2, 1, tile), F32),
            pltpu.VMEM((2, 1, tile), F32),
            pltpu.VMEM((2, 1, tile), F32),
            pltpu.VMEM((2, vdim + ones_rows, tile), F32),
        ],
        compiler_params=_cparams("parallel", "parallel", "parallel"),
    )(rel_bias, qk3, qk3, vt, bias_tiles, lambda_qk, subln_g.reshape(vdim, 1))


def _sigmoid(x):
    return 1.0 / (1.0 + jnp.exp(-x))


def _merge_kernel(pm_ref, att_ref, gp_ref, ga_ref, wp_ref, wa_ref, o_ref):
    pool_out = jnp.dot(pm_ref[...], wp_ref[...], preferred_element_type=F32)
    attn_out = jnp.dot(att_ref[...], wa_ref[...], preferred_element_type=F32)
    o_ref[...] = (_sigmoid(gp_ref[...]) * pool_out + _sigmoid(ga_ref[...]) * attn_out).astype(o_ref.dtype)


def _merge(pm, att, gates, w_pool_out, w_attn_out):
    n, pwid = pm.shape
    aw = att.shape[1]
    d = w_pool_out.shape[1]
    tm = min(TM_MERGE, n)
    return pl.pallas_call(
        _merge_kernel,
        grid=(n // tm,),
        in_specs=[
            pl.BlockSpec((tm, pwid), lambda i: (i, 0)),
            pl.BlockSpec((tm, aw), lambda i: (i, 0)),
            pl.BlockSpec((tm, d), lambda i: (i, 0)),
            pl.BlockSpec((tm, d), lambda i: (i, 1)),
            pl.BlockSpec((pwid, d), lambda i: (0, 0)),
            pl.BlockSpec((aw, d), lambda i: (0, 0)),
        ],
        out_specs=pl.BlockSpec((tm, d), lambda i: (i, 0)),
        out_shape=jax.ShapeDtypeStruct((n, d), BF16),
        compiler_params=_cparams("parallel"),
    )(pm, att, gates, gates, w_pool_out, w_attn_out)


def _oproj_kernel(m_ref, x_ref, wo_ref, g2_ref, wr_ref, h_ref, t_ref, aff_ref):
    h = x_ref[...] + jnp.dot(m_ref[...], wo_ref[...], preferred_element_type=F32)
    h_ref[...] = h
    ms = jnp.mean(h * h, axis=-1, keepdims=True)
    t = h * lax.rsqrt(ms + EPS) * g2_ref[...]
    half = t.shape[1] // 2
    bits = lax.bitcast_convert_type(t.astype(BF16).astype(F32), U32)
    words = bits[:, :half] | (bits[:, half:] >> 16)
    for p in range(t_ref.shape[0]):
        t_ref[p] = words[:, p * SC_PIECE:(p + 1) * SC_PIECE]
    t_hi, t_lo = _split_bf16(t)
    w2 = wr_ref[...]
    p_hi = jnp.dot(t_hi, w2, preferred_element_type=F32)
    p_lo = jnp.dot(t_lo, w2, preferred_element_type=F32)
    logits = (p_hi[:, :LANES] + p_hi[:, LANES:]) + p_lo[:, :LANES]
    ne = aff_ref.shape[0]
    lt = logits.T[:ne]
    e = jnp.exp(lt - jnp.max(lt, axis=0, keepdims=True))
    aff_ref[...] = e / jnp.sum(e, axis=0, keepdims=True)


def _oproj_router(merged, x2d, w_o, norm2_g, w_router):
    n, d = x2d.shape
    ne = w_router.shape[1]
    assert ne <= LANES
    w_hi, w_lo = _split_bf16(w_router)
    w2 = jnp.zeros((d, 2 * LANES), BF16).at[:, :ne].set(w_hi).at[:, LANES:LANES + ne].set(w_lo)
    tm = min(TM_OPROJ, n)
    return pl.pallas_call(
        _oproj_kernel,
        grid=(n // tm,),
        in_specs=[
            pl.BlockSpec((tm, d), lambda i: (i, 0)),
            pl.BlockSpec((tm, d), lambda i: (i, 0)),
            pl.BlockSpec((d, d), lambda i: (0, 0)),
            pl.BlockSpec((1, d), lambda i: (0, 0)),
            pl.BlockSpec((d, 2 * LANES), lambda i: (0, 0)),
        ],
        out_specs=[
            pl.BlockSpec((tm, d), lambda i: (i, 0)),
            pl.BlockSpec((d // 2 // SC_PIECE, tm, SC_PIECE), lambda i: (0, i, 0)),
            pl.BlockSpec((ne, tm), lambda i: (0, i)),
        ],
        out_shape=[
            jax.ShapeDtypeStruct((n, d), F32),
            jax.ShapeDtypeStruct((d // 2 // SC_PIECE, n, SC_PIECE), U32),
            jax.ShapeDtypeStruct((ne, n), F32),
        ],
        compiler_params=_cparams("parallel"),
    )(merged, x2d, w_o, norm2_g.reshape(1, d), w2)


def _route_kernel(a_ref, idx_ref, gate_ref, *, cap):
    a = a_ref[0]
    nc, wid = a.shape
    keys = lax.bitcast_convert_type(a, I32)

    def search(bit, thr):
        cand = thr | jnp.left_shift(jnp.int32(1), 30 - bit)
        cnt = jnp.sum(jnp.sum((keys >= cand).astype(I32), axis=1, keepdims=True), axis=0, keepdims=True)
        return jnp.where(cnt >= cap, cand, thr)

    thr = lax.fori_loop(0, 31, search, jnp.zeros((1, 1), I32))
    gt = keys > thr
    eq = keys == thr
    n_gt = jnp.sum(jnp.sum(gt.astype(F32), axis=1, keepdims=True), axis=0, keepdims=True)
    need = cap - n_gt

    upper = (lax.broadcasted_iota(I32, (wid, wid), 0) <= lax.broadcasted_iota(I32, (wid, wid), 1)).astype(BF16)
    lower = (lax.broadcasted_iota(I32, (nc, nc), 1) < lax.broadcasted_iota(I32, (nc, nc), 0)).astype(BF16)

    def prefix(mask):
        inc = jnp.dot(mask.astype(BF16), upper, preferred_element_type=F32)
        tot = inc[:, wid - 1:wid]
        off = jnp.dot(lower, jnp.broadcast_to(tot, (nc, LANES)).astype(BF16), preferred_element_type=F32)[:, 0:1]
        return inc, tot, off

    inc_e, _, off_e = prefix(eq)
    tie_rank = off_e + inc_e - eq.astype(F32)
    sel = gt | (eq & (tie_rank < need))
    inc_s, tot_s, off_s = prefix(sel)

    slot = lax.broadcasted_iota(I32, (1, cap), 1).astype(F32)
    chunk_end = off_s + tot_s
    k_row = jnp.sum((chunk_end <= slot).astype(F32), axis=0, keepdims=True)
    onehot = lax.broadcasted_iota(I32, (nc, cap), 0).astype(F32) == k_row
    off_k = jnp.sum(jnp.where(onehot, off_s, 0.0), axis=0, keepdims=True)
    rank = slot - off_k
    onehot_b = onehot.astype(BF16)
    inc_of_slot = jnp.dot(inc_s.T.astype(BF16), onehot_b, preferred_element_type=F32)
    local = jnp.sum((inc_of_slot <= rank).astype(F32), axis=0, keepdims=True)
    idx_ref[0] = (k_row * wid + local).astype(I32)

    at = a.T
    a1 = at.astype(BF16)
    r1 = at - a1.astype(F32)
    a2 = r1.astype(BF16)
    a3 = (r1 - a2.astype(F32)).astype(BF16)
    aff_of_slot = ((jnp.dot(a1, onehot_b, preferred_element_type=F32) + jnp.dot(a2, onehot_b, preferred_element_type=F32))
                   + jnp.dot(a3, onehot_b, preferred_element_type=F32))
    pick = lax.broadcasted_iota(I32, (wid, cap), 0).astype(F32) == local
    gate_ref[0] = jnp.sum(jnp.where(pick, aff_of_slot, 0.0), axis=0, keepdims=True)


def _route(aff_t, cap):
    ne, n = aff_t.shape
    wid = ROUTE_CHUNK
    assert n % wid == 0
    nc = n // wid
    a3 = aff_t.reshape(ne, nc, wid)
    kern = functools.partial(_route_kernel, cap=cap)
    return pl.pallas_call(
        kern,
        grid=(ne,),
        in_specs=[pl.BlockSpec((1, nc, wid), lambda e: (e, 0, 0))],
        out_specs=[pl.BlockSpec((1, 1, cap), lambda e: (e, 0, 0)), pl.BlockSpec((1, 1, cap), lambda e: (e, 0, 0))],
        out_shape=[jax.ShapeDtypeStruct((ne, 1, cap), I32), jax.ShapeDtypeStruct((ne, 1, cap), F32)],
        compiler_params=_cparams("parallel"),
    )(a3)


def _row_copy(src_hbm, tok, buf, r, sem):
    return pltpu.make_async_copy(src_hbm.at[pl.ds(tok, 1)], buf.at[pl.ds(r, 1)], sem)


def _gather_rows_sc(t, idx):
    npieces, n, _ = t.shape
    ne, _, cap = idx.shape
    total = npieces * ne * cap
    piece_idx = (jnp.arange(npieces, dtype=I32)[:, None] * n + idx.reshape(1, -1)).reshape(1, total)
    pieces = t.reshape(npieces * n, SC_PIECE)
    mesh = plsc.VectorSubcoreMesh(core_axis_name="core", subcore_axis_name="subcore")

    @pl.kernel(out_type=jax.ShapeDtypeStruct((total, SC_PIECE), t.dtype), mesh=mesh)
    def gather(x_hbm, i_hbm, o_hbm):
        def body(i_vmem, o_vmem):
            pltpu.sync_copy(x_hbm.at[i_vmem.at[0]], o_vmem)

        pltpu.emit_pipeline(
            body,
            grid=(total // SC_WINDOW,),
            in_specs=[pl.BlockSpec((1, SC_WINDOW), index_map=lambda i: (0, i))],
            out_specs=[pl.BlockSpec((SC_WINDOW, SC_PIECE), index_map=lambda i: (i, 0))],
            core_axis_name=("core", "subcore"),
            dimension_semantics=(pltpu.PARALLEL,),
        )(i_hbm, o_hbm)

    return gather(pieces, piece_idx).reshape(npieces, ne, cap, SC_PIECE)


def _ffn_kernel(x_ref, wg_ref, wu_ref, wd_ref, gate_ref, o_ref, acc, xb, *, nf):
    f = pl.program_id(2)

    @pl.when(f == 0)
    def _():
        acc[...] = jnp.zeros(acc.shape, F32)
        npieces, _, _, piece = x_ref.shape
        half = npieces * piece
        for p in range(npieces):
            words = x_ref[p, 0]
            lo, hi = p * piece, (p + 1) * piece
            xb[:, lo:hi] = lax.bitcast_convert_type(words & jnp.uint32(0xFFFF0000), F32).astype(BF16)
            xb[:, half + lo:half + hi] = lax.bitcast_convert_type(words << 16, F32).astype(BF16)

    x = xb[...]
    a = jnp.dot(x, wg_ref[0].astype(BF16), preferred_element_type=F32)
    b = jnp.dot(x, wu_ref[0].astype(BF16), preferred_element_type=F32)
    hid = (a * _sigmoid(a)) * b
    acc[...] += jnp.dot(hid.astype(BF16), wd_ref[0].astype(BF16), preferred_element_type=F32)

    @pl.when(f == nf - 1)
    def _():
        o_ref[0] = acc[...] * gate_ref[0]


def _expert_ffn(xe, w_gate, w_up, w_down, gate_col):
    npieces, ne, cap, piece = xe.shape
    d = 2 * npieces * piece
    dff = w_gate.shape[2]
    tm = min(TM_FFN, cap)
    tf = min(TF_FFN, dff)
    nf = dff // tf
    kern = functools.partial(_ffn_kernel, nf=nf)
    return pl.pallas_call(
        kern,
        grid=(ne, cap // tm, nf),
        in_specs=[
            pl.BlockSpec((npieces, 1, tm, piece), lambda e, c, f: (0, e, c, 0)),
            pl.BlockSpec((1, d, tf), lambda e, c, f: (e, 0, f)),
            pl.BlockSpec((1, d, tf), lambda e, c, f: (e, 0, f)),
            pl.BlockSpec((1, tf, d), lambda e, c, f: (e, f, 0)),
            pl.BlockSpec((1, tm, 1), lambda e, c, f: (e, c, 0)),
        ],
        out_specs=pl.BlockSpec((1, tm, d), lambda e, c, f: (e, c, 0)),
        out_shape=jax.ShapeDtypeStruct((ne, cap, d), F32),
        scratch_shapes=[pltpu.VMEM((tm, d), F32), pltpu.VMEM((tm, d), BF16)],
        compiler_params=_cparams("parallel", "parallel", "arbitrary"),
    )(xe, w_gate, w_up, w_down, gate_col)


def _scatter_kernel(idx_ref, idx_next_ref, ye_ref, h_hbm, y_hbm, buf, sem_in, sem_out, *, rows, steps):
    del h_hbm
    c = pl.program_id(1)
    slot = c % 2

    def fetch(iref, s):
        def body(r, carry):
            _row_copy(y_hbm, iref[0, 0, r], buf.at[s], r, sem_in.at[s]).start()
            return carry
        lax.fori_loop(0, rows, body, 0, unroll=DMA_UNROLL)

    def row_put(tok, s, r):
        return pltpu.make_async_copy(buf.at[s].at[pl.ds(r, 1)], y_hbm.at[pl.ds(tok, 1)], sem_out.at[s])

    def put_done(s):
        def body(r, carry):
            row_put(0, s, r).wait()
            return carry
        lax.fori_loop(0, rows, body, 0, unroll=DMA_UNROLL)

    @pl.when(c == 0)
    def _():
        fetch(idx_ref, 0)

    @pl.when(c >= 1)
    def _():
        put_done(1 - slot)

    @pl.when(c + 1 < steps)
    def _():
        fetch(idx_next_ref, 1 - slot)

    def fetched(r, carry):
        _row_copy(y_hbm, 0, buf.at[slot], r, sem_in.at[slot]).wait()
        return carry

    lax.fori_loop(0, rows, fetched, 0, unroll=DMA_UNROLL)
    buf[slot] = buf[slot] + ye_ref[0]

    def put(r, carry):
        row_put(idx_ref[0, 0, r], slot, r).start()
        return carry

    lax.fori_loop(0, rows, put, 0, unroll=DMA_UNROLL)

    @pl.when(c == steps - 1)
    def _():
        put_done(slot)


def _scatter_add(h, ye, idx):
    n, d = h.shape
    ne, _, cap = idx.shape
    rows = min(R_GATHER, cap)
    steps = cap // rows
    total = ne * steps
    idx3 = idx.reshape(total, 1, rows)
    kern = functools.partial(_scatter_kernel, rows=rows, steps=steps)
    return pl.pallas_call(
        kern,
        grid=(ne, steps),
        in_specs=[
            pl.BlockSpec((1, 1, rows), lambda e, c: (e * steps + c, 0, 0), memory_space=pltpu.SMEM),
            pl.BlockSpec((1, 1, rows), lambda e, c: (jnp.minimum(e * steps + c + 1, total - 1), 0, 0),
                         memory_space=pltpu.SMEM),
            pl.BlockSpec((1, rows, d), lambda e, c: (e, c, 0)),
            pl.BlockSpec(memory_space=pl.ANY),
        ],
        out_specs=pl.BlockSpec(memory_space=pl.ANY),
        out_shape=jax.ShapeDtypeStruct((n, d), F32),
        input_output_aliases={3: 0},
        scratch_shapes=[pltpu.VMEM((2, rows, d), F32), pltpu.SemaphoreType.DMA((2,)), pltpu.SemaphoreType.DMA((2,))],
        compiler_params=_cparams("arbitrary", "arbitrary"),
    )(idx3, idx3, ye, h)


def _prepare_weights(l, norm1_g, w_in, pool_w, pool_scale, w_pool_out, q_norm_g, k_norm_g, rel_bias, lambda_qk, subln_g,
                     w_attn_out, w_o, norm2_g, w_router, w_gate, w_up, w_down):
    head_dim = q_norm_g.shape[1]
    nh = rel_bias.shape[1]
    vdim = subln_g.shape[1]
    pool_width = pool_scale.shape[1]
    qk_width = nh * 2 * head_dim
    attn_width = nh * vdim
    o1, o2, o3 = pool_width, pool_width + 2 * qk_width, pool_width + 2 * qk_width + attn_width
    w = w_in[l]
    reps = qk_width // head_dim
    qk_gain = jnp.concatenate([jnp.tile(q_norm_g[l] * (head_dim ** -0.5) * LOG2E, reps), jnp.tile(k_norm_g[l], reps)])
    return dict(
        head_dim=head_dim, nh=nh, vdim=vdim, pool_width=pool_width, qk_width=qk_width,
        lam_init=0.8 - 0.6 * math.exp(-0.3 * l),
        norm1_g=norm1_g[l],
        w_uqkg=jnp.concatenate([w[:, :o2], w[:, o3:]], axis=1).astype(BF16),
        w_vt=w[:, o2:o3].T.astype(BF16),
        qk_gain=qk_gain.reshape(1, -1).astype(F32),
        pool_w=pool_w[l].astype(BF16), pool_scale=pool_scale[l],
        w_pool_out=w_pool_out[l].astype(BF16), w_attn_out=w_attn_out[l].astype(BF16), w_o=w_o[l].astype(BF16),
        rel_bias=rel_bias, lambda_qk=lambda_qk[l], subln_g=subln_g[l], norm2_g=norm2_g[l],
        w_router=w_router[l],
        w_gate=w_gate[l], w_up=w_up[l], w_down=w_down[l],
    )


def _layer(x, p, bias_tiles):
    b, s, d = x.shape
    n = b * s
    x2d = x.reshape(n, d)
    xn = _rmsnorm(x2d, p["norm1_g"])
    u, qk, gates = _inproj(xn, p["w_uqkg"], p["qk_gain"], p["head_dim"], p["pool_width"], p["qk_width"])
    vt = _v_transposed(xn, p["w_vt"])
    pm = _pool_mixer(u.reshape(b, s, -1), p["pool_w"], p["pool_scale"]).reshape(n, -1)
    att = _diff_attention(qk.reshape(b, s, -1), vt, bias_tiles, p["rel_bias"], p["lambda_qk"], p["subln_g"],
                          p["head_dim"], p["lam_init"]).reshape(n, -1)
    merged = _merge(pm, att, gates, p["w_pool_out"], p["w_attn_out"])
    h, t, aff_t = _oproj_router(merged, x2d, p["w_o"], p["norm2_g"], p["w_router"])
    ne = aff_t.shape[0]
    cap = max(1, (CAPACITY_FACTOR * n) // ne)
    idx, gate = _route(aff_t, cap)
    xe = _gather_rows_sc(t, idx)
    ye = _expert_ffn(xe, p["w_gate"], p["w_up"], p["w_down"], gate.reshape(ne, cap, 1))
    y = _scatter_add(h, ye, idx)
    return y.reshape(b, s, d)


def kernel(x_prompt, x_sample, norm1_g, w_in, pool_w, pool_scale, w_pool_out, q_norm_g, k_norm_g, rel_bias, lambda_qk,
           subln_g, w_attn_out, w_o, norm2_g, w_router, w_gate, w_up, w_down):
    tile = min(T_ATTN, x_prompt.shape[1], x_sample.shape[1])
    bias_tiles = _bias_tiles(rel_bias, tile)
    hp, hs = x_prompt, x_sample
    for l in range(norm1_g.shape[0]):
        p = _prepare_weights(l, norm1_g, w_in, pool_w, pool_scale, w_pool_out, q_norm_g, k_norm_g, rel_bias, lambda_qk,
                             subln_g, w_attn_out, w_o, norm2_g, w_router, w_gate, w_up, w_down)
        hp = _layer(hp, p, bias_tiles)
        hs = _layer(hs, p, bias_tiles)
    return (hp, hs)
```

```python
import functools
import math

import jax
import jax.numpy as jnp
from jax import lax
from jax.experimental import pallas as pl
from jax.experimental.pallas import tpu as pltpu
from jax.experimental.pallas import tpu_sc as plsc

F32, BF16, I32, U32 = jnp.float32, jnp.bfloat16, jnp.int32, jnp.uint32

EPS = 1e-6
POOL_WINDOWS = (2, 4, 8, 16)
MAX_DISTANCE = 128
CAPACITY_FACTOR = 2
LOG2E = math.log2(math.e)

V7X_VMEM_BYTES = 64 * 1024 * 1024
VMEM_LIMIT_BYTES = V7X_VMEM_BYTES - 8 * 1024 * 1024
LANES = 128
MXU_DIM = 256

TM_NORM = 512
TM_INPROJ = 1024
TN_INPROJ = 1024
TS_POOL = 512
POOL_PAD = 128
T_ATTN = 512
PAIRS_PER_TRIP = 3
TM_MERGE = 512
TM_OPROJ = 512
ROUTE_CHUNK = 256
R_GATHER = 512
DMA_UNROLL = 8
SC_PIECE = 256
SC_WINDOW = 128
TM_FFN = 1024
TF_FFN = 256


def _cparams(*sem):
    return pltpu.CompilerParams(dimension_semantics=sem, vmem_limit_bytes=VMEM_LIMIT_BYTES)


def _split_bf16(a):
    hi = a.astype(BF16)
    lo = (a - hi.astype(F32)).astype(BF16)
    return hi, lo


def _rmsnorm_kernel(x_ref, g_ref, o_ref):
    x = x_ref[...]
    ms = jnp.mean(x * x, axis=-1, keepdims=True)
    o_ref[...] = (x * lax.rsqrt(ms + EPS) * g_ref[...]).astype(o_ref.dtype)


def _rmsnorm(x2d, g):
    n, d = x2d.shape
    tm = min(TM_NORM, n)
    return pl.pallas_call(
        _rmsnorm_kernel,
        grid=(n // tm,),
        in_specs=[pl.BlockSpec((tm, d), lambda i: (i, 0)), pl.BlockSpec((1, d), lambda i: (0, 0))],
        out_specs=pl.BlockSpec((tm, d), lambda i: (i, 0)),
        out_shape=jax.ShapeDtypeStruct((n, d), BF16),
        compiler_params=_cparams("parallel"),
    )(x2d, g.reshape(1, d))


def _inproj_kernel(x_ref, w_ref, gain_ref, bd_ref, u_ref, qk_ref, g_ref, *, n_qk_tiles, head_dim):
    j = pl.program_id(1)
    acc = jnp.dot(x_ref[...], w_ref[...], preferred_element_type=F32)

    @pl.when(j == 0)
    def _():
        u_ref[...] = acc

    @pl.when((j >= 1) & (j <= n_qk_tiles))
    def _():
        bd = bd_ref[...]
        for c0 in range(0, acc.shape[1], MXU_DIM):
            a = acc[:, c0:c0 + MXU_DIM]
            hi, lo = _split_bf16(a * a)
            ss = jnp.dot(hi, bd, preferred_element_type=F32) + jnp.dot(lo, bd, preferred_element_type=F32)
            y = a * lax.rsqrt(ss * (1.0 / head_dim) + EPS) * gain_ref[:, c0:c0 + MXU_DIM]
            qk_ref[:, c0:c0 + MXU_DIM] = y.astype(qk_ref.dtype)

    @pl.when(j > n_qk_tiles)
    def _():
        g_ref[...] = acc


def _inproj(xn, w_uqkg, qk_gain, head_dim, pool_width, qk_width):
    n, d = xn.shape
    tn = TN_INPROJ
    assert pool_width == tn and qk_width % tn == 0
    tm = min(TM_INPROJ, n)
    n_qk_tiles = 2 * qk_width // tn
    n_gate_tiles = (w_uqkg.shape[1] - pool_width - 2 * qk_width) // tn
    n_col = 1 + n_qk_tiles + n_gate_tiles
    rows = lax.broadcasted_iota(I32, (MXU_DIM, MXU_DIM), 0) // head_dim
    cols = lax.broadcasted_iota(I32, (MXU_DIM, MXU_DIM), 1) // head_dim
    bd = (rows == cols).astype(BF16)
    kern = functools.partial(_inproj_kernel, n_qk_tiles=n_qk_tiles, head_dim=head_dim)
    return pl.pallas_call(
        kern,
        grid=(n // tm, n_col),
        in_specs=[
            pl.BlockSpec((tm, d), lambda i, j: (i, 0)),
            pl.BlockSpec((d, tn), lambda i, j: (0, j)),
            pl.BlockSpec((1, tn), lambda i, j: (0, jnp.clip(j - 1, 0, n_qk_tiles - 1))),
            pl.BlockSpec((MXU_DIM, MXU_DIM), lambda i, j: (0, 0)),
        ],
        out_specs=[
            pl.BlockSpec((tm, tn), lambda i, j: (i, 0)),
            pl.BlockSpec((tm, tn), lambda i, j: (i, jnp.clip(j - 1, 0, n_qk_tiles - 1))),
            pl.BlockSpec((tm, tn), lambda i, j: (i, jnp.clip(j - 1 - n_qk_tiles, 0, n_gate_tiles - 1))),
        ],
        out_shape=[
            jax.ShapeDtypeStruct((n, pool_width), F32),
            jax.ShapeDtypeStruct((n, 2 * qk_width), BF16),
            jax.ShapeDtypeStruct((n, n_gate_tiles * tn), F32),
        ],
        compiler_params=_cparams("parallel", "arbitrary"),
    )(xn, w_uqkg, qk_gain, bd)


def _vt_kernel(w_ref, x_ref, o_ref):
    o_ref[...] = lax.dot_general(w_ref[...], x_ref[...], (((1,), (1,)), ((), ())),
                                 preferred_element_type=F32).astype(o_ref.dtype)


def _v_transposed(xn, w_vt):
    n, d = xn.shape
    aw = w_vt.shape[0]
    tm = min(TM_INPROJ, n)
    return pl.pallas_call(
        _vt_kernel,
        grid=(n // tm,),
        in_specs=[pl.BlockSpec((aw, d), lambda i: (0, 0)), pl.BlockSpec((tm, d), lambda i: (i, 0))],
        out_specs=pl.BlockSpec((aw, tm), lambda i: (0, i)),
        out_shape=jax.ShapeDtypeStruct((aw, n), BF16),
        compiler_params=_cparams("parallel"),
    )(w_vt, xn)


def _pool_kernel(cur_ref, prev_ref, next_ref, pw_ref, ps_ref, o_ref, ext_hi, ext_lo, *, seq, ts, group):
    i = pl.program_id(1)
    pad = POOL_PAD
    halo = prev_ref.shape[1]
    cur = cur_ref[0]
    width = cur.shape[1]
    zeros = jnp.zeros((pad - halo, width), BF16)
    for ext, part in ((ext_hi, 0), (ext_lo, 1)):
        ext[0:pad - halo, :] = zeros
        ext[pad - halo:pad, :] = _split_bf16(prev_ref[0])[part]
        ext[pad:pad + ts, :] = _split_bf16(cur)[part]
        ext[pad + ts:pad + ts + halo, :] = _split_bf16(next_ref[0])[part]
        ext[pad + ts + halo:, :] = zeros
    t = i * ts + lax.broadcasted_iota(I32, (ts, ts + 2 * pad), 0)
    p = i * ts - pad + lax.broadcasted_iota(I32, (ts, ts + 2 * pad), 1)
    t_col = i * ts + lax.broadcasted_iota(I32, (ts, 1), 0)
    for gi, w in enumerate(POOL_WINDOWS):
        lo = jnp.maximum(t - w // 2, 0)
        hi = jnp.minimum(t + (w - w // 2), seq)
        band = ((p >= lo) & (p < hi)).astype(BF16)
        cnt = (jnp.minimum(t_col + (w - w // 2), seq) - jnp.maximum(t_col - w // 2, 0)).astype(F32)
        ch = slice(gi * group, (gi + 1) * group)
        wsum = (jnp.dot(band, ext_hi[:, ch], preferred_element_type=F32)
                + jnp.dot(band, ext_lo[:, ch], preferred_element_type=F32))
        pooled = wsum / cnt - cur[:, ch]
        y = jnp.dot(pooled.astype(BF16), pw_ref[gi], preferred_element_type=F32) * ps_ref[:, ch]
        o_ref[0, :, ch] = y.astype(o_ref.dtype)


def _pool_mixer(u3, pool_w, pool_scale):
    b, s, pwid = u3.shape
    ng, group, _ = pool_w.shape
    assert ng == len(POOL_WINDOWS) and ng * group == pwid
    ts = min(TS_POOL, s)
    halo = 16
    assert max(POOL_WINDOWS) // 2 <= halo and s % ts == 0 and ts % halo == 0
    nblk = ts // halo
    last = s // halo - 1
    kern = functools.partial(_pool_kernel, seq=s, ts=ts, group=group)
    return pl.pallas_call(
        kern,
        grid=(b, s // ts),
        in_specs=[
            pl.BlockSpec((1, ts, pwid), lambda bi, i: (bi, i, 0)),
            pl.BlockSpec((1, halo, pwid), lambda bi, i: (bi, jnp.maximum(i * nblk - 1, 0), 0)),
            pl.BlockSpec((1, halo, pwid), lambda bi, i: (bi, jnp.minimum((i + 1) * nblk, last), 0)),
            pl.BlockSpec((ng, group, group), lambda bi, i: (0, 0, 0)),
            pl.BlockSpec((1, pwid), lambda bi, i: (0, 0)),
        ],
        out_specs=pl.BlockSpec((1, ts, pwid), lambda bi, i: (bi, i, 0)),
        out_shape=jax.ShapeDtypeStruct((b, s, pwid), BF16),
        scratch_shapes=[pltpu.VMEM((ts + 2 * POOL_PAD, pwid), BF16), pltpu.VMEM((ts + 2 * POOL_PAD, pwid), BF16)],
        compiler_params=_cparams("parallel", "parallel"),
    )(u3, u3, u3, pool_w, pool_scale.reshape(1, pwid))


def _bucket_thresholds(num_buckets):
    half = num_buckets // 2
    max_exact = half // 2
    ratio = MAX_DISTANCE // max_exact
    assert ratio * max_exact == MAX_DISTANCE
    steps = half - max_exact
    thr = []
    for jj in range(1, steps):
        n = max_exact
        while n ** steps < (max_exact ** steps) * (ratio ** jj):
            n += 1
        thr.append(n)
    return half, max_exact, thr


def _bias_kernel(tab_ref, o_ref, *, tile, num_buckets):
    h = pl.program_id(0)
    delta = pl.program_id(1) - 2
    half, max_exact, thr = _bucket_thresholds(num_buckets)
    kk = lax.broadcasted_iota(I32, (tile, tile), 0)
    qq = lax.broadcasted_iota(I32, (tile, tile), 1)
    rel = delta * tile + kk - qq
    n = jnp.abs(rel)
    large = jnp.full((tile, tile), max_exact, I32)
    for th in thr:
        large = large + (n >= th).astype(I32)
    bucket = jnp.where(n < max_exact, n, large) + jnp.where(rel > 0, half, 0)
    out = jnp.zeros((tile, tile), F32)
    for bkt in range(num_buckets):
        out = jnp.where(bucket == bkt, tab_ref[bkt, h], out)
    o_ref[0, 0] = out * LOG2E


def _bias_tiles(rel_bias, tile):
    nb, nh = rel_bias.shape
    assert tile >= MAX_DISTANCE
    kern = functools.partial(_bias_kernel, tile=tile, num_buckets=nb)
    return pl.pallas_call(
        kern,
        grid=(nh, 5),
        in_specs=[pl.BlockSpec(memory_space=pltpu.SMEM)],
        out_specs=pl.BlockSpec((1, 1, tile, tile), lambda h, dd: (h, dd, 0, 0)),
        out_shape=jax.ShapeDtypeStruct((nh, 5, tile, tile), F32),
        compiler_params=_cparams("parallel", "parallel"),
    )(rel_bias)


def _attn_kernel(tab_ref, q_ref, k_ref, vt_ref, b_ref, lqk_ref, sg_ref, o_ref,
                 s_sc, mb_sc, off_sc, m_sc, acc_sc, *, head_dim, lam_init, nblk):
    h = pl.program_id(1)
    i = pl.program_id(2)
    tile = q_ref.shape[1]
    vdim = vt_ref.shape[0]
    nsteps = nblk // 2
    q = q_ref[0]
    lane = lax.broadcasted_iota(I32, (tile, q.shape[1]), 1)
    ones = jnp.ones((acc_sc.shape[1] - vdim, tile), BF16)
    nb = tab_ref.shape[0]

    def start(blk):
        return blk * tile if isinstance(blk, int) else pl.multiple_of(blk * tile, tile)

    def scores(slot, blk, far):
        k = k_ref[0, pl.ds(start(blk), tile), :]
        if far:
            off = jnp.where(blk < i, tab_ref[nb // 2 - 1, h], tab_ref[nb - 1, h]) * LOG2E
        else:
            off = 0.0
            bias = b_ref[0, jnp.clip(blk - i, -2, 2) + 2]
        off_sc[slot] = jnp.full((1, tile), off, F32)
        for c in range(2):
            kc = jnp.where((lane >= c * head_dim) & (lane < (c + 1) * head_dim), k, jnp.zeros_like(k))
            s = lax.dot_general(kc, q, (((1,), (1,)), ((), ())), preferred_element_type=F32)
            if not far:
                s = s + bias
            s_sc[slot, c] = s
            mb_sc[slot, c] = jnp.max(s, axis=0, keepdims=True) + off

    def consume(slot, blk):
        vt = vt_ref[:, pl.ds(start(blk), tile)]
        vt_ones = jnp.concatenate([vt, ones], axis=0)
        for c in range(2):
            m_prev = m_sc[c]
            m_new = jnp.maximum(m_prev, mb_sc[slot, c])
            alpha = jnp.exp2(m_prev - m_new)
            p = jnp.exp2(s_sc[slot, c] - (m_new - off_sc[slot]))
            acc_sc[c] = alpha * acc_sc[c] + jnp.dot(vt_ones, p.astype(BF16), preferred_element_type=F32)
            m_sc[c] = m_new

    m_sc[...] = jnp.full(m_sc.shape, -1e30, F32)
    acc_sc[...] = jnp.zeros(acc_sc.shape, F32)
    scores(0, 0, False)

    def pairs(j0, npairs):
        first = 2 * j0 + 1
        last = 2 * (j0 + npairs)
        all_far = (first - i >= 2) | (i - last >= 2)

        def run(far):
            for jj in range(npairs):
                j = j0 + jj
                scores(1, 2 * j + 1, far)
                consume(0, 2 * j)
                scores(0, 2 * j + 2, far)
                consume(1, 2 * j + 1)

        @pl.when(all_far)
        def _():
            run(True)

        @pl.when(jnp.logical_not(all_far))
        def _():
            run(False)

    group = PAIRS_PER_TRIP
    ngroups = (nsteps - 1) // group

    def trip(g, carry):
        pairs(g * group, group)
        return carry

    lax.fori_loop(0, ngroups, trip, 0)
    if (nsteps - 1) % group:
        pairs(ngroups * group, (nsteps - 1) % group)
    scores(1, nblk - 1, False)
    consume(0, nblk - 2)
    consume(1, nblk - 1)

    lq = lqk_ref[...]
    lam = (jnp.exp(jnp.sum(lq[0:1] * lq[1:2], axis=1, keepdims=True))
           - jnp.exp(jnp.sum(lq[2:3] * lq[3:4], axis=1, keepdims=True)) + lam_init)
    a0 = acc_sc[0]
    a1 = acc_sc[1]
    o = a0[:vdim] / a0[vdim:vdim + 1] - lam * (a1[:vdim] / a1[vdim:vdim + 1])
    ms = jnp.mean(o * o, axis=0, keepdims=True)
    y = o * lax.rsqrt(ms + EPS) * sg_ref[...] * (1.0 - lam_init)
    o_ref[0] = y.T.astype(o_ref.dtype)


def _diff_attention(qk3, vt, bias_tiles, rel_bias, lambda_qk, subln_g, head_dim, lam_init):
    b, s, two_qk = qk3.shape
    nh = bias_tiles.shape[0]
    tile = bias_tiles.shape[2]
    vdim = vt.shape[0] // nh
    assert vdim == 2 * head_dim == LANES and two_qk == 2 * nh * LANES and s % (2 * tile) == 0
    nblk = s // tile
    ones_rows = 16
    kern = functools.partial(_attn_kernel, head_dim=head_dim, lam_init=lam_init, nblk=nblk)
    return pl.pallas_call(
        kern,
        grid=(b, nh, nblk),
        in_specs=[
            pl.BlockSpec(memory_space=pltpu.SMEM),
            pl.BlockSpec((1, tile, LANES), lambda bi, h, i: (bi, i, h)),
            pl.BlockSpec((1, s, LANES), lambda bi, h, i: (bi, 0, nh + h)),
            pl.BlockSpec((vdim, s), lambda bi, h, i: (h, bi)),
            pl.BlockSpec((1, 5, tile, tile), lambda bi, h, i: (h, 0, 0, 0)),
            pl.BlockSpec(lambda_qk.shape, lambda bi, h, i: (0, 0)),
            pl.BlockSpec((vdim, 1), lambda bi, h, i: (0, 0)),
        ],
        out_specs=pl.BlockSpec((1, tile, vdim), lambda bi, h, i: (bi, i, h)),
        out_shape=jax.ShapeDtypeStruct((b, s, nh * vdim), BF16),
        scratch_shapes=[
            pltpu.VMEM((2, 2, tile, tile), F32),
            pltpu.VMEM((2, 2, 1, tile), F32),
            pltpu.VMEM((2, 1, tile), F32),
            pltpu.VMEM((2, 1, tile), F32),
            pltpu.VMEM((2, vdim + ones_rows, tile), F32),
        ],
        compiler_params=_cparams("parallel", "parallel", "parallel"),
    )(rel_bias, qk3, qk3, vt, bias_tiles, lambda_qk, subln_g.reshape(vdim, 1))


def _sigmoid(x):
    return 1.0 / (1.0 + jnp.exp(-x))


def _merge_kernel(pm_ref, att_ref, gp_ref, ga_ref, wp_ref, wa_ref, o_ref):
    pool_out = jnp.dot(pm_ref[...], wp_ref[...], preferred_element_type=F32)
    attn_out = jnp.dot(att_ref[...], wa_ref[...], preferred_element_type=F32)
    o_ref[...] = (_sigmoid(gp_ref[...]) * pool_out + _sigmoid(ga_ref[...]) * attn_out).astype(o_ref.dtype)


def _merge(pm, att, gates, w_pool_out, w_attn_out):
    n, pwid = pm.shape
    aw = att.shape[1]
    d = w_pool_out.shape[1]
    tm = min(TM_MERGE, n)
    return pl.pallas_call(
        _merge_kernel,
        grid=(n // tm,),
        in_specs=[
            pl.BlockSpec((tm, pwid), lambda i: (i, 0)),
            pl.BlockSpec((tm, aw), lambda i: (i, 0)),
            pl.BlockSpec((tm, d), lambda i: (i, 0)),
            pl.BlockSpec((tm, d), lambda i: (i, 1)),
            pl.BlockSpec((pwid, d), lambda i: (0, 0), pipeline_mode=pl.Buffered(1)),
            pl.BlockSpec((aw, d), lambda i: (0, 0), pipeline_mode=pl.Buffered(1)),
        ],
        out_specs=pl.BlockSpec((tm, d), lambda i: (i, 0)),
        out_shape=jax.ShapeDtypeStruct((n, d), BF16),
        compiler_params=_cparams("parallel"),
    )(pm, att, gates, gates, w_pool_out, w_attn_out)


def _oproj_kernel(m_ref, x_ref, wo_ref, g2_ref, wr_ref, h_ref, t_ref, aff_ref):
    h = x_ref[...] + jnp.dot(m_ref[...], wo_ref[...], preferred_element_type=F32)
    h_ref[...] = h
    ms = jnp.mean(h * h, axis=-1, keepdims=True)
    t = h * lax.rsqrt(ms + EPS) * g2_ref[...]
    half = t.shape[1] // 2
    bits = lax.bitcast_convert_type(t.astype(BF16).astype(F32), U32)
    words = bits[:, :half] | (bits[:, half:] >> 16)
    for p in range(t_ref.shape[0]):
        t_ref[p] = words[:, p * SC_PIECE:(p + 1) * SC_PIECE]
    t_hi, t_lo = _split_bf16(t)
    w2 = wr_ref[...]
    p_hi = jnp.dot(t_hi, w2, preferred_element_type=F32)
    p_lo = jnp.dot(t_lo, w2, preferred_element_type=F32)
    logits = (p_hi[:, :LANES] + p_hi[:, LANES:]) + p_lo[:, :LANES]
    ne = aff_ref.shape[0]
    lt = logits.T[:ne]
    e = jnp.exp(lt - jnp.max(lt, axis=0, keepdims=True))
    aff_ref[...] = e / jnp.sum(e, axis=0, keepdims=True)


def _oproj_router(merged, x2d, w_o, norm2_g, w_router):
    n, d = x2d.shape
    ne = w_router.shape[1]
    assert ne <= LANES
    w_hi, w_lo = _split_bf16(w_router)
    w2 = jnp.zeros((d, 2 * LANES), BF16).at[:, :ne].set(w_hi).at[:, LANES:LANES + ne].set(w_lo)
    tm = min(TM_OPROJ, n)
    return pl.pallas_call(
        _oproj_kernel,
        grid=(n // tm,),
        in_specs=[
            pl.BlockSpec((tm, d), lambda i: (i, 0)),
            pl.BlockSpec((tm, d), lambda i: (i, 0)),
            pl.BlockSpec((d, d), lambda i: (0, 0), pipeline_mode=pl.Buffered(1)),
            pl.BlockSpec((1, d), lambda i: (0, 0)),
            pl.BlockSpec((d, 2 * LANES), lambda i: (0, 0), pipeline_mode=pl.Buffered(1)),
        ],
        out_specs=[
            pl.BlockSpec((tm, d), lambda i: (i, 0)),
            pl.BlockSpec((d // 2 // SC_PIECE, tm, SC_PIECE), lambda i: (0, i, 0)),
            pl.BlockSpec((ne, tm), lambda i: (0, i)),
        ],
        out_shape=[
            jax.ShapeDtypeStruct((n, d), F32),
            jax.ShapeDtypeStruct((d // 2 // SC_PIECE, n, SC_PIECE), U32),
            jax.ShapeDtypeStruct((ne, n), F32),
        ],
        compiler_params=_cparams("parallel"),
    )(merged, x2d, w_o, norm2_g.reshape(1, d), w2)


def _route_kernel(a_ref, idx_ref, gate_ref, *, cap):
    a = a_ref[0]
    nc, wid = a.shape
    keys = lax.bitcast_convert_type(a, I32)

    def search(bit, thr):
        cand = thr | jnp.left_shift(jnp.int32(1), 30 - bit)
        cnt = jnp.sum(jnp.sum((keys >= cand).astype(I32), axis=1, keepdims=True), axis=0, keepdims=True)
        return jnp.where(cnt >= cap, cand, thr)

    thr = lax.fori_loop(0, 31, search, jnp.zeros((1, 1), I32))
    gt = keys > thr
    eq = keys == thr
    n_gt = jnp.sum(jnp.sum(gt.astype(F32), axis=1, keepdims=True), axis=0, keepdims=True)
    need = cap - n_gt

    upper = (lax.broadcasted_iota(I32, (wid, wid), 0) <= lax.broadcasted_iota(I32, (wid, wid), 1)).astype(BF16)
    lower = (lax.broadcasted_iota(I32, (nc, nc), 1) < lax.broadcasted_iota(I32, (nc, nc), 0)).astype(BF16)

    def prefix(mask):
        inc = jnp.dot(mask.astype(BF16), upper, preferred_element_type=F32)
        tot = inc[:, wid - 1:wid]
        off = jnp.dot(lower, jnp.broadcast_to(tot, (nc, LANES)).astype(BF16), preferred_element_type=F32)[:, 0:1]
        return inc, tot, off

    inc_e, _, off_e = prefix(eq)
    tie_rank = off_e + inc_e - eq.astype(F32)
    sel = gt | (eq & (tie_rank < need))
    inc_s, tot_s, off_s = prefix(sel)

    slot = lax.broadcasted_iota(I32, (1, cap), 1).astype(F32)
    chunk_end = off_s + tot_s
    k_row = jnp.sum((chunk_end <= slot).astype(F32), axis=0, keepdims=True)
    onehot = lax.broadcasted_iota(I32, (nc, cap), 0).astype(F32) == k_row
    off_k = jnp.sum(jnp.where(onehot, off_s, 0.0), axis=0, keepdims=True)
    rank = slot - off_k
    onehot_b = onehot.astype(BF16)
    inc_of_slot = jnp.dot(inc_s.T.astype(BF16), onehot_b, preferred_element_type=F32)
    local = jnp.sum((inc_of_slot <= rank).astype(F32), axis=0, keepdims=True)
    idx_ref[0] = (k_row * wid + local).astype(I32)

    at = a.T
    a1 = at.astype(BF16)
    r1 = at - a1.astype(F32)
    a2 = r1.astype(BF16)
    a3 = (r1 - a2.astype(F32)).astype(BF16)
    aff_of_slot = ((jnp.dot(a1, onehot_b, preferred_element_type=F32) + jnp.dot(a2, onehot_b, preferred_element_type=F32))
                   + jnp.dot(a3, onehot_b, preferred_element_type=F32))
    pick = lax.broadcasted_iota(I32, (wid, cap), 0).astype(F32) == local
    gate_ref[0] = jnp.sum(jnp.where(pick, aff_of_slot, 0.0), axis=0, keepdims=True)


def _route(aff_t, cap):
    ne, n = aff_t.shape
    wid = ROUTE_CHUNK
    assert n % wid == 0
    nc = n // wid
    a3 = aff_t.reshape(ne, nc, wid)
    kern = functools.partial(_route_kernel, cap=cap)
    return pl.pallas_call(
        kern,
        grid=(ne,),
        in_specs=[pl.BlockSpec((1, nc, wid), lambda e: (e, 0, 0))],
        out_specs=[pl.BlockSpec((1, 1, cap), lambda e: (e, 0, 0)), pl.BlockSpec((1, 1, cap), lambda e: (e, 0, 0))],
        out_shape=[jax.ShapeDtypeStruct((ne, 1, cap), I32), jax.ShapeDtypeStruct((ne, 1, cap), F32)],
        compiler_params=_cparams("parallel"),
    )(a3)


def _row_copy(src_hbm, tok, buf, r, sem):
    return pltpu.make_async_copy(src_hbm.at[pl.ds(tok, 1)], buf.at[pl.ds(r, 1)], sem)


def _gather_rows_sc(t, idx):
    npieces, n, _ = t.shape
    ne, _, cap = idx.shape
    total = npieces * ne * cap
    piece_idx = (jnp.arange(npieces, dtype=I32)[:, None] * n + idx.reshape(1, -1)).reshape(1, total)
    pieces = t.reshape(npieces * n, SC_PIECE)
    mesh = plsc.VectorSubcoreMesh(core_axis_name="core", subcore_axis_name="subcore")

    @pl.kernel(out_type=jax.ShapeDtypeStruct((total, SC_PIECE), t.dtype), mesh=mesh)
    def gather(x_hbm, i_hbm, o_hbm):
        def body(i_vmem, o_vmem):
            pltpu.sync_copy(x_hbm.at[i_vmem.at[0]], o_vmem)

        pltpu.emit_pipeline(
            body,
            grid=(total // SC_WINDOW,),
            in_specs=[pl.BlockSpec((1, SC_WINDOW), index_map=lambda i: (0, i))],
            out_specs=[pl.BlockSpec((SC_WINDOW, SC_PIECE), index_map=lambda i: (i, 0))],
            core_axis_name=("core", "subcore"),
            dimension_semantics=(pltpu.PARALLEL,),
        )(i_hbm, o_hbm)

    return gather(pieces, piece_idx).reshape(npieces, ne, cap, SC_PIECE)


def _ffn_kernel(x_ref, wg_ref, wu_ref, wd_ref, gate_ref, o_ref, acc, xb, *, nf):
    f = pl.program_id(2)

    @pl.when(f == 0)
    def _():
        acc[...] = jnp.zeros(acc.shape, F32)
        npieces, _, _, piece = x_ref.shape
        half = npieces * piece
        for p in range(npieces):
            words = x_ref[p, 0]
            lo, hi = p * piece, (p + 1) * piece
            xb[:, lo:hi] = lax.bitcast_convert_type(words & jnp.uint32(0xFFFF0000), F32).astype(BF16)
            xb[:, half + lo:half + hi] = lax.bitcast_convert_type(words << 16, F32).astype(BF16)

    x = xb[...]
    a = jnp.dot(x, wg_ref[0].astype(BF16), preferred_element_type=F32)
    b = jnp.dot(x, wu_ref[0].astype(BF16), preferred_element_type=F32)
    hid = (a * _sigmoid(a)) * b
    acc[...] += jnp.dot(hid.astype(BF16), wd_ref[0].astype(BF16), preferred_element_type=F32)

    @pl.when(f == nf - 1)
    def _():
        o_ref[0] = acc[...] * gate_ref[0]


def _expert_ffn(xe, w_gate, w_up, w_down, gate_col):
    npieces, ne, cap, piece = xe.shape
    d = 2 * npieces * piece
    dff = w_gate.shape[2]
    tm = min(TM_FFN, cap)
    tf = min(TF_FFN, dff)
    nf = dff // tf
    kern = functools.partial(_ffn_kernel, nf=nf)
    return pl.pallas_call(
        kern,
        grid=(ne, cap // tm, nf),
        in_specs=[
            pl.BlockSpec((npieces, 1, tm, piece), lambda e, c, f: (0, e, c, 0)),
            pl.BlockSpec((1, d, tf), lambda e, c, f: (e, 0, f)),
            pl.BlockSpec((1, d, tf), lambda e, c, f: (e, 0, f)),
            pl.BlockSpec((1, tf, d), lambda e, c, f: (e, f, 0)),
            pl.BlockSpec((1, tm, 1), lambda e, c, f: (e, c, 0)),
        ],
        out_specs=pl.BlockSpec((1, tm, d), lambda e, c, f: (e, c, 0)),
        out_shape=jax.ShapeDtypeStruct((ne, cap, d), F32),
        scratch_shapes=[pltpu.VMEM((tm, d), F32), pltpu.VMEM((tm, d), BF16)],
        compiler_params=_cparams("parallel", "parallel", "arbitrary"),
    )(xe, w_gate, w_up, w_down, gate_col)


def _scatter_kernel(idx_ref, idx_next_ref, ye_ref, h_hbm, y_hbm, buf, sem_in, sem_out, *, rows, steps):
    del h_hbm
    c = pl.program_id(1)
    slot = c % 2

    def fetch(iref, s):
        def body(r, carry):
            _row_copy(y_hbm, iref[0, 0, r], buf.at[s], r, sem_in.at[s]).start()
            return carry
        lax.fori_loop(0, rows, body, 0, unroll=DMA_UNROLL)

    def row_put(tok, s, r):
        return pltpu.make_async_copy(buf.at[s].at[pl.ds(r, 1)], y_hbm.at[pl.ds(tok, 1)], sem_out.at[s])

    def put_done(s):
        def body(r, carry):
            row_put(0, s, r).wait()
            return carry
        lax.fori_loop(0, rows, body, 0, unroll=DMA_UNROLL)

    @pl.when(c == 0)
    def _():
        fetch(idx_ref, 0)

    @pl.when(c >= 1)
    def _():
        put_done(1 - slot)

    @pl.when(c + 1 < steps)
    def _():
        fetch(idx_next_ref, 1 - slot)

    def fetched(r, carry):
        _row_copy(y_hbm, 0, buf.at[slot], r, sem_in.at[slot]).wait()
        return carry

    lax.fori_loop(0, rows, fetched, 0, unroll=DMA_UNROLL)
    buf[slot] = buf[slot] + ye_ref[0]

    def put(r, carry):
        row_put(idx_ref[0, 0, r], slot, r).start()
        return carry

    lax.fori_loop(0, rows, put, 0, unroll=DMA_UNROLL)

    @pl.when(c == steps - 1)
    def _():
        put_done(slot)


def _scatter_add(h, ye, idx):
    n, d = h.shape
    ne, _, cap = idx.shape
    rows = min(R_GATHER, cap)
    steps = cap // rows
    total = ne * steps
    idx3 = idx.reshape(total, 1, rows)
    kern = functools.partial(_scatter_kernel, rows=rows, steps=steps)
    return pl.pallas_call(
        kern,
        grid=(ne, steps),
        in_specs=[
            pl.BlockSpec((1, 1, rows), lambda e, c: (e * steps + c, 0, 0), memory_space=pltpu.SMEM),
            pl.BlockSpec((1, 1, rows), lambda e, c: (jnp.minimum(e * steps + c + 1, total - 1), 0, 0),
                         memory_space=pltpu.SMEM),
            pl.BlockSpec((1, rows, d), lambda e, c: (e, c, 0)),
            pl.BlockSpec(memory_space=pl.ANY),
        ],
        out_specs=pl.BlockSpec(memory_space=pl.ANY),
        out_shape=jax.ShapeDtypeStruct((n, d), F32),
        input_output_aliases={3: 0},
        scratch_shapes=[pltpu.VMEM((2, rows, d), F32), pltpu.SemaphoreType.DMA((2,)), pltpu.SemaphoreType.DMA((2,))],
        compiler_params=_cparams("arbitrary", "arbitrary"),
    )(idx3, idx3, ye, h)


def _prepare_weights(l, norm1_g, w_in, pool_w, pool_scale, w_pool_out, q_norm_g, k_norm_g, rel_bias, lambda_qk, subln_g,
                     w_attn_out, w_o, norm2_g, w_router, w_gate, w_up, w_down):
    head_dim = q_norm_g.shape[1]
    nh = rel_bias.shape[1]
    vdim = subln_g.shape[1]
    pool_width = pool_scale.shape[1]
    qk_width = nh * 2 * head_dim
    attn_width = nh * vdim
    o1, o2, o3 = pool_width, pool_width + 2 * qk_width, pool_width + 2 * qk_width + attn_width
    w = w_in[l]
    reps = qk_width // head_dim
    qk_gain = jnp.concatenate([jnp.tile(q_norm_g[l] * (head_dim ** -0.5) * LOG2E, reps), jnp.tile(k_norm_g[l], reps)])
    return dict(
        head_dim=head_dim, nh=nh, vdim=vdim, pool_width=pool_width, qk_width=qk_width,
        lam_init=0.8 - 0.6 * math.exp(-0.3 * l),
        norm1_g=norm1_g[l],
        w_uqkg=jnp.concatenate([w[:, :o2], w[:, o3:]], axis=1).astype(BF16),
        w_vt=w[:, o2:o3].T.astype(BF16),
        qk_gain=qk_gain.reshape(1, -1).astype(F32),
        pool_w=pool_w[l].astype(BF16), pool_scale=pool_scale[l],
        w_pool_out=w_pool_out[l].astype(BF16), w_attn_out=w_attn_out[l].astype(BF16), w_o=w_o[l].astype(BF16),
        rel_bias=rel_bias, lambda_qk=lambda_qk[l], subln_g=subln_g[l], norm2_g=norm2_g[l],
        w_router=w_router[l],
        w_gate=w_gate[l], w_up=w_up[l], w_down=w_down[l],
    )


def _layer(x, p, bias_tiles):
    b, s, d = x.shape
    n = b * s
    x2d = x.reshape(n, d)
    xn = _rmsnorm(x2d, p["norm1_g"])
    u, qk, gates = _inproj(xn, p["w_uqkg"], p["qk_gain"], p["head_dim"], p["pool_width"], p["qk_width"])
    vt = _v_transposed(xn, p["w_vt"])
    pm = _pool_mixer(u.reshape(b, s, -1), p["pool_w"], p["pool_scale"]).reshape(n, -1)
    att = _diff_attention(qk.reshape(b, s, -1), vt, bias_tiles, p["rel_bias"], p["lambda_qk"], p["subln_g"],
                          p["head_dim"], p["lam_init"]).reshape(n, -1)
    merged = _merge(pm, att, gates, p["w_pool_out"], p["w_attn_out"])
    h, t, aff_t = _oproj_router(merged, x2d, p["w_o"], p["norm2_g"], p["w_router"])
    ne = aff_t.shape[0]
    cap = max(1, (CAPACITY_FACTOR * n) // ne)
    idx, gate = _route(aff_t, cap)
    xe = _gather_rows_sc(t, idx)
    ye = _expert_ffn(xe, p["w_gate"], p["w_up"], p["w_down"], gate.reshape(ne, cap, 1))
    y = _scatter_add(h, ye, idx)
    return y.reshape(b, s, d)


def kernel(x_prompt, x_sample, norm1_g, w_in, pool_w, pool_scale, w_pool_out, q_norm_g, k_norm_g, rel_bias, lambda_qk,
           subln_g, w_attn_out, w_o, norm2_g, w_router, w_gate, w_up, w_down):
    tile = min(T_ATTN, x_prompt.shape[1], x_sample.shape[1])
    bias_tiles = _bias_tiles(rel_bias, tile)
    hp, hs = x_prompt, x_sample
    for l in range(norm1_g.shape[0]):
        p = _prepare_weights(l, norm1_g, w_in, pool_w, pool_scale, w_pool_out, q_norm_g, k_norm_g, rel_bias, lambda_qk,
                             subln_g, w_attn_out, w_o, norm2_g, w_router, w_gate, w_up, w_down)
        hp = _layer(hp, p, bias_tiles)
        hs = _layer(hs, p, bias_tiles)
    return (hp, hs)
```

```python
import functools
import math

import jax
import jax.numpy as jnp
from jax import lax
from jax.experimental import pallas as pl
from jax.experimental.pallas import tpu as pltpu
from jax.experimental.pallas import tpu_sc as plsc

F32, BF16, I32, U32 = jnp.float32, jnp.bfloat16, jnp.int32, jnp.uint32

EPS = 1e-6
POOL_WINDOWS = (2, 4, 8, 16)
MAX_DISTANCE = 128
CAPACITY_FACTOR = 2
LOG2E = math.log2(math.e)

V7X_VMEM_BYTES = 64 * 1024 * 1024
VMEM_LIMIT_BYTES = V7X_VMEM_BYTES - 8 * 1024 * 1024
LANES = 128
MXU_DIM = 256

TM_NORM = 512
TM_INPROJ = 1024
TN_INPROJ = 1024
TS_POOL = 512
POOL_PAD = 128
T_ATTN = 512
PAIRS_PER_TRIP = 3
TM_MERGE = 512
TM_OPROJ = 512
ROUTE_CHUNK = 256
R_GATHER = 512
DMA_UNROLL = 8
SC_PIECE = 256
SC_WINDOW = 128
TM_FFN = 1024
TF_FFN = 256


def _cparams(*sem):
    return pltpu.CompilerParams(dimension_semantics=sem, vmem_limit_bytes=VMEM_LIMIT_BYTES)


def _split_bf16(a):
    hi = a.astype(BF16)
    lo = (a - hi.astype(F32)).astype(BF16)
    return hi, lo


def _rmsnorm_kernel(x_ref, g_ref, o_ref):
    x = x_ref[...]
    ms = jnp.mean(x * x, axis=-1, keepdims=True)
    o_ref[...] = (x * lax.rsqrt(ms + EPS) * g_ref[...]).astype(o_ref.dtype)


def _rmsnorm(x2d, g):
    n, d = x2d.shape
    tm = min(TM_NORM, n)
    return pl.pallas_call(
        _rmsnorm_kernel,
        grid=(n // tm,),
        in_specs=[pl.BlockSpec((tm, d), lambda i: (i, 0)), pl.BlockSpec((1, d), lambda i: (0, 0))],
        out_specs=pl.BlockSpec((tm, d), lambda i: (i, 0)),
        out_shape=jax.ShapeDtypeStruct((n, d), BF16),
        compiler_params=_cparams("parallel"),
    )(x2d, g.reshape(1, d))


def _inproj_kernel(x_ref, w_ref, gain_ref, bd_ref, u_ref, qk_ref, g_ref, *, n_qk_tiles, head_dim):
    j = pl.program_id(1)
    acc = jnp.dot(x_ref[...], w_ref[...], preferred_element_type=F32)

    @pl.when(j == 0)
    def _():
        u_ref[...] = acc

    @pl.when((j >= 1) & (j <= n_qk_tiles))
    def _():
        bd = bd_ref[...]
        for c0 in range(0, acc.shape[1], MXU_DIM):
            a = acc[:, c0:c0 + MXU_DIM]
            hi, lo = _split_bf16(a * a)
            ss = jnp.dot(hi, bd, preferred_element_type=F32) + jnp.dot(lo, bd, preferred_element_type=F32)
            y = a * lax.rsqrt(ss * (1.0 / head_dim) + EPS) * gain_ref[:, c0:c0 + MXU_DIM]
            qk_ref[:, c0:c0 + MXU_DIM] = y.astype(qk_ref.dtype)

    @pl.when(j > n_qk_tiles)
    def _():
        g_ref[...] = acc


def _inproj(xn, w_uqkg, qk_gain, head_dim, pool_width, qk_width):
    n, d = xn.shape
    tn = TN_INPROJ
    assert pool_width == tn and qk_width % tn == 0
    tm = min(TM_INPROJ, n)
    n_qk_tiles = 2 * qk_width // tn
    n_gate_tiles = (w_uqkg.shape[1] - pool_width - 2 * qk_width) // tn
    n_col = 1 + n_qk_tiles + n_gate_tiles
    rows = lax.broadcasted_iota(I32, (MXU_DIM, MXU_DIM), 0) // head_dim
    cols = lax.broadcasted_iota(I32, (MXU_DIM, MXU_DIM), 1) // head_dim
    bd = (rows == cols).astype(BF16)
    kern = functools.partial(_inproj_kernel, n_qk_tiles=n_qk_tiles, head_dim=head_dim)
    return pl.pallas_call(
        kern,
        grid=(n // tm, n_col),
        in_specs=[
            pl.BlockSpec((tm, d), lambda i, j: (i, 0)),
            pl.BlockSpec((d, tn), lambda i, j: (0, j)),
            pl.BlockSpec((1, tn), lambda i, j: (0, jnp.clip(j - 1, 0, n_qk_tiles - 1))),
            pl.BlockSpec((MXU_DIM, MXU_DIM), lambda i, j: (0, 0)),
        ],
        out_specs=[
            pl.BlockSpec((tm, tn), lambda i, j: (i, 0)),
            pl.BlockSpec((tm, tn), lambda i, j: (i, jnp.clip(j - 1, 0, n_qk_tiles - 1))),
            pl.BlockSpec((tm, tn), lambda i, j: (i, jnp.clip(j - 1 - n_qk_tiles, 0, n_gate_tiles - 1))),
        ],
        out_shape=[
            jax.ShapeDtypeStruct((n, pool_width), F32),
            jax.ShapeDtypeStruct((n, 2 * qk_width), BF16),
            jax.ShapeDtypeStruct((n, n_gate_tiles * tn), F32),
        ],
        compiler_params=_cparams("parallel", "arbitrary"),
    )(xn, w_uqkg, qk_gain, bd)


def _vt_kernel(w_ref, x_ref, o_ref):
    o_ref[...] = lax.dot_general(w_ref[...], x_ref[...], (((1,), (1,)), ((), ())),
                                 preferred_element_type=F32).astype(o_ref.dtype)


def _v_transposed(xn, w_vt):
    n, d = xn.shape
    aw = w_vt.shape[0]
    tm = min(TM_INPROJ, n)
    return pl.pallas_call(
        _vt_kernel,
        grid=(n // tm,),
        in_specs=[pl.BlockSpec((aw, d), lambda i: (0, 0)), pl.BlockSpec((tm, d), lambda i: (i, 0))],
        out_specs=pl.BlockSpec((aw, tm), lambda i: (0, i)),
        out_shape=jax.ShapeDtypeStruct((aw, n), BF16),
        compiler_params=_cparams("parallel"),
    )(w_vt, xn)


def _pool_kernel(cur_ref, prev_ref, next_ref, pw_ref, ps_ref, o_ref, ext_hi, ext_lo, *, seq, ts, group):
    i = pl.program_id(1)
    pad = POOL_PAD
    halo = prev_ref.shape[1]
    cur = cur_ref[0]
    width = cur.shape[1]
    zeros = jnp.zeros((pad - halo, width), BF16)
    for ext, part in ((ext_hi, 0), (ext_lo, 1)):
        ext[0:pad - halo, :] = zeros
        ext[pad - halo:pad, :] = _split_bf16(prev_ref[0])[part]
        ext[pad:pad + ts, :] = _split_bf16(cur)[part]
        ext[pad + ts:pad + ts + halo, :] = _split_bf16(next_ref[0])[part]
        ext[pad + ts + halo:, :] = zeros
    t = i * ts + lax.broadcasted_iota(I32, (ts, ts + 2 * pad), 0)
    p = i * ts - pad + lax.broadcasted_iota(I32, (ts, ts + 2 * pad), 1)
    t_col = i * ts + lax.broadcasted_iota(I32, (ts, 1), 0)
    for gi, w in enumerate(POOL_WINDOWS):
        lo = jnp.maximum(t - w // 2, 0)
        hi = jnp.minimum(t + (w - w // 2), seq)
        band = ((p >= lo) & (p < hi)).astype(BF16)
        cnt = (jnp.minimum(t_col + (w - w // 2), seq) - jnp.maximum(t_col - w // 2, 0)).astype(F32)
        ch = slice(gi * group, (gi + 1) * group)
        wsum = (jnp.dot(band, ext_hi[:, ch], preferred_element_type=F32)
                + jnp.dot(band, ext_lo[:, ch], preferred_element_type=F32))
        pooled = wsum / cnt - cur[:, ch]
        y = jnp.dot(pooled.astype(BF16), pw_ref[gi], preferred_element_type=F32) * ps_ref[:, ch]
        o_ref[0, :, ch] = y.astype(o_ref.dtype)


def _pool_mixer(u3, pool_w, pool_scale):
    b, s, pwid = u3.shape
    ng, group, _ = pool_w.shape
    assert ng == len(POOL_WINDOWS) and ng * group == pwid
    ts = min(TS_POOL, s)
    halo = 16
    assert max(POOL_WINDOWS) // 2 <= halo and s % ts == 0 and ts % halo == 0
    nblk = ts // halo
    last = s // halo - 1
    kern = functools.partial(_pool_kernel, seq=s, ts=ts, group=group)
    return pl.pallas_call(
        kern,
        grid=(b, s // ts),
        in_specs=[
            pl.BlockSpec((1, ts, pwid), lambda bi, i: (bi, i, 0)),
            pl.BlockSpec((1, halo, pwid), lambda bi, i: (bi, jnp.maximum(i * nblk - 1, 0), 0)),
            pl.BlockSpec((1, halo, pwid), lambda bi, i: (bi, jnp.minimum((i + 1) * nblk, last), 0)),
            pl.BlockSpec((ng, group, group), lambda bi, i: (0, 0, 0)),
            pl.BlockSpec((1, pwid), lambda bi, i: (0, 0)),
        ],
        out_specs=pl.BlockSpec((1, ts, pwid), lambda bi, i: (bi, i, 0)),
        out_shape=jax.ShapeDtypeStruct((b, s, pwid), BF16),
        scratch_shapes=[pltpu.VMEM((ts + 2 * POOL_PAD, pwid), BF16), pltpu.VMEM((ts + 2 * POOL_PAD, pwid), BF16)],
        compiler_params=_cparams("parallel", "parallel"),
    )(u3, u3, u3, pool_w, pool_scale.reshape(1, pwid))


def _bucket_thresholds(num_buckets):
    half = num_buckets // 2
    max_exact = half // 2
    ratio = MAX_DISTANCE // max_exact
    assert ratio * max_exact == MAX_DISTANCE
    steps = half - max_exact
    thr = []
    for jj in range(1, steps):
        n = max_exact
        while n ** steps < (max_exact ** steps) * (ratio ** jj):
            n += 1
        thr.append(n)
    return half, max_exact, thr


def _bias_kernel(tab_ref, o_ref, *, tile, num_buckets):
    h = pl.program_id(0)
    delta = pl.program_id(1) - 2
    half, max_exact, thr = _bucket_thresholds(num_buckets)
    kk = lax.broadcasted_iota(I32, (tile, tile), 0)
    qq = lax.broadcasted_iota(I32, (tile, tile), 1)
    rel = delta * tile + kk - qq
    n = jnp.abs(rel)
    large = jnp.full((tile, tile), max_exact, I32)
    for th in thr:
        large = large + (n >= th).astype(I32)
    bucket = jnp.where(n < max_exact, n, large) + jnp.where(rel > 0, half, 0)
    out = jnp.zeros((tile, tile), F32)
    for bkt in range(num_buckets):
        out = jnp.where(bucket == bkt, tab_ref[bkt, h], out)
    o_ref[0, 0] = out * LOG2E


def _bias_tiles(rel_bias, tile):
    nb, nh = rel_bias.shape
    assert tile >= MAX_DISTANCE
    kern = functools.partial(_bias_kernel, tile=tile, num_buckets=nb)
    return pl.pallas_call(
        kern,
        grid=(nh, 5),
        in_specs=[pl.BlockSpec(memory_space=pltpu.SMEM)],
        out_specs=pl.BlockSpec((1, 1, tile, tile), lambda h, dd: (h, dd, 0, 0)),
        out_shape=jax.ShapeDtypeStruct((nh, 5, tile, tile), F32),
        compiler_params=_cparams("parallel", "parallel"),
    )(rel_bias)


def _attn_kernel(tab_ref, q_ref, k_ref, vt_ref, b_ref, lqk_ref, sg_ref, o_ref,
                 s_sc, mb_sc, off_sc, m_sc, acc_sc, *, head_dim, lam_init, nblk):
    h = pl.program_id(1)
    i = pl.program_id(2)
    tile = q_ref.shape[1]
    vdim = vt_ref.shape[0]
    nsteps = nblk // 2
    q = q_ref[0]
    lane = lax.broadcasted_iota(I32, (tile, q.shape[1]), 1)
    ones = jnp.ones((acc_sc.shape[1] - vdim, tile), BF16)
    nb = tab_ref.shape[0]

    def start(blk):
        return blk * tile if isinstance(blk, int) else pl.multiple_of(blk * tile, tile)

    def scores(slot, blk, far):
        k = k_ref[0, pl.ds(start(blk), tile), :]
        if far:
            off = jnp.where(blk < i, tab_ref[nb // 2 - 1, h], tab_ref[nb - 1, h]) * LOG2E
        else:
            off = 0.0
            bias = b_ref[0, jnp.clip(blk - i, -2, 2) + 2]
        off_sc[slot] = jnp.full((1, tile), off, F32)
        for c in range(2):
            kc = jnp.where((lane >= c * head_dim) & (lane < (c + 1) * head_dim), k, jnp.zeros_like(k))
            s = lax.dot_general(kc, q, (((1,), (1,)), ((), ())), preferred_element_type=F32)
            if not far:
                s = s + bias
            s_sc[slot, c] = s
            mb_sc[slot, c] = jnp.max(s, axis=0, keepdims=True) + off

    def consume(slot, blk):
        vt = vt_ref[:, pl.ds(start(blk), tile)]
        vt_ones = jnp.concatenate([vt, ones], axis=0)
        for c in range(2):
            m_prev = m_sc[c]
            m_new = jnp.maximum(m_prev, mb_sc[slot, c])
            alpha = jnp.exp2(m_prev - m_new)
            p = jnp.exp2(s_sc[slot, c] - (m_new - off_sc[slot]))
            acc_sc[c] = alpha * acc_sc[c] + jnp.dot(vt_ones, p.astype(BF16), preferred_element_type=F32)
            m_sc[c] = m_new

    m_sc[...] = jnp.full(m_sc.shape, -1e30, F32)
    acc_sc[...] = jnp.zeros(acc_sc.shape, F32)
    scores(0, 0, False)

    def pairs(j0, npairs):
        first = 2 * j0 + 1
        last = 2 * (j0 + npairs)
        all_far = (first - i >= 2) | (i - last >= 2)

        def run(far):
            for jj in range(npairs):
                j = j0 + jj
                scores(1, 2 * j + 1, far)
                consume(0, 2 * j)
                scores(0, 2 * j + 2, far)
                consume(1, 2 * j + 1)

        @pl.when(all_far)
        def _():
            run(True)

        @pl.when(jnp.logical_not(all_far))
        def _():
            run(False)

    group = PAIRS_PER_TRIP if nsteps - 1 >= 2 * PAIRS_PER_TRIP else min(2, PAIRS_PER_TRIP)
    ngroups = (nsteps - 1) // group

    def trip(g, carry):
        pairs(g * group, group)
        return carry

    lax.fori_loop(0, ngroups, trip, 0)
    if (nsteps - 1) % group:
        pairs(ngroups * group, (nsteps - 1) % group)
    scores(1, nblk - 1, False)
    consume(0, nblk - 2)
    consume(1, nblk - 1)

    lq = lqk_ref[...]
    lam = (jnp.exp(jnp.sum(lq[0:1] * lq[1:2], axis=1, keepdims=True))
           - jnp.exp(jnp.sum(lq[2:3] * lq[3:4], axis=1, keepdims=True)) + lam_init)
    a0 = acc_sc[0]
    a1 = acc_sc[1]
    o = a0[:vdim] / a0[vdim:vdim + 1] - lam * (a1[:vdim] / a1[vdim:vdim + 1])
    ms = jnp.mean(o * o, axis=0, keepdims=True)
    y = o * lax.rsqrt(ms + EPS) * sg_ref[...] * (1.0 - lam_init)
    o_ref[0] = y.T.astype(o_ref.dtype)


def _diff_attention(qk3, vt, bias_tiles, rel_bias, lambda_qk, subln_g, head_dim, lam_init):
    b, s, two_qk = qk3.shape
    nh = bias_tiles.shape[0]
    tile = bias_tiles.shape[2]
    vdim = vt.shape[0] // nh
    assert vdim == 2 * head_dim == LANES and two_qk == 2 * nh * LANES and s % (2 * tile) == 0
    nblk = s // tile
    ones_rows = 16
    kern = functools.partial(_attn_kernel, head_dim=head_dim, lam_init=lam_init, nblk=nblk)
    return pl.pallas_call(
        kern,
        grid=(b, nh, nblk),
        in_specs=[
            pl.BlockSpec(memory_space=pltpu.SMEM),
            pl.BlockSpec((1, tile, LANES), lambda bi, h, i: (bi, i, h)),
            pl.BlockSpec((1, s, LANES), lambda bi, h, i: (bi, 0, nh + h)),
            pl.BlockSpec((vdim, s), lambda bi, h, i: (h, bi)),
            pl.BlockSpec((1, 5, tile, tile), lambda bi, h, i: (h, 0, 0, 0)),
            pl.BlockSpec(lambda_qk.shape, lambda bi, h, i: (0, 0)),
            pl.BlockSpec((vdim, 1), lambda bi, h, i: (0, 0)),
        ],
        out_specs=pl.BlockSpec((1, tile, vdim), lambda bi, h, i: (bi, i, h)),
        out_shape=jax.ShapeDtypeStruct((b, s, nh * vdim), BF16),
        scratch_shapes=[
            pltpu.VMEM((2, 2, tile, tile), F32),
            pltpu.VMEM((2, 2, 1, tile), F32),
            pltpu.VMEM((2, 1, tile), F32),
            pltpu.VMEM((2, 1, tile), F32),
            pltpu.VMEM((2, vdim + ones_rows, tile), F32),
        ],
        compiler_params=_cparams("parallel", "parallel", "parallel"),
    )(rel_bias, qk3, qk3, vt, bias_tiles, lambda_qk, subln_g.reshape(vdim, 1))


def _sigmoid(x):
    return 1.0 / (1.0 + jnp.exp(-x))


def _merge_kernel(pm_ref, att_ref, gp_ref, ga_ref, wp_ref, wa_ref, o_ref):
    pool_out = jnp.dot(pm_ref[...], wp_ref[...], preferred_element_type=F32)
    attn_out = jnp.dot(att_ref[...], wa_ref[...], preferred_element_type=F32)
    o_ref[...] = (_sigmoid(gp_ref[...]) * pool_out + _sigmoid(ga_ref[...]) * attn_out).astype(o_ref.dtype)


def _merge(pm, att, gates, w_pool_out, w_attn_out):
    n, pwid = pm.shape
    aw = att.shape[1]
    d = w_pool_out.shape[1]
    tm = min(TM_MERGE, n)
    return pl.pallas_call(
        _merge_kernel,
        grid=(n // tm,),
        in_specs=[
            pl.BlockSpec((tm, pwid), lambda i: (i, 0)),
            pl.BlockSpec((tm, aw), lambda i: (i, 0)),
            pl.BlockSpec((tm, d), lambda i: (i, 0)),
            pl.BlockSpec((tm, d), lambda i: (i, 1)),
            pl.BlockSpec((pwid, d), lambda i: (0, 0), pipeline_mode=pl.Buffered(1)),
            pl.BlockSpec((aw, d), lambda i: (0, 0), pipeline_mode=pl.Buffered(1)),
        ],
        out_specs=pl.BlockSpec((tm, d), lambda i: (i, 0)),
        out_shape=jax.ShapeDtypeStruct((n, d), BF16),
        compiler_params=_cparams("parallel"),
    )(pm, att, gates, gates, w_pool_out, w_attn_out)


def _oproj_kernel(m_ref, x_ref, wo_ref, g2_ref, wr_ref, h_ref, t_ref, aff_ref):
    h = x_ref[...] + jnp.dot(m_ref[...], wo_ref[...], preferred_element_type=F32)
    h_ref[...] = h
    ms = jnp.mean(h * h, axis=-1, keepdims=True)
    t = h * lax.rsqrt(ms + EPS) * g2_ref[...]
    half = t.shape[1] // 2
    bits = lax.bitcast_convert_type(t.astype(BF16).astype(F32), U32)
    words = bits[:, :half] | (bits[:, half:] >> 16)
    for p in range(t_ref.shape[0]):
        t_ref[p] = words[:, p * SC_PIECE:(p + 1) * SC_PIECE]
    t_hi, t_lo = _split_bf16(t)
    w2 = wr_ref[...]
    p_hi = jnp.dot(t_hi, w2, preferred_element_type=F32)
    p_lo = jnp.dot(t_lo, w2, preferred_element_type=F32)
    logits = (p_hi[:, :LANES] + p_hi[:, LANES:]) + p_lo[:, :LANES]
    ne = aff_ref.shape[0]
    lt = logits.T[:ne]
    e = jnp.exp(lt - jnp.max(lt, axis=0, keepdims=True))
    aff_ref[...] = e / jnp.sum(e, axis=0, keepdims=True)


def _oproj_router(merged, x2d, w_o, norm2_g, w_router):
    n, d = x2d.shape
    ne = w_router.shape[1]
    assert ne <= LANES
    w_hi, w_lo = _split_bf16(w_router)
    w2 = jnp.zeros((d, 2 * LANES), BF16).at[:, :ne].set(w_hi).at[:, LANES:LANES + ne].set(w_lo)
    tm = min(TM_OPROJ, n)
    return pl.pallas_call(
        _oproj_kernel,
        grid=(n // tm,),
        in_specs=[
            pl.BlockSpec((tm, d), lambda i: (i, 0)),
            pl.BlockSpec((tm, d), lambda i: (i, 0)),
            pl.BlockSpec((d, d), lambda i: (0, 0), pipeline_mode=pl.Buffered(1)),
            pl.BlockSpec((1, d), lambda i: (0, 0)),
            pl.BlockSpec((d, 2 * LANES), lambda i: (0, 0), pipeline_mode=pl.Buffered(1)),
        ],
        out_specs=[
            pl.BlockSpec((tm, d), lambda i: (i, 0)),
            pl.BlockSpec((d // 2 // SC_PIECE, tm, SC_PIECE), lambda i: (0, i, 0)),
            pl.BlockSpec((ne, tm), lambda i: (0, i)),
        ],
        out_shape=[
            jax.ShapeDtypeStruct((n, d), F32),
            jax.ShapeDtypeStruct((d // 2 // SC_PIECE, n, SC_PIECE), U32),
            jax.ShapeDtypeStruct((ne, n), F32),
        ],
        compiler_params=_cparams("parallel"),
    )(merged, x2d, w_o, norm2_g.reshape(1, d), w2)


def _route_kernel(a_ref, idx_ref, gate_ref, *, cap):
    a = a_ref[0]
    nc, wid = a.shape
    keys = lax.bitcast_convert_type(a, I32)

    def search(bit, thr):
        cand = thr | jnp.left_shift(jnp.int32(1), 30 - bit)
        cnt = jnp.sum(jnp.sum((keys >= cand).astype(I32), axis=1, keepdims=True), axis=0, keepdims=True)
        return jnp.where(cnt >= cap, cand, thr)

    thr = lax.fori_loop(0, 31, search, jnp.zeros((1, 1), I32))
    gt = keys > thr
    eq = keys == thr
    n_gt = jnp.sum(jnp.sum(gt.astype(F32), axis=1, keepdims=True), axis=0, keepdims=True)
    need = cap - n_gt

    upper = (lax.broadcasted_iota(I32, (wid, wid), 0) <= lax.broadcasted_iota(I32, (wid, wid), 1)).astype(BF16)
    lower = (lax.broadcasted_iota(I32, (nc, nc), 1) < lax.broadcasted_iota(I32, (nc, nc), 0)).astype(BF16)

    def prefix(mask):
        inc = jnp.dot(mask.astype(BF16), upper, preferred_element_type=F32)
        tot = inc[:, wid - 1:wid]
        off = jnp.dot(lower, jnp.broadcast_to(tot, (nc, LANES)).astype(BF16), preferred_element_type=F32)[:, 0:1]
        return inc, tot, off

    inc_e, _, off_e = prefix(eq)
    tie_rank = off_e + inc_e - eq.astype(F32)
    sel = gt | (eq & (tie_rank < need))
    inc_s, tot_s, off_s = prefix(sel)

    slot = lax.broadcasted_iota(I32, (1, cap), 1).astype(F32)
    chunk_end = off_s + tot_s
    k_row = jnp.sum((chunk_end <= slot).astype(F32), axis=0, keepdims=True)
    onehot = lax.broadcasted_iota(I32, (nc, cap), 0).astype(F32) == k_row
    off_k = jnp.sum(jnp.where(onehot, off_s, 0.0), axis=0, keepdims=True)
    rank = slot - off_k
    onehot_b = onehot.astype(BF16)
    inc_of_slot = jnp.dot(inc_s.T.astype(BF16), onehot_b, preferred_element_type=F32)
    local = jnp.sum((inc_of_slot <= rank).astype(F32), axis=0, keepdims=True)
    idx_ref[0] = (k_row * wid + local).astype(I32)

    at = a.T
    a1 = at.astype(BF16)
    r1 = at - a1.astype(F32)
    a2 = r1.astype(BF16)
    a3 = (r1 - a2.astype(F32)).astype(BF16)
    aff_of_slot = ((jnp.dot(a1, onehot_b, preferred_element_type=F32) + jnp.dot(a2, onehot_b, preferred_element_type=F32))
                   + jnp.dot(a3, onehot_b, preferred_element_type=F32))
    pick = lax.broadcasted_iota(I32, (wid, cap), 0).astype(F32) == local
    gate_ref[0] = jnp.sum(jnp.where(pick, aff_of_slot, 0.0), axis=0, keepdims=True)


def _route(aff_t, cap):
    ne, n = aff_t.shape
    wid = ROUTE_CHUNK
    assert n % wid == 0
    nc = n // wid
    a3 = aff_t.reshape(ne, nc, wid)
    kern = functools.partial(_route_kernel, cap=cap)
    return pl.pallas_call(
        kern,
        grid=(ne,),
        in_specs=[pl.BlockSpec((1, nc, wid), lambda e: (e, 0, 0))],
        out_specs=[pl.BlockSpec((1, 1, cap), lambda e: (e, 0, 0)), pl.BlockSpec((1, 1, cap), lambda e: (e, 0, 0))],
        out_shape=[jax.ShapeDtypeStruct((ne, 1, cap), I32), jax.ShapeDtypeStruct((ne, 1, cap), F32)],
        compiler_params=_cparams("parallel"),
    )(a3)


def _row_copy(src_hbm, tok, buf, r, sem):
    return pltpu.make_async_copy(src_hbm.at[pl.ds(tok, 1)], buf.at[pl.ds(r, 1)], sem)


def _gather_rows_sc(t, idx):
    npieces, n, _ = t.shape
    ne, _, cap = idx.shape
    total = npieces * ne * cap
    piece_idx = (jnp.arange(npieces, dtype=I32)[:, None] * n + idx.reshape(1, -1)).reshape(1, total)
    pieces = t.reshape(npieces * n, SC_PIECE)
    mesh = plsc.VectorSubcoreMesh(core_axis_name="core", subcore_axis_name="subcore")

    @pl.kernel(out_type=jax.ShapeDtypeStruct((total, SC_PIECE), t.dtype), mesh=mesh)
    def gather(x_hbm, i_hbm, o_hbm):
        def body(i_vmem, o_vmem):
            pltpu.sync_copy(x_hbm.at[i_vmem.at[0]], o_vmem)

        pltpu.emit_pipeline(
            body,
            grid=(total // SC_WINDOW,),
            in_specs=[pl.BlockSpec((1, SC_WINDOW), index_map=lambda i: (0, i))],
            out_specs=[pl.BlockSpec((SC_WINDOW, SC_PIECE), index_map=lambda i: (i, 0))],
            core_axis_name=("core", "subcore"),
            dimension_semantics=(pltpu.PARALLEL,),
        )(i_hbm, o_hbm)

    return gather(pieces, piece_idx).reshape(npieces, ne, cap, SC_PIECE)


def _ffn_kernel(x_ref, wg_ref, wu_ref, wd_ref, gate_ref, o_ref, acc, xb, *, nf):
    f = pl.program_id(2)

    @pl.when(f == 0)
    def _():
        acc[...] = jnp.zeros(acc.shape, F32)
        npieces, _, _, piece = x_ref.shape
        half = npieces * piece
        for p in range(npieces):
            words = x_ref[p, 0]
            lo, hi = p * piece, (p + 1) * piece
            xb[:, lo:hi] = lax.bitcast_convert_type(words & jnp.uint32(0xFFFF0000), F32).astype(BF16)
            xb[:, half + lo:half + hi] = lax.bitcast_convert_type(words << 16, F32).astype(BF16)

    x = xb[...]
    a = jnp.dot(x, wg_ref[0].astype(BF16), preferred_element_type=F32)
    b = jnp.dot(x, wu_ref[0].astype(BF16), preferred_element_type=F32)
    hid = (a * _sigmoid(a)) * b
    acc[...] += jnp.dot(hid.astype(BF16), wd_ref[0].astype(BF16), preferred_element_type=F32)

    @pl.when(f == nf - 1)
    def _():
        o_ref[0] = acc[...] * gate_ref[0]


def _expert_ffn(xe, w_gate, w_up, w_down, gate_col):
    npieces, ne, cap, piece = xe.shape
    d = 2 * npieces * piece
    dff = w_gate.shape[2]
    tm = min(TM_FFN, cap)
    tf = min(TF_FFN, dff)
    nf = dff // tf
    kern = functools.partial(_ffn_kernel, nf=nf)
    return pl.pallas_call(
        kern,
        grid=(ne, cap // tm, nf),
        in_specs=[
            pl.BlockSpec((npieces, 1, tm, piece), lambda e, c, f: (0, e, c, 0)),
            pl.BlockSpec((1, d, tf), lambda e, c, f: (e, 0, f)),
            pl.BlockSpec((1, d, tf), lambda e, c, f: (e, 0, f)),
            pl.BlockSpec((1, tf, d), lambda e, c, f: (e, f, 0)),
            pl.BlockSpec((1, tm, 1), lambda e, c, f: (e, c, 0)),
        ],
        out_specs=pl.BlockSpec((1, tm, d), lambda e, c, f: (e, c, 0)),
        out_shape=jax.ShapeDtypeStruct((ne, cap, d), F32),
        scratch_shapes=[pltpu.VMEM((tm, d), F32), pltpu.VMEM((tm, d), BF16)],
        compiler_params=_cparams("parallel", "parallel", "arbitrary"),
    )(xe, w_gate, w_up, w_down, gate_col)


def _scatter_kernel(idx_ref, idx_next_ref, ye_ref, h_hbm, y_hbm, buf, sem_in, sem_out, *, rows, steps):
    del h_hbm
    c = pl.program_id(1)
    slot = c % 2

    def fetch(iref, s):
        def body(r, carry):
            _row_copy(y_hbm, iref[0, 0, r], buf.at[s], r, sem_in.at[s]).start()
            return carry
        lax.fori_loop(0, rows, body, 0, unroll=DMA_UNROLL)

    def row_put(tok, s, r):
        return pltpu.make_async_copy(buf.at[s].at[pl.ds(r, 1)], y_hbm.at[pl.ds(tok, 1)], sem_out.at[s])

    def put_done(s):
        def body(r, carry):
            row_put(0, s, r).wait()
            return carry
        lax.fori_loop(0, rows, body, 0, unroll=DMA_UNROLL)

    @pl.when(c == 0)
    def _():
        fetch(idx_ref, 0)

    @pl.when(c >= 1)
    def _():
        put_done(1 - slot)

    @pl.when(c + 1 < steps)
    def _():
        fetch(idx_next_ref, 1 - slot)

    def fetched(r, carry):
        _row_copy(y_hbm, 0, buf.at[slot], r, sem_in.at[slot]).wait()
        return carry

    lax.fori_loop(0, rows, fetched, 0, unroll=DMA_UNROLL)
    buf[slot] = buf[slot] + ye_ref[0]

    def put(r, carry):
        row_put(idx_ref[0, 0, r], slot, r).start()
        return carry

    lax.fori_loop(0, rows, put, 0, unroll=DMA_UNROLL)

    @pl.when(c == steps - 1)
    def _():
        put_done(slot)


def _scatter_add(h, ye, idx):
    n, d = h.shape
    ne, _, cap = idx.shape
    rows = min(R_GATHER, cap)
    steps = cap // rows
    total = ne * steps
    idx3 = idx.reshape(total, 1, rows)
    kern = functools.partial(_scatter_kernel, rows=rows, steps=steps)
    return pl.pallas_call(
        kern,
        grid=(ne, steps),
        in_specs=[
            pl.BlockSpec((1, 1, rows), lambda e, c: (e * steps + c, 0, 0), memory_space=pltpu.SMEM),
            pl.BlockSpec((1, 1, rows), lambda e, c: (jnp.minimum(e * steps + c + 1, total - 1), 0, 0),
                         memory_space=pltpu.SMEM),
            pl.BlockSpec((1, rows, d), lambda e, c: (e, c, 0)),
            pl.BlockSpec(memory_space=pl.ANY),
        ],
        out_specs=pl.BlockSpec(memory_space=pl.ANY),
        out_shape=jax.ShapeDtypeStruct((n, d), F32),
        input_output_aliases={3: 0},
        scratch_shapes=[pltpu.VMEM((2, rows, d), F32), pltpu.SemaphoreType.DMA((2,)), pltpu.SemaphoreType.DMA((2,))],
        compiler_params=_cparams("arbitrary", "arbitrary"),
    )(idx3, idx3, ye, h)


def _prepare_weights(l, norm1_g, w_in, pool_w, pool_scale, w_pool_out, q_norm_g, k_norm_g, rel_bias, lambda_qk, subln_g,
                     w_attn_out, w_o, norm2_g, w_router, w_gate, w_up, w_down):
    head_dim = q_norm_g.shape[1]
    nh = rel_bias.shape[1]
    vdim = subln_g.shape[1]
    pool_width = pool_scale.shape[1]
    qk_width = nh * 2 * head_dim
    attn_width = nh * vdim
    o1, o2, o3 = pool_width, pool_width + 2 * qk_width, pool_width + 2 * qk_width + attn_width
    w = w_in[l]
    reps = qk_width // head_dim
    qk_gain = jnp.concatenate([jnp.tile(q_norm_g[l] * (head_dim ** -0.5) * LOG2E, reps), jnp.tile(k_norm_g[l], reps)])
    return dict(
        head_dim=head_dim, nh=nh, vdim=vdim, pool_width=pool_width, qk_width=qk_width,
        lam_init=0.8 - 0.6 * math.exp(-0.3 * l),
        norm1_g=norm1_g[l],
        w_uqkg=jnp.concatenate([w[:, :o2], w[:, o3:]], axis=1).astype(BF16),
        w_vt=w[:, o2:o3].T.astype(BF16),
        qk_gain=qk_gain.reshape(1, -1).astype(F32),
        pool_w=pool_w[l].astype(BF16), pool_scale=pool_scale[l],
        w_pool_out=w_pool_out[l].astype(BF16), w_attn_out=w_attn_out[l].astype(BF16), w_o=w_o[l].astype(BF16),
        rel_bias=rel_bias, lambda_qk=lambda_qk[l], subln_g=subln_g[l], norm2_g=norm2_g[l],
        w_router=w_router[l],
        w_gate=w_gate[l], w_up=w_up[l], w_down=w_down[l],
    )


def _layer(x, p, bias_tiles):
    b, s, d = x.shape
    n = b * s
    x2d = x.reshape(n, d)
    xn = _rmsnorm(x2d, p["norm1_g"])
    u, qk, gates = _inproj(xn, p["w_uqkg"], p["qk_gain"], p["head_dim"], p["pool_width"], p["qk_width"])
    vt = _v_transposed(xn, p["w_vt"])
    pm = _pool_mixer(u.reshape(b, s, -1), p["pool_w"], p["pool_scale"]).reshape(n, -1)
    att = _diff_attention(qk.reshape(b, s, -1), vt, bias_tiles, p["rel_bias"], p["lambda_qk"], p["subln_g"],
                          p["head_dim"], p["lam_init"]).reshape(n, -1)
    merged = _merge(pm, att, gates, p["w_pool_out"], p["w_attn_out"])
    h, t, aff_t = _oproj_router(merged, x2d, p["w_o"], p["norm2_g"], p["w_router"])
    ne = aff_t.shape[0]
    cap = max(1, (CAPACITY_FACTOR * n) // ne)
    idx, gate = _route(aff_t, cap)
    xe = _gather_rows_sc(t, idx)
    ye = _expert_ffn(xe, p["w_gate"], p["w_up"], p["w_down"], gate.reshape(ne, cap, 1))
    y = _scatter_add(h, ye, idx)
    return y.reshape(b, s, d)


def kernel(x_prompt, x_sample, norm1_g, w_in, pool_w, pool_scale, w_pool_out, q_norm_g, k_norm_g, rel_bias, lambda_qk,
           subln_g, w_attn_out, w_o, norm2_g, w_router, w_gate, w_up, w_down):
    tile = min(T_ATTN, x_prompt.shape[1], x_sample.shape[1])
    bias_tiles = _bias_tiles(rel_bias, tile)
    hp, hs = x_prompt, x_sample
    for l in range(norm1_g.shape[0]):
        p = _prepare_weights(l, norm1_g, w_in, pool_w, pool_scale, w_pool_out, q_norm_g, k_norm_g, rel_bias, lambda_qk,
                             subln_g, w_attn_out, w_o, norm2_g, w_router, w_gate, w_up, w_down)
        hp = _layer(hp, p, bias_tiles)
        hs = _layer(hs, p, bias_tiles)
    return (hp, hs)
```

```python
import functools
import math

import jax
import jax.numpy as jnp
from jax import lax
from jax.experimental import pallas as pl
from jax.experimental.pallas import tpu as pltpu
from jax.experimental.pallas import tpu_sc as plsc

F32, BF16, I32, U32 = jnp.float32, jnp.bfloat16, jnp.int32, jnp.uint32

EPS = 1e-6
POOL_WINDOWS = (2, 4, 8, 16)
MAX_DISTANCE = 128
CAPACITY_FACTOR = 2
LOG2E = math.log2(math.e)

V7X_VMEM_BYTES = 64 * 1024 * 1024
VMEM_LIMIT_BYTES = V7X_VMEM_BYTES - 8 * 1024 * 1024
LANES = 128
MXU_DIM = 256

TM_NORM = 512
TM_INPROJ = 1024
TN_INPROJ = 1024
TS_POOL = 512
POOL_PAD = 128
T_ATTN = 512
PAIRS_PER_TRIP = 3
TM_MERGE = 512
TM_OPROJ = 512
ROUTE_CHUNK = 256
R_GATHER = 512
DMA_UNROLL = 8
SC_PIECE = 256
SC_WINDOW = 128
TM_FFN = 2048
TF_FFN = 256
TN_FFN = 512


def _cparams(*sem):
    return pltpu.CompilerParams(dimension_semantics=sem, vmem_limit_bytes=VMEM_LIMIT_BYTES)


def _split_bf16(a):
    hi = a.astype(BF16)
    lo = (a - hi.astype(F32)).astype(BF16)
    return hi, lo


def _rmsnorm_kernel(x_ref, g_ref, o_ref):
    x = x_ref[...]
    ms = jnp.mean(x * x, axis=-1, keepdims=True)
    o_ref[...] = (x * lax.rsqrt(ms + EPS) * g_ref[...]).astype(o_ref.dtype)


def _rmsnorm(x2d, g):
    n, d = x2d.shape
    tm = min(TM_NORM, n)
    return pl.pallas_call(
        _rmsnorm_kernel,
        grid=(n // tm,),
        in_specs=[pl.BlockSpec((tm, d), lambda i: (i, 0)), pl.BlockSpec((1, d), lambda i: (0, 0))],
        out_specs=pl.BlockSpec((tm, d), lambda i: (i, 0)),
        out_shape=jax.ShapeDtypeStruct((n, d), BF16),
        compiler_params=_cparams("parallel"),
    )(x2d, g.reshape(1, d))


def _inproj_kernel(x_ref, w_ref, gain_ref, bd_ref, u_ref, qk_ref, g_ref, *, n_qk_tiles, head_dim):
    j = pl.program_id(1)
    acc = jnp.dot(x_ref[...], w_ref[...], preferred_element_type=F32)

    @pl.when(j == 0)
    def _():
        u_ref[...] = acc

    @pl.when((j >= 1) & (j <= n_qk_tiles))
    def _():
        bd = bd_ref[...]
        for c0 in range(0, acc.shape[1], MXU_DIM):
            a = acc[:, c0:c0 + MXU_DIM]
            hi, lo = _split_bf16(a * a)
            ss = jnp.dot(hi, bd, preferred_element_type=F32) + jnp.dot(lo, bd, preferred_element_type=F32)
            y = a * lax.rsqrt(ss * (1.0 / head_dim) + EPS) * gain_ref[:, c0:c0 + MXU_DIM]
            qk_ref[:, c0:c0 + MXU_DIM] = y.astype(qk_ref.dtype)

    @pl.when(j > n_qk_tiles)
    def _():
        g_ref[...] = acc


def _inproj(xn, w_uqkg, qk_gain, head_dim, pool_width, qk_width):
    n, d = xn.shape
    tn = TN_INPROJ
    assert pool_width == tn and qk_width % tn == 0
    tm = min(TM_INPROJ, n)
    n_qk_tiles = 2 * qk_width // tn
    n_gate_tiles = (w_uqkg.shape[1] - pool_width - 2 * qk_width) // tn
    n_col = 1 + n_qk_tiles + n_gate_tiles
    rows = lax.broadcasted_iota(I32, (MXU_DIM, MXU_DIM), 0) // head_dim
    cols = lax.broadcasted_iota(I32, (MXU_DIM, MXU_DIM), 1) // head_dim
    bd = (rows == cols).astype(BF16)
    kern = functools.partial(_inproj_kernel, n_qk_tiles=n_qk_tiles, head_dim=head_dim)
    return pl.pallas_call(
        kern,
        grid=(n // tm, n_col),
        in_specs=[
            pl.BlockSpec((tm, d), lambda i, j: (i, 0)),
            pl.BlockSpec((d, tn), lambda i, j: (0, j)),
            pl.BlockSpec((1, tn), lambda i, j: (0, jnp.clip(j - 1, 0, n_qk_tiles - 1))),
            pl.BlockSpec((MXU_DIM, MXU_DIM), lambda i, j: (0, 0)),
        ],
        out_specs=[
            pl.BlockSpec((tm, tn), lambda i, j: (i, 0)),
            pl.BlockSpec((tm, tn), lambda i, j: (i, jnp.clip(j - 1, 0, n_qk_tiles - 1))),
            pl.BlockSpec((tm, tn), lambda i, j: (i, jnp.clip(j - 1 - n_qk_tiles, 0, n_gate_tiles - 1))),
        ],
        out_shape=[
            jax.ShapeDtypeStruct((n, pool_width), F32),
            jax.ShapeDtypeStruct((n, 2 * qk_width), BF16),
            jax.ShapeDtypeStruct((n, n_gate_tiles * tn), F32),
        ],
        compiler_params=_cparams("parallel", "arbitrary"),
    )(xn, w_uqkg, qk_gain, bd)


def _vt_kernel(w_ref, x_ref, o_ref):
    o_ref[...] = lax.dot_general(w_ref[...], x_ref[...], (((1,), (1,)), ((), ())),
                                 preferred_element_type=F32).astype(o_ref.dtype)


def _v_transposed(xn, w_vt):
    n, d = xn.shape
    aw = w_vt.shape[0]
    tm = min(TM_INPROJ, n)
    return pl.pallas_call(
        _vt_kernel,
        grid=(n // tm,),
        in_specs=[pl.BlockSpec((aw, d), lambda i: (0, 0)), pl.BlockSpec((tm, d), lambda i: (i, 0))],
        out_specs=pl.BlockSpec((aw, tm), lambda i: (0, i)),
        out_shape=jax.ShapeDtypeStruct((aw, n), BF16),
        compiler_params=_cparams("parallel"),
    )(w_vt, xn)


def _pool_kernel(cur_ref, prev_ref, next_ref, pw_ref, ps_ref, o_ref, ext_hi, ext_lo, *, seq, ts, group):
    i = pl.program_id(1)
    pad = POOL_PAD
    halo = prev_ref.shape[1]
    cur = cur_ref[0]
    width = cur.shape[1]
    zeros = jnp.zeros((pad - halo, width), BF16)
    for ext, part in ((ext_hi, 0), (ext_lo, 1)):
        ext[0:pad - halo, :] = zeros
        ext[pad - halo:pad, :] = _split_bf16(prev_ref[0])[part]
        ext[pad:pad + ts, :] = _split_bf16(cur)[part]
        ext[pad + ts:pad + ts + halo, :] = _split_bf16(next_ref[0])[part]
        ext[pad + ts + halo:, :] = zeros
    t = i * ts + lax.broadcasted_iota(I32, (ts, ts + 2 * pad), 0)
    p = i * ts - pad + lax.broadcasted_iota(I32, (ts, ts + 2 * pad), 1)
    t_col = i * ts + lax.broadcasted_iota(I32, (ts, 1), 0)
    for gi, w in enumerate(POOL_WINDOWS):
        lo = jnp.maximum(t - w // 2, 0)
        hi = jnp.minimum(t + (w - w // 2), seq)
        band = ((p >= lo) & (p < hi)).astype(BF16)
        cnt = (jnp.minimum(t_col + (w - w // 2), seq) - jnp.maximum(t_col - w // 2, 0)).astype(F32)
        ch = slice(gi * group, (gi + 1) * group)
        wsum = (jnp.dot(band, ext_hi[:, ch], preferred_element_type=F32)
                + jnp.dot(band, ext_lo[:, ch], preferred_element_type=F32))
        pooled = wsum / cnt - cur[:, ch]
        y = jnp.dot(pooled.astype(BF16), pw_ref[gi], preferred_element_type=F32) * ps_ref[:, ch]
        o_ref[0, :, ch] = y.astype(o_ref.dtype)


def _pool_mixer(u3, pool_w, pool_scale):
    b, s, pwid = u3.shape
    ng, group, _ = pool_w.shape
    assert ng == len(POOL_WINDOWS) and ng * group == pwid
    ts = min(TS_POOL, s)
    halo = 16
    assert max(POOL_WINDOWS) // 2 <= halo and s % ts == 0 and ts % halo == 0
    nblk = ts // halo
    last = s // halo - 1
    kern = functools.partial(_pool_kernel, seq=s, ts=ts, group=group)
    return pl.pallas_call(
        kern,
        grid=(b, s // ts),
        in_specs=[
            pl.BlockSpec((1, ts, pwid), lambda bi, i: (bi, i, 0)),
            pl.BlockSpec((1, halo, pwid), lambda bi, i: (bi, jnp.maximum(i * nblk - 1, 0), 0)),
            pl.BlockSpec((1, halo, pwid), lambda bi, i: (bi, jnp.minimum((i + 1) * nblk, last), 0)),
            pl.BlockSpec((ng, group, group), lambda bi, i: (0, 0, 0)),
            pl.BlockSpec((1, pwid), lambda bi, i: (0, 0)),
        ],
        out_specs=pl.BlockSpec((1, ts, pwid), lambda bi, i: (bi, i, 0)),
        out_shape=jax.ShapeDtypeStruct((b, s, pwid), BF16),
        scratch_shapes=[pltpu.VMEM((ts + 2 * POOL_PAD, pwid), BF16), pltpu.VMEM((ts + 2 * POOL_PAD, pwid), BF16)],
        compiler_params=_cparams("parallel", "parallel"),
    )(u3, u3, u3, pool_w, pool_scale.reshape(1, pwid))


def _bucket_thresholds(num_buckets):
    half = num_buckets // 2
    max_exact = half // 2
    ratio = MAX_DISTANCE // max_exact
    assert ratio * max_exact == MAX_DISTANCE
    steps = half - max_exact
    thr = []
    for jj in range(1, steps):
        n = max_exact
        while n ** steps < (max_exact ** steps) * (ratio ** jj):
            n += 1
        thr.append(n)
    return half, max_exact, thr


def _bias_kernel(tab_ref, o_ref, *, tile, num_buckets):
    h = pl.program_id(0)
    delta = pl.program_id(1) - 2
    half, max_exact, thr = _bucket_thresholds(num_buckets)
    kk = lax.broadcasted_iota(I32, (tile, tile), 0)
    qq = lax.broadcasted_iota(I32, (tile, tile), 1)
    rel = delta * tile + kk - qq
    n = jnp.abs(rel)
    large = jnp.full((tile, tile), max_exact, I32)
    for th in thr:
        large = large + (n >= th).astype(I32)
    bucket = jnp.where(n < max_exact, n, large) + jnp.where(rel > 0, half, 0)
    out = jnp.zeros((tile, tile), F32)
    for bkt in range(num_buckets):
        out = jnp.where(bucket == bkt, tab_ref[bkt, h], out)
    o_ref[0, 0] = out * LOG2E


def _bias_tiles(rel_bias, tile):
    nb, nh = rel_bias.shape
    assert tile >= MAX_DISTANCE
    kern = functools.partial(_bias_kernel, tile=tile, num_buckets=nb)
    return pl.pallas_call(
        kern,
        grid=(nh, 5),
        in_specs=[pl.BlockSpec(memory_space=pltpu.SMEM)],
        out_specs=pl.BlockSpec((1, 1, tile, tile), lambda h, dd: (h, dd, 0, 0)),
        out_shape=jax.ShapeDtypeStruct((nh, 5, tile, tile), F32),
        compiler_params=_cparams("parallel", "parallel"),
    )(rel_bias)


def _attn_kernel(tab_ref, q_ref, k_ref, vt_ref, b_ref, lqk_ref, sg_ref, o_ref,
                 s_sc, mb_sc, off_sc, m_sc, acc_sc, *, head_dim, lam_init, nblk):
    h = pl.program_id(1)
    i = pl.program_id(2)
    tile = q_ref.shape[1]
    vdim = vt_ref.shape[0]
    nsteps = nblk // 2
    q = q_ref[0]
    lane = lax.broadcasted_iota(I32, (tile, q.shape[1]), 1)
    ones = jnp.ones((acc_sc.shape[1] - vdim, tile), BF16)
    nb = tab_ref.shape[0]

    def start(blk):
        return blk * tile if isinstance(blk, int) else pl.multiple_of(blk * tile, tile)

    def scores(slot, blk, far):
        k = k_ref[0, pl.ds(start(blk), tile), :]
        if far:
            off = jnp.where(blk < i, tab_ref[nb // 2 - 1, h], tab_ref[nb - 1, h]) * LOG2E
        else:
            off = 0.0
            bias = b_ref[0, jnp.clip(blk - i, -2, 2) + 2]
        off_sc[slot] = jnp.full((1, tile), off, F32)
        for c in range(2):
            kc = jnp.where((lane >= c * head_dim) & (lane < (c + 1) * head_dim), k, jnp.zeros_like(k))
            s = lax.dot_general(kc, q, (((1,), (1,)), ((), ())), preferred_element_type=F32)
            if not far:
                s = s + bias
            s_sc[slot, c] = s
            mb_sc[slot, c] = jnp.max(s, axis=0, keepdims=True) + off

    def consume(slot, blk):
        vt = vt_ref[:, pl.ds(start(blk), tile)]
        vt_ones = jnp.concatenate([vt, ones], axis=0)
        for c in range(2):
            m_prev = m_sc[c]
            m_new = jnp.maximum(m_prev, mb_sc[slot, c])
            alpha = jnp.exp2(m_prev - m_new)
            p = jnp.exp2(s_sc[slot, c] - (m_new - off_sc[slot]))
            acc_sc[c] = alpha * acc_sc[c] + jnp.dot(vt_ones, p.astype(BF16), preferred_element_type=F32)
            m_sc[c] = m_new

    m_sc[...] = jnp.full(m_sc.shape, -1e30, F32)
    acc_sc[...] = jnp.zeros(acc_sc.shape, F32)
    scores(0, 0, False)

    def pairs(j0, npairs):
        first = 2 * j0 + 1
        last = 2 * (j0 + npairs)
        all_far = (first - i >= 2) | (i - last >= 2)

        def run(far):
            for jj in range(npairs):
                j = j0 + jj
                scores(1, 2 * j + 1, far)
                consume(0, 2 * j)
                scores(0, 2 * j + 2, far)
                consume(1, 2 * j + 1)

        @pl.when(all_far)
        def _():
            run(True)

        @pl.when(jnp.logical_not(all_far))
        def _():
            run(False)

    group = PAIRS_PER_TRIP if nsteps - 1 >= 2 * PAIRS_PER_TRIP else min(2, PAIRS_PER_TRIP)
    ngroups = (nsteps - 1) // group

    def trip(g, carry):
        pairs(g * group, group)
        return carry

    lax.fori_loop(0, ngroups, trip, 0)
    if (nsteps - 1) % group:
        pairs(ngroups * group, (nsteps - 1) % group)
    scores(1, nblk - 1, False)
    consume(0, nblk - 2)
    consume(1, nblk - 1)

    lq = lqk_ref[...]
    lam = (jnp.exp(jnp.sum(lq[0:1] * lq[1:2], axis=1, keepdims=True))
           - jnp.exp(jnp.sum(lq[2:3] * lq[3:4], axis=1, keepdims=True)) + lam_init)
    a0 = acc_sc[0]
    a1 = acc_sc[1]
    o = a0[:vdim] / a0[vdim:vdim + 1] - lam * (a1[:vdim] / a1[vdim:vdim + 1])
    ms = jnp.mean(o * o, axis=0, keepdims=True)
    y = o * lax.rsqrt(ms + EPS) * sg_ref[...] * (1.0 - lam_init)
    o_ref[0] = y.T.astype(o_ref.dtype)


def _diff_attention(qk3, vt, bias_tiles, rel_bias, lambda_qk, subln_g, head_dim, lam_init):
    b, s, two_qk = qk3.shape
    nh = bias_tiles.shape[0]
    tile = bias_tiles.shape[2]
    vdim = vt.shape[0] // nh
    assert vdim == 2 * head_dim == LANES and two_qk == 2 * nh * LANES and s % (2 * tile) == 0
    nblk = s // tile
    ones_rows = 16
    kern = functools.partial(_attn_kernel, head_dim=head_dim, lam_init=lam_init, nblk=nblk)
    return pl.pallas_call(
        kern,
        grid=(b, nh, nblk),
        in_specs=[
            pl.BlockSpec(memory_space=pltpu.SMEM),
            pl.BlockSpec((1, tile, LANES), lambda bi, h, i: (bi, i, h)),
            pl.BlockSpec((1, s, LANES), lambda bi, h, i: (bi, 0, nh + h)),
            pl.BlockSpec((vdim, s), lambda bi, h, i: (h, bi)),
            pl.BlockSpec((1, 5, tile, tile), lambda bi, h, i: (h, 0, 0, 0)),
            pl.BlockSpec(lambda_qk.shape, lambda bi, h, i: (0, 0)),
            pl.BlockSpec((vdim, 1), lambda bi, h, i: (0, 0)),
        ],
        out_specs=pl.BlockSpec((1, tile, vdim), lambda bi, h, i: (bi, i, h)),
        out_shape=jax.ShapeDtypeStruct((b, s, nh * vdim), BF16),
        scratch_shapes=[
            pltpu.VMEM((2, 2, tile, tile), F32),
            pltpu.VMEM((2, 2, 1, tile), F32),
            pltpu.VMEM((2, 1, tile), F32),
            pltpu.VMEM((2, 1, tile), F32),
            pltpu.VMEM((2, vdim + ones_rows, tile), F32),
        ],
        compiler_params=_cparams("parallel", "parallel", "parallel"),
    )(rel_bias, qk3, qk3, vt, bias_tiles, lambda_qk, subln_g.reshape(vdim, 1))


def _sigmoid(x):
    return 1.0 / (1.0 + jnp.exp(-x))


def _merge_kernel(pm_ref, att_ref, gp_ref, ga_ref, wp_ref, wa_ref, o_ref):
    pool_out = jnp.dot(pm_ref[...], wp_ref[...], preferred_element_type=F32)
    attn_out = jnp.dot(att_ref[...], wa_ref[...], preferred_element_type=F32)
    o_ref[...] = (_sigmoid(gp_ref[...]) * pool_out + _sigmoid(ga_ref[...]) * attn_out).astype(o_ref.dtype)


def _merge(pm, att, gates, w_pool_out, w_attn_out):
    n, pwid = pm.shape
    aw = att.shape[1]
    d = w_pool_out.shape[1]
    tm = min(TM_MERGE, n)
    return pl.pallas_call(
        _merge_kernel,
        grid=(n // tm,),
        in_specs=[
            pl.BlockSpec((tm, pwid), lambda i: (i, 0)),
            pl.BlockSpec((tm, aw), lambda i: (i, 0)),
            pl.BlockSpec((tm, d), lambda i: (i, 0)),
            pl.BlockSpec((tm, d), lambda i: (i, 1)),
            pl.BlockSpec((pwid, d), lambda i: (0, 0), pipeline_mode=pl.Buffered(1)),
            pl.BlockSpec((aw, d), lambda i: (0, 0), pipeline_mode=pl.Buffered(1)),
        ],
        out_specs=pl.BlockSpec((tm, d), lambda i: (i, 0)),
        out_shape=jax.ShapeDtypeStruct((n, d), BF16),
        compiler_params=_cparams("parallel"),
    )(pm, att, gates, gates, w_pool_out, w_attn_out)


def _oproj_kernel(m_ref, x_ref, wo_ref, g2_ref, wr_ref, h_ref, t_ref, aff_ref):
    h = x_ref[...] + jnp.dot(m_ref[...], wo_ref[...], preferred_element_type=F32)
    h_ref[...] = h
    ms = jnp.mean(h * h, axis=-1, keepdims=True)
    t = h * lax.rsqrt(ms + EPS) * g2_ref[...]
    half = t.shape[1] // 2
    bits = lax.bitcast_convert_type(t.astype(BF16).astype(F32), U32)
    words = bits[:, :half] | (bits[:, half:] >> 16)
    for p in range(t_ref.shape[0]):
        t_ref[p] = words[:, p * SC_PIECE:(p + 1) * SC_PIECE]
    t_hi, t_lo = _split_bf16(t)
    w2 = wr_ref[...]
    p_hi = jnp.dot(t_hi, w2, preferred_element_type=F32)
    p_lo = jnp.dot(t_lo, w2, preferred_element_type=F32)
    logits = (p_hi[:, :LANES] + p_hi[:, LANES:]) + p_lo[:, :LANES]
    ne = aff_ref.shape[0]
    lt = logits.T[:ne]
    e = jnp.exp(lt - jnp.max(lt, axis=0, keepdims=True))
    aff_ref[...] = e / jnp.sum(e, axis=0, keepdims=True)


def _oproj_router(merged, x2d, w_o, norm2_g, w_router):
    n, d = x2d.shape
    ne = w_router.shape[1]
    assert ne <= LANES
    w_hi, w_lo = _split_bf16(w_router)
    w2 = jnp.zeros((d, 2 * LANES), BF16).at[:, :ne].set(w_hi).at[:, LANES:LANES + ne].set(w_lo)
    tm = min(TM_OPROJ, n)
    return pl.pallas_call(
        _oproj_kernel,
        grid=(n // tm,),
        in_specs=[
            pl.BlockSpec((tm, d), lambda i: (i, 0)),
            pl.BlockSpec((tm, d), lambda i: (i, 0)),
            pl.BlockSpec((d, d), lambda i: (0, 0), pipeline_mode=pl.Buffered(1)),
            pl.BlockSpec((1, d), lambda i: (0, 0)),
            pl.BlockSpec((d, 2 * LANES), lambda i: (0, 0), pipeline_mode=pl.Buffered(1)),
        ],
        out_specs=[
            pl.BlockSpec((tm, d), lambda i: (i, 0)),
            pl.BlockSpec((d // 2 // SC_PIECE, tm, SC_PIECE), lambda i: (0, i, 0)),
            pl.BlockSpec((ne, tm), lambda i: (0, i)),
        ],
        out_shape=[
            jax.ShapeDtypeStruct((n, d), F32),
            jax.ShapeDtypeStruct((d // 2 // SC_PIECE, n, SC_PIECE), U32),
            jax.ShapeDtypeStruct((ne, n), F32),
        ],
        compiler_params=_cparams("parallel"),
    )(merged, x2d, w_o, norm2_g.reshape(1, d), w2)


def _route_kernel(a_ref, idx_ref, gate_ref, *, cap):
    a = a_ref[0]
    nc, wid = a.shape
    keys = lax.bitcast_convert_type(a, I32)

    def search(bit, thr):
        cand = thr | jnp.left_shift(jnp.int32(1), 30 - bit)
        cnt = jnp.sum(jnp.sum((keys >= cand).astype(I32), axis=1, keepdims=True), axis=0, keepdims=True)
        return jnp.where(cnt >= cap, cand, thr)

    thr = lax.fori_loop(0, 31, search, jnp.zeros((1, 1), I32))
    gt = keys > thr
    eq = keys == thr
    n_gt = jnp.sum(jnp.sum(gt.astype(F32), axis=1, keepdims=True), axis=0, keepdims=True)
    need = cap - n_gt

    upper = (lax.broadcasted_iota(I32, (wid, wid), 0) <= lax.broadcasted_iota(I32, (wid, wid), 1)).astype(BF16)
    lower = (lax.broadcasted_iota(I32, (nc, nc), 1) < lax.broadcasted_iota(I32, (nc, nc), 0)).astype(BF16)

    def prefix(mask):
        inc = jnp.dot(mask.astype(BF16), upper, preferred_element_type=F32)
        tot = inc[:, wid - 1:wid]
        off = jnp.dot(lower, jnp.broadcast_to(tot, (nc, LANES)).astype(BF16), preferred_element_type=F32)[:, 0:1]
        return inc, tot, off

    inc_e, _, off_e = prefix(eq)
    tie_rank = off_e + inc_e - eq.astype(F32)
    sel = gt | (eq & (tie_rank < need))
    inc_s, tot_s, off_s = prefix(sel)

    slot = lax.broadcasted_iota(I32, (1, cap), 1).astype(F32)
    chunk_end = off_s + tot_s
    k_row = jnp.sum((chunk_end <= slot).astype(F32), axis=0, keepdims=True)
    onehot = lax.broadcasted_iota(I32, (nc, cap), 0).astype(F32) == k_row
    off_k = jnp.sum(jnp.where(onehot, off_s, 0.0), axis=0, keepdims=True)
    rank = slot - off_k
    onehot_b = onehot.astype(BF16)
    inc_of_slot = jnp.dot(inc_s.T.astype(BF16), onehot_b, preferred_element_type=F32)
    local = jnp.sum((inc_of_slot <= rank).astype(F32), axis=0, keepdims=True)
    idx_ref[0] = (k_row * wid + local).astype(I32)

    at = a.T
    a1 = at.astype(BF16)
    r1 = at - a1.astype(F32)
    a2 = r1.astype(BF16)
    a3 = (r1 - a2.astype(F32)).astype(BF16)
    aff_of_slot = ((jnp.dot(a1, onehot_b, preferred_element_type=F32) + jnp.dot(a2, onehot_b, preferred_element_type=F32))
                   + jnp.dot(a3, onehot_b, preferred_element_type=F32))
    pick = lax.broadcasted_iota(I32, (wid, cap), 0).astype(F32) == local
    gate_ref[0] = jnp.sum(jnp.where(pick, aff_of_slot, 0.0), axis=0, keepdims=True)


def _route(aff_t, cap):
    ne, n = aff_t.shape
    wid = ROUTE_CHUNK
    assert n % wid == 0
    nc = n // wid
    a3 = aff_t.reshape(ne, nc, wid)
    kern = functools.partial(_route_kernel, cap=cap)
    return pl.pallas_call(
        kern,
        grid=(ne,),
        in_specs=[pl.BlockSpec((1, nc, wid), lambda e: (e, 0, 0))],
        out_specs=[pl.BlockSpec((1, 1, cap), lambda e: (e, 0, 0)), pl.BlockSpec((1, 1, cap), lambda e: (e, 0, 0))],
        out_shape=[jax.ShapeDtypeStruct((ne, 1, cap), I32), jax.ShapeDtypeStruct((ne, 1, cap), F32)],
        compiler_params=_cparams("parallel"),
    )(a3)


def _row_copy(src_hbm, tok, buf, r, sem):
    return pltpu.make_async_copy(src_hbm.at[pl.ds(tok, 1)], buf.at[pl.ds(r, 1)], sem)


def _gather_rows_sc(t, idx):
    npieces, n, _ = t.shape
    ne, _, cap = idx.shape
    total = npieces * ne * cap
    piece_idx = (jnp.arange(npieces, dtype=I32)[:, None] * n + idx.reshape(1, -1)).reshape(1, total)
    pieces = t.reshape(npieces * n, SC_PIECE)
    mesh = plsc.VectorSubcoreMesh(core_axis_name="core", subcore_axis_name="subcore")

    @pl.kernel(out_type=jax.ShapeDtypeStruct((total, SC_PIECE), t.dtype), mesh=mesh)
    def gather(x_hbm, i_hbm, o_hbm):
        def body(i_vmem, o_vmem):
            pltpu.sync_copy(x_hbm.at[i_vmem.at[0]], o_vmem)

        pltpu.emit_pipeline(
            body,
            grid=(total // SC_WINDOW,),
            in_specs=[pl.BlockSpec((1, SC_WINDOW), index_map=lambda i: (0, i))],
            out_specs=[pl.BlockSpec((SC_WINDOW, SC_PIECE), index_map=lambda i: (i, 0))],
            core_axis_name=("core", "subcore"),
            dimension_semantics=(pltpu.PARALLEL,),
        )(i_hbm, o_hbm)

    return gather(pieces, piece_idx).reshape(npieces, ne, cap, SC_PIECE)


def _ffn_up_kernel(x_ref, wg_ref, wu_ref, o_ref, xb):
    @pl.when(pl.program_id(2) == 0)
    def _():
        npieces, _, _, piece = x_ref.shape
        half = npieces * piece
        for p in range(npieces):
            words = x_ref[p, 0]
            lo, hi = p * piece, (p + 1) * piece
            xb[:, lo:hi] = lax.bitcast_convert_type(words & jnp.uint32(0xFFFF0000), F32).astype(BF16)
            xb[:, half + lo:half + hi] = lax.bitcast_convert_type(words << 16, F32).astype(BF16)

    x = xb[...]
    a = jnp.dot(x, wg_ref[0].astype(BF16), preferred_element_type=F32)
    b = jnp.dot(x, wu_ref[0].astype(BF16), preferred_element_type=F32)
    o_ref[0] = ((a * _sigmoid(a)) * b).astype(o_ref.dtype)


def _ffn_down_kernel(h_ref, wd_ref, gate_ref, o_ref):
    o_ref[0] = jnp.dot(h_ref[0], wd_ref[0].astype(BF16), preferred_element_type=F32) * gate_ref[0]


def _expert_ffn(xe, w_gate, w_up, w_down, gate_col):
    npieces, ne, cap, piece = xe.shape
    d = 2 * npieces * piece
    dff = w_gate.shape[2]
    tm = min(TM_FFN, cap)
    tf = min(TF_FFN, dff)
    tn = min(TN_FFN, d)
    hid = pl.pallas_call(
        _ffn_up_kernel,
        grid=(ne, cap // tm, dff // tf),
        in_specs=[
            pl.BlockSpec((npieces, 1, tm, piece), lambda e, c, f: (0, e, c, 0)),
            pl.BlockSpec((1, d, tf), lambda e, c, f: (e, 0, f)),
            pl.BlockSpec((1, d, tf), lambda e, c, f: (e, 0, f)),
        ],
        out_specs=pl.BlockSpec((1, tm, tf), lambda e, c, f: (e, c, f)),
        out_shape=jax.ShapeDtypeStruct((ne, cap, dff), BF16),
        scratch_shapes=[pltpu.VMEM((tm, d), BF16)],
        compiler_params=_cparams("parallel", "parallel", "arbitrary"),
    )(xe, w_gate, w_up)
    return pl.pallas_call(
        _ffn_down_kernel,
        grid=(ne, cap // tm, d // tn),
        in_specs=[
            pl.BlockSpec((1, tm, dff), lambda e, c, n: (e, c, 0)),
            pl.BlockSpec((1, dff, tn), lambda e, c, n: (e, 0, n)),
            pl.BlockSpec((1, tm, 1), lambda e, c, n: (e, c, 0)),
        ],
        out_specs=pl.BlockSpec((1, tm, tn), lambda e, c, n: (e, c, n)),
        out_shape=jax.ShapeDtypeStruct((ne, cap, d), F32),
        compiler_params=_cparams("parallel", "parallel", "parallel"),
    )(hid, w_down, gate_col)


def _scatter_kernel(idx_ref, idx_next_ref, ye_ref, h_hbm, y_hbm, buf, sem_in, sem_out, *, rows, steps):
    del h_hbm
    c = pl.program_id(1)
    slot = c % 2

    def fetch(iref, s):
        def body(r, carry):
            _row_copy(y_hbm, iref[0, 0, r], buf.at[s], r, sem_in.at[s]).start()
            return carry
        lax.fori_loop(0, rows, body, 0, unroll=DMA_UNROLL)

    def row_put(tok, s, r):
        return pltpu.make_async_copy(buf.at[s].at[pl.ds(r, 1)], y_hbm.at[pl.ds(tok, 1)], sem_out.at[s])

    def put_done(s):
        def body(r, carry):
            row_put(0, s, r).wait()
            return carry
        lax.fori_loop(0, rows, body, 0, unroll=DMA_UNROLL)

    @pl.when(c == 0)
    def _():
        fetch(idx_ref, 0)

    @pl.when(c >= 1)
    def _():
        put_done(1 - slot)

    @pl.when(c + 1 < steps)
    def _():
        fetch(idx_next_ref, 1 - slot)

    def fetched(r, carry):
        _row_copy(y_hbm, 0, buf.at[slot], r, sem_in.at[slot]).wait()
        return carry

    lax.fori_loop(0, rows, fetched, 0, unroll=DMA_UNROLL)
    buf[slot] = buf[slot] + ye_ref[0]

    def put(r, carry):
        row_put(idx_ref[0, 0, r], slot, r).start()
        return carry

    lax.fori_loop(0, rows, put, 0, unroll=DMA_UNROLL)

    @pl.when(c == steps - 1)
    def _():
        put_done(slot)


def _scatter_add(h, ye, idx):
    n, d = h.shape
    ne, _, cap = idx.shape
    rows = min(R_GATHER, cap)
    steps = cap // rows
    total = ne * steps
    idx3 = idx.reshape(total, 1, rows)
    kern = functools.partial(_scatter_kernel, rows=rows, steps=steps)
    return pl.pallas_call(
        kern,
        grid=(ne, steps),
        in_specs=[
            pl.BlockSpec((1, 1, rows), lambda e, c: (e * steps + c, 0, 0), memory_space=pltpu.SMEM),
            pl.BlockSpec((1, 1, rows), lambda e, c: (jnp.minimum(e * steps + c + 1, total - 1), 0, 0),
                         memory_space=pltpu.SMEM),
            pl.BlockSpec((1, rows, d), lambda e, c: (e, c, 0)),
            pl.BlockSpec(memory_space=pl.ANY),
        ],
        out_specs=pl.BlockSpec(memory_space=pl.ANY),
        out_shape=jax.ShapeDtypeStruct((n, d), F32),
        input_output_aliases={3: 0},
        scratch_shapes=[pltpu.VMEM((2, rows, d), F32), pltpu.SemaphoreType.DMA((2,)), pltpu.SemaphoreType.DMA((2,))],
        compiler_params=_cparams("arbitrary", "arbitrary"),
    )(idx3, idx3, ye, h)


def _prepare_weights(l, norm1_g, w_in, pool_w, pool_scale, w_pool_out, q_norm_g, k_norm_g, rel_bias, lambda_qk, subln_g,
                     w_attn_out, w_o, norm2_g, w_router, w_gate, w_up, w_down):
    head_dim = q_norm_g.shape[1]
    nh = rel_bias.shape[1]
    vdim = subln_g.shape[1]
    pool_width = pool_scale.shape[1]
    qk_width = nh * 2 * head_dim
    attn_width = nh * vdim
    o1, o2, o3 = pool_width, pool_width + 2 * qk_width, pool_width + 2 * qk_width + attn_width
    w = w_in[l]
    reps = qk_width // head_dim
    qk_gain = jnp.concatenate([jnp.tile(q_norm_g[l] * (head_dim ** -0.5) * LOG2E, reps), jnp.tile(k_norm_g[l], reps)])
    return dict(
        head_dim=head_dim, nh=nh, vdim=vdim, pool_width=pool_width, qk_width=qk_width,
        lam_init=0.8 - 0.6 * math.exp(-0.3 * l),
        norm1_g=norm1_g[l],
        w_uqkg=jnp.concatenate([w[:, :o2], w[:, o3:]], axis=1).astype(BF16),
        w_vt=w[:, o2:o3].T.astype(BF16),
        qk_gain=qk_gain.reshape(1, -1).astype(F32),
        pool_w=pool_w[l].astype(BF16), pool_scale=pool_scale[l],
        w_pool_out=w_pool_out[l].astype(BF16), w_attn_out=w_attn_out[l].astype(BF16), w_o=w_o[l].astype(BF16),
        rel_bias=rel_bias, lambda_qk=lambda_qk[l], subln_g=subln_g[l], norm2_g=norm2_g[l],
        w_router=w_router[l],
        w_gate=w_gate[l], w_up=w_up[l], w_down=w_down[l],
    )


def _layer(x, p, bias_tiles):
    b, s, d = x.shape
    n = b * s
    x2d = x.reshape(n, d)
    xn = _rmsnorm(x2d, p["norm1_g"])
    u, qk, gates = _inproj(xn, p["w_uqkg"], p["qk_gain"], p["head_dim"], p["pool_width"], p["qk_width"])
    vt = _v_transposed(xn, p["w_vt"])
    pm = _pool_mixer(u.reshape(b, s, -1), p["pool_w"], p["pool_scale"]).reshape(n, -1)
    att = _diff_attention(qk.reshape(b, s, -1), vt, bias_tiles, p["rel_bias"], p["lambda_qk"], p["subln_g"],
                          p["head_dim"], p["lam_init"]).reshape(n, -1)
    merged = _merge(pm, att, gates, p["w_pool_out"], p["w_attn_out"])
    h, t, aff_t = _oproj_router(merged, x2d, p["w_o"], p["norm2_g"], p["w_router"])
    ne = aff_t.shape[0]
    cap = max(1, (CAPACITY_FACTOR * n) // ne)
    idx, gate = _route(aff_t, cap)
    xe = _gather_rows_sc(t, idx)
    ye = _expert_ffn(xe, p["w_gate"], p["w_up"], p["w_down"], gate.reshape(ne, cap, 1))
    y = _scatter_add(h, ye, idx)
    return y.reshape(b, s, d)


def kernel(x_prompt, x_sample, norm1_g, w_in, pool_w, pool_scale, w_pool_out, q_norm_g, k_norm_g, rel_bias, lambda_qk,
           subln_g, w_attn_out, w_o, norm2_g, w_router, w_gate, w_up, w_down):
    tile = min(T_ATTN, x_prompt.shape[1], x_sample.shape[1])
    bias_tiles = _bias_tiles(rel_bias, tile)
    hp, hs = x_prompt, x_sample
    for l in range(norm1_g.shape[0]):
        p = _prepare_weights(l, norm1_g, w_in, pool_w, pool_scale, w_pool_out, q_norm_g, k_norm_g, rel_bias, lambda_qk,
                             subln_g, w_attn_out, w_o, norm2_g, w_router, w_gate, w_up, w_down)
        hp = _layer(hp, p, bias_tiles)
        hs = _layer(hs, p, bias_tiles)
    return (hp, hs)
```

```python
import functools
import math

import jax
import jax.numpy as jnp
from jax import lax
from jax.experimental import pallas as pl
from jax.experimental.pallas import tpu as pltpu
from jax.experimental.pallas import tpu_sc as plsc

F32, BF16, I32, U32 = jnp.float32, jnp.bfloat16, jnp.int32, jnp.uint32

EPS = 1e-6
POOL_WINDOWS = (2, 4, 8, 16)
MAX_DISTANCE = 128
CAPACITY_FACTOR = 2
LOG2E = math.log2(math.e)

V7X_VMEM_BYTES = 64 * 1024 * 1024
VMEM_LIMIT_BYTES = V7X_VMEM_BYTES - 8 * 1024 * 1024
LANES = 128
MXU_DIM = 256

TM_NORM = 512
TM_INPROJ = 1024
TN_INPROJ = 1024
TS_POOL = 512
POOL_PAD = 128
T_ATTN = 512
PAIRS_PER_TRIP = 3
TM_MERGE = 512
TM_OPROJ = 512
ROUTE_CHUNK = 256
R_GATHER = 512
DMA_UNROLL = 8
SC_PIECE = 256
SC_WINDOW = 128
TM_FFN = 2048
TF_FFN = 256
TN_FFN = 512


def _cparams(*sem):
    return pltpu.CompilerParams(dimension_semantics=sem, vmem_limit_bytes=VMEM_LIMIT_BYTES)


def _split_bf16(a):
    hi = a.astype(BF16)
    lo = (a - hi.astype(F32)).astype(BF16)
    return hi, lo


def _rmsnorm_kernel(x_ref, g_ref, o_ref):
    x = x_ref[...]
    ms = jnp.mean(x * x, axis=-1, keepdims=True)
    o_ref[...] = (x * lax.rsqrt(ms + EPS) * g_ref[...]).astype(o_ref.dtype)


def _rmsnorm(x2d, g):
    n, d = x2d.shape
    tm = min(TM_NORM, n)
    return pl.pallas_call(
        _rmsnorm_kernel,
        grid=(n // tm,),
        in_specs=[pl.BlockSpec((tm, d), lambda i: (i, 0)), pl.BlockSpec((1, d), lambda i: (0, 0))],
        out_specs=pl.BlockSpec((tm, d), lambda i: (i, 0)),
        out_shape=jax.ShapeDtypeStruct((n, d), BF16),
        compiler_params=_cparams("parallel"),
    )(x2d, g.reshape(1, d))


def _inproj_kernel(x_ref, w_ref, gain_ref, bd_ref, u_ref, qk_ref, g_ref, *, n_qk_tiles, head_dim):
    j = pl.program_id(1)
    acc = jnp.dot(x_ref[...], w_ref[...], preferred_element_type=F32)

    @pl.when(j == 0)
    def _():
        u_ref[...] = acc

    @pl.when((j >= 1) & (j <= n_qk_tiles))
    def _():
        bd = bd_ref[...]
        for c0 in range(0, acc.shape[1], MXU_DIM):
            a = acc[:, c0:c0 + MXU_DIM]
            hi, lo = _split_bf16(a * a)
            ss = jnp.dot(hi, bd, preferred_element_type=F32) + jnp.dot(lo, bd, preferred_element_type=F32)
            y = a * lax.rsqrt(ss * (1.0 / head_dim) + EPS) * gain_ref[:, c0:c0 + MXU_DIM]
            qk_ref[:, c0:c0 + MXU_DIM] = y.astype(qk_ref.dtype)

    @pl.when(j > n_qk_tiles)
    def _():
        g_ref[...] = acc


def _inproj(xn, w_uqkg, qk_gain, head_dim, pool_width, qk_width):
    n, d = xn.shape
    tn = TN_INPROJ
    assert pool_width == tn and qk_width % tn == 0
    tm = min(TM_INPROJ, n)
    n_qk_tiles = 2 * qk_width // tn
    n_gate_tiles = (w_uqkg.shape[1] - pool_width - 2 * qk_width) // tn
    n_col = 1 + n_qk_tiles + n_gate_tiles
    rows = lax.broadcasted_iota(I32, (MXU_DIM, MXU_DIM), 0) // head_dim
    cols = lax.broadcasted_iota(I32, (MXU_DIM, MXU_DIM), 1) // head_dim
    bd = (rows == cols).astype(BF16)
    kern = functools.partial(_inproj_kernel, n_qk_tiles=n_qk_tiles, head_dim=head_dim)
    return pl.pallas_call(
        kern,
        grid=(n // tm, n_col),
        in_specs=[
            pl.BlockSpec((tm, d), lambda i, j: (i, 0)),
            pl.BlockSpec((d, tn), lambda i, j: (0, j)),
            pl.BlockSpec((1, tn), lambda i, j: (0, jnp.clip(j - 1, 0, n_qk_tiles - 1))),
            pl.BlockSpec((MXU_DIM, MXU_DIM), lambda i, j: (0, 0)),
        ],
        out_specs=[
            pl.BlockSpec((tm, tn), lambda i, j: (i, 0)),
            pl.BlockSpec((tm, tn), lambda i, j: (i, jnp.clip(j - 1, 0, n_qk_tiles - 1))),
            pl.BlockSpec((tm, tn), lambda i, j: (i, jnp.clip(j - 1 - n_qk_tiles, 0, n_gate_tiles - 1))),
        ],
        out_shape=[
            jax.ShapeDtypeStruct((n, pool_width), F32),
            jax.ShapeDtypeStruct((n, 2 * qk_width), BF16),
            jax.ShapeDtypeStruct((n, n_gate_tiles * tn), F32),
        ],
        compiler_params=_cparams("parallel", "arbitrary"),
    )(xn, w_uqkg, qk_gain, bd)


def _vt_kernel(w_ref, x_ref, o_ref):
    o_ref[...] = lax.dot_general(w_ref[...], x_ref[...], (((1,), (1,)), ((), ())),
                                 preferred_element_type=F32).astype(o_ref.dtype)


def _v_transposed(xn, w_vt):
    n, d = xn.shape
    aw = w_vt.shape[0]
    tm = min(TM_INPROJ, n)
    return pl.pallas_call(
        _vt_kernel,
        grid=(n // tm,),
        in_specs=[pl.BlockSpec((aw, d), lambda i: (0, 0)), pl.BlockSpec((tm, d), lambda i: (i, 0))],
        out_specs=pl.BlockSpec((aw, tm), lambda i: (0, i)),
        out_shape=jax.ShapeDtypeStruct((aw, n), BF16),
        compiler_params=_cparams("parallel"),
    )(w_vt, xn)


def _pool_kernel(cur_ref, prev_ref, next_ref, pw_ref, ps_ref, o_ref, ext_hi, ext_lo, *, seq, ts, group):
    i = pl.program_id(1)
    pad = POOL_PAD
    halo = prev_ref.shape[1]
    cur = cur_ref[0]
    width = cur.shape[1]
    zeros = jnp.zeros((pad - halo, width), BF16)
    for ext, part in ((ext_hi, 0), (ext_lo, 1)):
        ext[0:pad - halo, :] = zeros
        ext[pad - halo:pad, :] = _split_bf16(prev_ref[0])[part]
        ext[pad:pad + ts, :] = _split_bf16(cur)[part]
        ext[pad + ts:pad + ts + halo, :] = _split_bf16(next_ref[0])[part]
        ext[pad + ts + halo:, :] = zeros
    t = i * ts + lax.broadcasted_iota(I32, (ts, ts + 2 * pad), 0)
    p = i * ts - pad + lax.broadcasted_iota(I32, (ts, ts + 2 * pad), 1)
    t_col = i * ts + lax.broadcasted_iota(I32, (ts, 1), 0)
    for gi, w in enumerate(POOL_WINDOWS):
        lo = jnp.maximum(t - w // 2, 0)
        hi = jnp.minimum(t + (w - w // 2), seq)
        band = ((p >= lo) & (p < hi)).astype(BF16)
        cnt = (jnp.minimum(t_col + (w - w // 2), seq) - jnp.maximum(t_col - w // 2, 0)).astype(F32)
        ch = slice(gi * group, (gi + 1) * group)
        wsum = (jnp.dot(band, ext_hi[:, ch], preferred_element_type=F32)
                + jnp.dot(band, ext_lo[:, ch], preferred_element_type=F32))
        pooled = wsum / cnt - cur[:, ch]
        y = jnp.dot(pooled.astype(BF16), pw_ref[gi], preferred_element_type=F32) * ps_ref[:, ch]
        o_ref[0, :, ch] = y.astype(o_ref.dtype)


def _pool_mixer(u3, pool_w, pool_scale):
    b, s, pwid = u3.shape
    ng, group, _ = pool_w.shape
    assert ng == len(POOL_WINDOWS) and ng * group == pwid
    ts = min(TS_POOL, s)
    halo = 16
    assert max(POOL_WINDOWS) // 2 <= halo and s % ts == 0 and ts % halo == 0
    nblk = ts // halo
    last = s // halo - 1
    kern = functools.partial(_pool_kernel, seq=s, ts=ts, group=group)
    return pl.pallas_call(
        kern,
        grid=(b, s // ts),
        in_specs=[
            pl.BlockSpec((1, ts, pwid), lambda bi, i: (bi, i, 0)),
            pl.BlockSpec((1, halo, pwid), lambda bi, i: (bi, jnp.maximum(i * nblk - 1, 0), 0)),
            pl.BlockSpec((1, halo, pwid), lambda bi, i: (bi, jnp.minimum((i + 1) * nblk, last), 0)),
            pl.BlockSpec((ng, group, group), lambda bi, i: (0, 0, 0)),
            pl.BlockSpec((1, pwid), lambda bi, i: (0, 0)),
        ],
        out_specs=pl.BlockSpec((1, ts, pwid), lambda bi, i: (bi, i, 0)),
        out_shape=jax.ShapeDtypeStruct((b, s, pwid), BF16),
        scratch_shapes=[pltpu.VMEM((ts + 2 * POOL_PAD, pwid), BF16), pltpu.VMEM((ts + 2 * POOL_PAD, pwid), BF16)],
        compiler_params=_cparams("parallel", "parallel"),
    )(u3, u3, u3, pool_w, pool_scale.reshape(1, pwid))


def _bucket_thresholds(num_buckets):
    half = num_buckets // 2
    max_exact = half // 2
    ratio = MAX_DISTANCE // max_exact
    assert ratio * max_exact == MAX_DISTANCE
    steps = half - max_exact
    thr = []
    for jj in range(1, steps):
        n = max_exact
        while n ** steps < (max_exact ** steps) * (ratio ** jj):
            n += 1
        thr.append(n)
    return half, max_exact, thr


def _bias_kernel(tab_ref, o_ref, *, tile, num_buckets):
    h = pl.program_id(0)
    delta = pl.program_id(1) - 2
    half, max_exact, thr = _bucket_thresholds(num_buckets)
    kk = lax.broadcasted_iota(I32, (tile, tile), 0)
    qq = lax.broadcasted_iota(I32, (tile, tile), 1)
    rel = delta * tile + kk - qq
    n = jnp.abs(rel)
    large = jnp.full((tile, tile), max_exact, I32)
    for th in thr:
        large = large + (n >= th).astype(I32)
    bucket = jnp.where(n < max_exact, n, large) + jnp.where(rel > 0, half, 0)
    out = jnp.zeros((tile, tile), F32)
    for bkt in range(num_buckets):
        out = jnp.where(bucket == bkt, tab_ref[bkt, h], out)
    o_ref[0, 0] = out * LOG2E


def _bias_tiles(rel_bias, tile):
    nb, nh = rel_bias.shape
    assert tile >= MAX_DISTANCE
    kern = functools.partial(_bias_kernel, tile=tile, num_buckets=nb)
    return pl.pallas_call(
        kern,
        grid=(nh, 5),
        in_specs=[pl.BlockSpec(memory_space=pltpu.SMEM)],
        out_specs=pl.BlockSpec((1, 1, tile, tile), lambda h, dd: (h, dd, 0, 0)),
        out_shape=jax.ShapeDtypeStruct((nh, 5, tile, tile), F32),
        compiler_params=_cparams("parallel", "parallel"),
    )(rel_bias)


def _attn_kernel(tab_ref, q_ref, k_ref, vt_ref, b_ref, lqk_ref, sg_ref, o_ref,
                 s_sc, mb_sc, off_sc, m_sc, acc_sc, *, head_dim, lam_init, nblk):
    h = pl.program_id(1)
    i = pl.program_id(2)
    tile = q_ref.shape[1]
    vdim = vt_ref.shape[0]
    nsteps = nblk // 2
    q = q_ref[0]
    lane = lax.broadcasted_iota(I32, (tile, q.shape[1]), 1)
    ones = jnp.ones((acc_sc.shape[1] - vdim, tile), BF16)
    nb = tab_ref.shape[0]

    def start(blk):
        return blk * tile if isinstance(blk, int) else pl.multiple_of(blk * tile, tile)

    def scores(slot, blk, far):
        k = k_ref[0, pl.ds(start(blk), tile), :]
        if far:
            off = jnp.where(blk < i, tab_ref[nb // 2 - 1, h], tab_ref[nb - 1, h]) * LOG2E
        else:
            off = 0.0
            bias = b_ref[0, jnp.clip(blk - i, -2, 2) + 2]
        off_sc[slot] = jnp.full((1, tile), off, F32)
        for c in range(2):
            kc = jnp.where((lane >= c * head_dim) & (lane < (c + 1) * head_dim), k, jnp.zeros_like(k))
            s = lax.dot_general(kc, q, (((1,), (1,)), ((), ())), preferred_element_type=F32)
            if not far:
                s = s + bias
            s_sc[slot, c] = s
            mb_sc[slot, c] = jnp.max(s, axis=0, keepdims=True) + off

    def consume(slot, blk):
        vt = vt_ref[:, pl.ds(start(blk), tile)]
        vt_ones = jnp.concatenate([vt, ones], axis=0)
        for c in range(2):
            m_prev = m_sc[c]
            m_new = jnp.maximum(m_prev, mb_sc[slot, c])
            alpha = jnp.exp2(m_prev - m_new)
            p = jnp.exp2(s_sc[slot, c] - (m_new - off_sc[slot]))
            acc_sc[c] = alpha * acc_sc[c] + jnp.dot(vt_ones, p.astype(BF16), preferred_element_type=F32)
            m_sc[c] = m_new

    m_sc[...] = jnp.full(m_sc.shape, -1e30, F32)
    acc_sc[...] = jnp.zeros(acc_sc.shape, F32)
    scores(0, 0, False)

    def pairs(j0, npairs):
        first = 2 * j0 + 1
        last = 2 * (j0 + npairs)
        all_far = (first - i >= 2) | (i - last >= 2)

        def run(far):
            for jj in range(npairs):
                j = j0 + jj
                scores(1, 2 * j + 1, far)
                consume(0, 2 * j)
                scores(0, 2 * j + 2, far)
                consume(1, 2 * j + 1)

        @pl.when(all_far)
        def _():
            run(True)

        @pl.when(jnp.logical_not(all_far))
        def _():
            run(False)

    group = PAIRS_PER_TRIP if nsteps - 1 >= 2 * PAIRS_PER_TRIP else min(2, PAIRS_PER_TRIP)
    ngroups = (nsteps - 1) // group

    def trip(g, carry):
        pairs(g * group, group)
        return carry

    lax.fori_loop(0, ngroups, trip, 0)
    if (nsteps - 1) % group:
        pairs(ngroups * group, (nsteps - 1) % group)
    scores(1, nblk - 1, False)
    consume(0, nblk - 2)
    consume(1, nblk - 1)

    lq = lqk_ref[...]
    lam = (jnp.exp(jnp.sum(lq[0:1] * lq[1:2], axis=1, keepdims=True))
           - jnp.exp(jnp.sum(lq[2:3] * lq[3:4], axis=1, keepdims=True)) + lam_init)
    a0 = acc_sc[0]
    a1 = acc_sc[1]
    o = a0[:vdim] / a0[vdim:vdim + 1] - lam * (a1[:vdim] / a1[vdim:vdim + 1])
    ms = jnp.mean(o * o, axis=0, keepdims=True)
    y = o * lax.rsqrt(ms + EPS) * sg_ref[...] * (1.0 - lam_init)
    o_ref[0] = y.T.astype(o_ref.dtype)


def _diff_attention(qk3, vt, bias_tiles, rel_bias, lambda_qk, subln_g, head_dim, lam_init):
    b, s, two_qk = qk3.shape
    nh = bias_tiles.shape[0]
    tile = bias_tiles.shape[2]
    vdim = vt.shape[0] // nh
    assert vdim == 2 * head_dim == LANES and two_qk == 2 * nh * LANES and s % (2 * tile) == 0
    nblk = s // tile
    ones_rows = 16
    kern = functools.partial(_attn_kernel, head_dim=head_dim, lam_init=lam_init, nblk=nblk)
    return pl.pallas_call(
        kern,
        grid=(b, nh, nblk),
        in_specs=[
            pl.BlockSpec(memory_space=pltpu.SMEM),
            pl.BlockSpec((1, tile, LANES), lambda bi, h, i: (bi, i, h)),
            pl.BlockSpec((1, s, LANES), lambda bi, h, i: (bi, 0, nh + h)),
            pl.BlockSpec((vdim, s), lambda bi, h, i: (h, bi)),
            pl.BlockSpec((1, 5, tile, tile), lambda bi, h, i: (h, 0, 0, 0)),
            pl.BlockSpec(lambda_qk.shape, lambda bi, h, i: (0, 0)),
            pl.BlockSpec((vdim, 1), lambda bi, h, i: (0, 0)),
        ],
        out_specs=pl.BlockSpec((1, tile, vdim), lambda bi, h, i: (bi, i, h)),
        out_shape=jax.ShapeDtypeStruct((b, s, nh * vdim), BF16),
        scratch_shapes=[
            pltpu.VMEM((2, 2, tile, tile), F32),
            pltpu.VMEM((2, 2, 1, tile), F32),
            pltpu.VMEM((2, 1, tile), F32),
            pltpu.VMEM((2, 1, tile), F32),
            pltpu.VMEM((2, vdim + ones_rows, tile), F32),
        ],
        compiler_params=_cparams("parallel", "parallel", "parallel"),
    )(rel_bias, qk3, qk3, vt, bias_tiles, lambda_qk, subln_g.reshape(vdim, 1))


def _sigmoid(x):
    return 1.0 / (1.0 + jnp.exp(-x))


def _merge_kernel(pm_ref, att_ref, gp_ref, ga_ref, wp_ref, wa_ref, o_ref):
    pool_out = jnp.dot(pm_ref[...], wp_ref[...], preferred_element_type=F32)
    attn_out = jnp.dot(att_ref[...], wa_ref[...], preferred_element_type=F32)
    o_ref[...] = (_sigmoid(gp_ref[...]) * pool_out + _sigmoid(ga_ref[...]) * attn_out).astype(o_ref.dtype)


def _merge(pm, att, gates, w_pool_out, w_attn_out):
    n, pwid = pm.shape
    aw = att.shape[1]
    d = w_pool_out.shape[1]
    tm = min(TM_MERGE, n)
    return pl.pallas_call(
        _merge_kernel,
        grid=(n // tm,),
        in_specs=[
            pl.BlockSpec((tm, pwid), lambda i: (i, 0)),
            pl.BlockSpec((tm, aw), lambda i: (i, 0)),
            pl.BlockSpec((tm, d), lambda i: (i, 0)),
            pl.BlockSpec((tm, d), lambda i: (i, 1)),
            pl.BlockSpec((pwid, d), lambda i: (0, 0), pipeline_mode=pl.Buffered(1)),
            pl.BlockSpec((aw, d), lambda i: (0, 0), pipeline_mode=pl.Buffered(1)),
        ],
        out_specs=pl.BlockSpec((tm, d), lambda i: (i, 0)),
        out_shape=jax.ShapeDtypeStruct((n, d), BF16),
        compiler_params=_cparams("parallel"),
    )(pm, att, gates, gates, w_pool_out, w_attn_out)


def _oproj_kernel(m_ref, x_ref, wo_ref, g2_ref, wr_ref, h_ref, t_ref, aff_ref):
    h = x_ref[...] + jnp.dot(m_ref[...], wo_ref[...], preferred_element_type=F32)
    h_ref[...] = h
    ms = jnp.mean(h * h, axis=-1, keepdims=True)
    t = h * lax.rsqrt(ms + EPS) * g2_ref[...]
    half = t.shape[1] // 2
    bits = lax.bitcast_convert_type(t.astype(BF16).astype(F32), U32)
    words = bits[:, :half] | (bits[:, half:] >> 16)
    for p in range(t_ref.shape[0]):
        t_ref[p] = words[:, p * SC_PIECE:(p + 1) * SC_PIECE]
    t_hi, t_lo = _split_bf16(t)
    w2 = wr_ref[...]
    p_hi = jnp.dot(t_hi, w2, preferred_element_type=F32)
    p_lo = jnp.dot(t_lo, w2, preferred_element_type=F32)
    logits = (p_hi[:, :LANES] + p_hi[:, LANES:]) + p_lo[:, :LANES]
    ne = aff_ref.shape[0]
    lt = logits.T[:ne]
    e = jnp.exp(lt - jnp.max(lt, axis=0, keepdims=True))
    aff_ref[...] = e / jnp.sum(e, axis=0, keepdims=True)


def _oproj_router(merged, x2d, w_o, norm2_g, w_router):
    n, d = x2d.shape
    ne = w_router.shape[1]
    assert ne <= LANES
    w_hi, w_lo = _split_bf16(w_router)
    w2 = jnp.zeros((d, 2 * LANES), BF16).at[:, :ne].set(w_hi).at[:, LANES:LANES + ne].set(w_lo)
    tm = min(TM_OPROJ, n)
    return pl.pallas_call(
        _oproj_kernel,
        grid=(n // tm,),
        in_specs=[
            pl.BlockSpec((tm, d), lambda i: (i, 0)),
            pl.BlockSpec((tm, d), lambda i: (i, 0)),
            pl.BlockSpec((d, d), lambda i: (0, 0), pipeline_mode=pl.Buffered(1)),
            pl.BlockSpec((1, d), lambda i: (0, 0)),
            pl.BlockSpec((d, 2 * LANES), lambda i: (0, 0), pipeline_mode=pl.Buffered(1)),
        ],
        out_specs=[
            pl.BlockSpec((tm, d), lambda i: (i, 0)),
            pl.BlockSpec((d // 2 // SC_PIECE, tm, SC_PIECE), lambda i: (0, i, 0)),
            pl.BlockSpec((ne, tm), lambda i: (0, i)),
        ],
        out_shape=[
            jax.ShapeDtypeStruct((n, d), F32),
            jax.ShapeDtypeStruct((d // 2 // SC_PIECE, n, SC_PIECE), U32),
            jax.ShapeDtypeStruct((ne, n), F32),
        ],
        compiler_params=_cparams("parallel"),
    )(merged, x2d, w_o, norm2_g.reshape(1, d), w2)


def _route_kernel(a_ref, idx_ref, gate_ref, *, cap):
    a = a_ref[0]
    nc, wid = a.shape
    keys = lax.bitcast_convert_type(a, I32)

    def search(bit, thr):
        cand = thr | jnp.left_shift(jnp.int32(1), 30 - bit)
        cnt = jnp.sum(jnp.sum((keys >= cand).astype(I32), axis=1, keepdims=True), axis=0, keepdims=True)
        return jnp.where(cnt >= cap, cand, thr)

    thr = lax.fori_loop(0, 31, search, jnp.zeros((1, 1), I32))
    gt = keys > thr
    eq = keys == thr
    n_gt = jnp.sum(jnp.sum(gt.astype(F32), axis=1, keepdims=True), axis=0, keepdims=True)
    need = cap - n_gt

    upper = (lax.broadcasted_iota(I32, (wid, wid), 0) <= lax.broadcasted_iota(I32, (wid, wid), 1)).astype(BF16)
    lower = (lax.broadcasted_iota(I32, (nc, nc), 1) < lax.broadcasted_iota(I32, (nc, nc), 0)).astype(BF16)

    def prefix(mask):
        inc = jnp.dot(mask.astype(BF16), upper, preferred_element_type=F32)
        tot = inc[:, wid - 1:wid]
        off = jnp.dot(lower, jnp.broadcast_to(tot, (nc, LANES)).astype(BF16), preferred_element_type=F32)[:, 0:1]
        return inc, tot, off

    inc_e, _, off_e = prefix(eq)
    tie_rank = off_e + inc_e - eq.astype(F32)
    sel = gt | (eq & (tie_rank < need))
    inc_s, tot_s, off_s = prefix(sel)

    slot = lax.broadcasted_iota(I32, (1, cap), 1).astype(F32)
    chunk_end = off_s + tot_s
    k_row = jnp.sum((chunk_end <= slot).astype(F32), axis=0, keepdims=True)
    onehot = lax.broadcasted_iota(I32, (nc, cap), 0).astype(F32) == k_row
    off_k = jnp.sum(jnp.where(onehot, off_s, 0.0), axis=0, keepdims=True)
    rank = slot - off_k
    onehot_b = onehot.astype(BF16)
    inc_of_slot = jnp.dot(inc_s.T.astype(BF16), onehot_b, preferred_element_type=F32)
    local = jnp.sum((inc_of_slot <= rank).astype(F32), axis=0, keepdims=True)
    idx_ref[0] = (k_row * wid + local).astype(I32)

    at = a.T
    a1 = at.astype(BF16)
    r1 = at - a1.astype(F32)
    a2 = r1.astype(BF16)
    a3 = (r1 - a2.astype(F32)).astype(BF16)
    aff_of_slot = ((jnp.dot(a1, onehot_b, preferred_element_type=F32) + jnp.dot(a2, onehot_b, preferred_element_type=F32))
                   + jnp.dot(a3, onehot_b, preferred_element_type=F32))
    pick = lax.broadcasted_iota(I32, (wid, cap), 0).astype(F32) == local
    gate_ref[0] = jnp.sum(jnp.where(pick, aff_of_slot, 0.0), axis=0, keepdims=True)


def _route(aff_t, cap):
    ne, n = aff_t.shape
    wid = ROUTE_CHUNK
    assert n % wid == 0
    nc = n // wid
    a3 = aff_t.reshape(ne, nc, wid)
    kern = functools.partial(_route_kernel, cap=cap)
    return pl.pallas_call(
        kern,
        grid=(ne,),
        in_specs=[pl.BlockSpec((1, nc, wid), lambda e: (e, 0, 0))],
        out_specs=[pl.BlockSpec((1, 1, cap), lambda e: (e, 0, 0)), pl.BlockSpec((1, 1, cap), lambda e: (e, 0, 0))],
        out_shape=[jax.ShapeDtypeStruct((ne, 1, cap), I32), jax.ShapeDtypeStruct((ne, 1, cap), F32)],
        compiler_params=_cparams("parallel"),
    )(a3)


def _row_copy(src_hbm, tok, buf, r, sem):
    return pltpu.make_async_copy(src_hbm.at[pl.ds(tok, 1)], buf.at[pl.ds(r, 1)], sem)


def _gather_rows_sc(t, idx):
    npieces, n, _ = t.shape
    ne, _, cap = idx.shape
    total = npieces * ne * cap
    piece_idx = (jnp.arange(npieces, dtype=I32)[:, None] * n + idx.reshape(1, -1)).reshape(1, total)
    pieces = t.reshape(npieces * n, SC_PIECE)
    mesh = plsc.VectorSubcoreMesh(core_axis_name="core", subcore_axis_name="subcore")

    @pl.kernel(out_type=jax.ShapeDtypeStruct((total, SC_PIECE), t.dtype), mesh=mesh)
    def gather(x_hbm, i_hbm, o_hbm):
        def body(i_vmem, o_vmem):
            pltpu.sync_copy(x_hbm.at[i_vmem.at[0]], o_vmem)

        pltpu.emit_pipeline(
            body,
            grid=(total // SC_WINDOW,),
            in_specs=[pl.BlockSpec((1, SC_WINDOW), index_map=lambda i: (0, i))],
            out_specs=[pl.BlockSpec((SC_WINDOW, SC_PIECE), index_map=lambda i: (i, 0))],
            core_axis_name=("core", "subcore"),
            dimension_semantics=(pltpu.PARALLEL,),
        )(i_hbm, o_hbm)

    return gather(pieces, piece_idx).reshape(npieces, ne, cap, SC_PIECE)


def _ffn_up_kernel(x_ref, wg_ref, wu_ref, o_ref, xb):
    @pl.when(pl.program_id(2) == 0)
    def _():
        npieces, _, _, piece = x_ref.shape
        half = npieces * piece
        for p in range(npieces):
            words = x_ref[p, 0]
            lo, hi = p * piece, (p + 1) * piece
            xb[:, lo:hi] = lax.bitcast_convert_type(words & jnp.uint32(0xFFFF0000), F32).astype(BF16)
            xb[:, half + lo:half + hi] = lax.bitcast_convert_type(words << 16, F32).astype(BF16)

    x = xb[...]
    a = jnp.dot(x, wg_ref[0].astype(BF16), preferred_element_type=F32)
    b = jnp.dot(x, wu_ref[0].astype(BF16), preferred_element_type=F32)
    o_ref[0] = ((a * _sigmoid(a)) * b).astype(o_ref.dtype)


def _ffn_down_kernel(h_ref, wd_ref, gate_ref, o_ref):
    o_ref[0] = jnp.dot(h_ref[0], wd_ref[0].astype(BF16), preferred_element_type=F32) * gate_ref[0]


def _expert_ffn(xe, w_gate, w_up, w_down, gate_col):
    npieces, ne, cap, piece = xe.shape
    d = 2 * npieces * piece
    dff = w_gate.shape[2]
    tm = min(TM_FFN, cap)
    tf = min(TF_FFN, dff)
    tn = min(TN_FFN, d)
    hid = pl.pallas_call(
        _ffn_up_kernel,
        grid=(ne, cap // tm, dff // tf),
        in_specs=[
            pl.BlockSpec((npieces, 1, tm, piece), lambda e, c, f: (0, e, c, 0)),
            pl.BlockSpec((1, d, tf), lambda e, c, f: (e, 0, f)),
            pl.BlockSpec((1, d, tf), lambda e, c, f: (e, 0, f)),
        ],
        out_specs=pl.BlockSpec((1, tm, tf), lambda e, c, f: (e, c, f)),
        out_shape=jax.ShapeDtypeStruct((ne, cap, dff), BF16),
        scratch_shapes=[pltpu.VMEM((tm, d), BF16)],
        compiler_params=_cparams("parallel", "parallel", "arbitrary"),
    )(xe, w_gate, w_up)
    return pl.pallas_call(
        _ffn_down_kernel,
        grid=(ne, cap // tm, d // tn),
        in_specs=[
            pl.BlockSpec((1, tm, dff), lambda e, c, n: (e, c, 0)),
            pl.BlockSpec((1, dff, tn), lambda e, c, n: (e, 0, n)),
            pl.BlockSpec((1, tm, 1), lambda e, c, n: (e, c, 0)),
        ],
        out_specs=pl.BlockSpec((1, tm, tn), lambda e, c, n: (e, c, n)),
        out_shape=jax.ShapeDtypeStruct((ne, cap, d), F32),
        compiler_params=_cparams("parallel", "parallel", "parallel"),
    )(hid, w_down, gate_col)


def _scatter_kernel(idx_ref, idx_next_ref, ye_ref, h_hbm, y_hbm, buf, sem_in, sem_out, *, rows, steps):
    del h_hbm
    c = pl.program_id(1)
    slot = c % 2

    def fetch(iref, s):
        def body(r, carry):
            _row_copy(y_hbm, iref[0, 0, r], buf.at[s], r, sem_in.at[s]).start()
            return carry
        lax.fori_loop(0, rows, body, 0, unroll=DMA_UNROLL)

    def row_put(tok, s, r):
        return pltpu.make_async_copy(buf.at[s].at[pl.ds(r, 1)], y_hbm.at[pl.ds(tok, 1)], sem_out.at[s])

    def put_done(s):
        def body(r, carry):
            row_put(0, s, r).wait()
            return carry
        lax.fori_loop(0, rows, body, 0, unroll=DMA_UNROLL)

    @pl.when(c == 0)
    def _():
        fetch(idx_ref, 0)

    @pl.when(c >= 1)
    def _():
        put_done(1 - slot)

    @pl.when(c + 1 < steps)
    def _():
        fetch(idx_next_ref, 1 - slot)

    def fetched(r, carry):
        _row_copy(y_hbm, 0, buf.at[slot], r, sem_in.at[slot]).wait()
        return carry

    lax.fori_loop(0, rows, fetched, 0, unroll=DMA_UNROLL)
    buf[slot] = buf[slot] + ye_ref[0]

    def put(r, carry):
        row_put(idx_ref[0, 0, r], slot, r).start()
        return carry

    lax.fori_loop(0, rows, put, 0, unroll=DMA_UNROLL)

    @pl.when(c == steps - 1)
    def _():
        put_done(slot)


def _scatter_add(h, ye, idx):
    n, d = h.shape
    ne, _, cap = idx.shape
    rows = min(R_GATHER, cap)
    steps = cap // rows
    total = ne * steps
    idx3 = idx.reshape(total, 1, rows)
    kern = functools.partial(_scatter_kernel, rows=rows, steps=steps)
    return pl.pallas_call(
        kern,
        grid=(ne, steps),
        in_specs=[
            pl.BlockSpec((1, 1, rows), lambda e, c: (e * steps + c, 0, 0), memory_space=pltpu.SMEM),
            pl.BlockSpec((1, 1, rows), lambda e, c: (jnp.minimum(e * steps + c + 1, total - 1), 0, 0),
                         memory_space=pltpu.SMEM),
            pl.BlockSpec((1, rows, d), lambda e, c: (e, c, 0)),
            pl.BlockSpec(memory_space=pl.ANY),
        ],
        out_specs=pl.BlockSpec(memory_space=pl.ANY),
        out_shape=jax.ShapeDtypeStruct((n, d), F32),
        input_output_aliases={3: 0},
        scratch_shapes=[pltpu.VMEM((2, rows, d), F32), pltpu.SemaphoreType.DMA((2,)), pltpu.SemaphoreType.DMA((2,))],
        compiler_params=_cparams("arbitrary", "arbitrary"),
    )(idx3, idx3, ye, h)


def _prepare_weights(l, norm1_g, w_in, pool_w, pool_scale, w_pool_out, q_norm_g, k_norm_g, rel_bias, lambda_qk, subln_g,
                     w_attn_out, w_o, norm2_g, w_router, w_gate, w_up, w_down):
    head_dim = q_norm_g.shape[1]
    nh = rel_bias.shape[1]
    vdim = subln_g.shape[1]
    pool_width = pool_scale.shape[1]
    qk_width = nh * 2 * head_dim
    attn_width = nh * vdim
    o1, o2, o3 = pool_width, pool_width + 2 * qk_width, pool_width + 2 * qk_width + attn_width
    w = w_in[l]
    reps = qk_width // head_dim
    qk_gain = jnp.concatenate([jnp.tile(q_norm_g[l] * (head_dim ** -0.5) * LOG2E, reps), jnp.tile(k_norm_g[l], reps)])
    return dict(
        head_dim=head_dim, nh=nh, vdim=vdim, pool_width=pool_width, qk_width=qk_width,
        lam_init=0.8 - 0.6 * math.exp(-0.3 * l),
        norm1_g=norm1_g[l],
        w_uqkg=jnp.concatenate([w[:, :o2], w[:, o3:]], axis=1).astype(BF16),
        w_vt=w[:, o2:o3].T.astype(BF16),
        qk_gain=qk_gain.reshape(1, -1).astype(F32),
        pool_w=pool_w[l].astype(BF16), pool_scale=pool_scale[l],
        w_pool_out=w_pool_out[l].astype(BF16), w_attn_out=w_attn_out[l].astype(BF16), w_o=w_o[l].astype(BF16),
        rel_bias=rel_bias, lambda_qk=lambda_qk[l], subln_g=subln_g[l], norm2_g=norm2_g[l],
        w_router=w_router[l],
        w_gate=w_gate[l], w_up=w_up[l], w_down=w_down[l],
    )


def _mix_and_route(x, p, bias_tiles):
    b, s, d = x.shape
    n = b * s
    x2d = x.reshape(n, d)
    xn = _rmsnorm(x2d, p["norm1_g"])
    u, qk, gates = _inproj(xn, p["w_uqkg"], p["qk_gain"], p["head_dim"], p["pool_width"], p["qk_width"])
    vt = _v_transposed(xn, p["w_vt"])
    pm = _pool_mixer(u.reshape(b, s, -1), p["pool_w"], p["pool_scale"]).reshape(n, -1)
    att = _diff_attention(qk.reshape(b, s, -1), vt, bias_tiles, p["rel_bias"], p["lambda_qk"], p["subln_g"],
                          p["head_dim"], p["lam_init"]).reshape(n, -1)
    merged = _merge(pm, att, gates, p["w_pool_out"], p["w_attn_out"])
    h, t, aff_t = _oproj_router(merged, x2d, p["w_o"], p["norm2_g"], p["w_router"])
    ne = aff_t.shape[0]
    cap = max(1, (CAPACITY_FACTOR * n) // ne)
    idx, gate = _route(aff_t, cap)
    return h, _gather_rows_sc(t, idx), idx, gate.reshape(ne, cap, 1)


def _experts(shape, p, h, xe, idx, gate_col):
    ye = _expert_ffn(xe, p["w_gate"], p["w_up"], p["w_down"], gate_col)
    return _scatter_add(h, ye, idx).reshape(shape)


def kernel(x_prompt, x_sample, norm1_g, w_in, pool_w, pool_scale, w_pool_out, q_norm_g, k_norm_g, rel_bias, lambda_qk,
           subln_g, w_attn_out, w_o, norm2_g, w_router, w_gate, w_up, w_down):
    tile = min(T_ATTN, x_prompt.shape[1], x_sample.shape[1])
    bias_tiles = _bias_tiles(rel_bias, tile)
    hp, hs = x_prompt, x_sample
    for l in range(norm1_g.shape[0]):
        p = _prepare_weights(l, norm1_g, w_in, pool_w, pool_scale, w_pool_out, q_norm_g, k_norm_g, rel_bias, lambda_qk,
                             subln_g, w_attn_out, w_o, norm2_g, w_router, w_gate, w_up, w_down)
        routed_p = _mix_and_route(hp, p, bias_tiles)
        routed_s = _mix_and_route(hs, p, bias_tiles)
        hp = _experts(hp.shape, p, *routed_p)
        hs = _experts(hs.shape, p, *routed_s)
    return (hp, hs)
```

```python
import functools
import math

import jax
import jax.numpy as jnp
from jax import lax
from jax.experimental import pallas as pl
from jax.experimental.pallas import tpu as pltpu
from jax.experimental.pallas import tpu_sc as plsc

F32, BF16, I32, U32 = jnp.float32, jnp.bfloat16, jnp.int32, jnp.uint32

EPS = 1e-6
POOL_WINDOWS = (2, 4, 8, 16)
MAX_DISTANCE = 128
CAPACITY_FACTOR = 2
LOG2E = math.log2(math.e)

V7X_VMEM_BYTES = 64 * 1024 * 1024
VMEM_LIMIT_BYTES = V7X_VMEM_BYTES - 8 * 1024 * 1024
LANES = 128
MXU_DIM = 256

TM_NORM = 512
TM_INPROJ = 1024
TN_INPROJ = 1024
TS_POOL = 512
T_ATTN = 512
PAIRS_PER_TRIP = 3
TM_MERGE = 512
TM_OPROJ = 512
ROUTE_CHUNK = 256
R_GATHER = 512
DMA_UNROLL = 8
SC_PIECE = 256
SC_WINDOW = 128
TM_FFN = 2048
TF_FFN = 256
TN_FFN = 512


def _cparams(*sem):
    return pltpu.CompilerParams(dimension_semantics=sem, vmem_limit_bytes=VMEM_LIMIT_BYTES)


def _split_bf16(a):
    hi = a.astype(BF16)
    lo = (a - hi.astype(F32)).astype(BF16)
    return hi, lo


def _rmsnorm_kernel(x_ref, g_ref, o_ref):
    x = x_ref[...]
    ms = jnp.mean(x * x, axis=-1, keepdims=True)
    o_ref[...] = (x * lax.rsqrt(ms + EPS) * g_ref[...]).astype(o_ref.dtype)


def _rmsnorm(x2d, g):
    n, d = x2d.shape
    tm = min(TM_NORM, n)
    return pl.pallas_call(
        _rmsnorm_kernel,
        grid=(n // tm,),
        in_specs=[pl.BlockSpec((tm, d), lambda i: (i, 0)), pl.BlockSpec((1, d), lambda i: (0, 0))],
        out_specs=pl.BlockSpec((tm, d), lambda i: (i, 0)),
        out_shape=jax.ShapeDtypeStruct((n, d), BF16),
        compiler_params=_cparams("parallel"),
    )(x2d, g.reshape(1, d))


def _inproj_kernel(x_ref, w_ref, gain_ref, bd_ref, u_ref, qk_ref, g_ref, *, n_qk_tiles, head_dim):
    j = pl.program_id(1)
    acc = jnp.dot(x_ref[...], w_ref[...], preferred_element_type=F32)

    @pl.when(j == 0)
    def _():
        u_ref[...] = acc

    @pl.when((j >= 1) & (j <= n_qk_tiles))
    def _():
        bd = bd_ref[...]
        for c0 in range(0, acc.shape[1], MXU_DIM):
            a = acc[:, c0:c0 + MXU_DIM]
            hi, lo = _split_bf16(a * a)
            ss = jnp.dot(hi, bd, preferred_element_type=F32) + jnp.dot(lo, bd, preferred_element_type=F32)
            y = a * lax.rsqrt(ss * (1.0 / head_dim) + EPS) * gain_ref[:, c0:c0 + MXU_DIM]
            qk_ref[:, c0:c0 + MXU_DIM] = y.astype(qk_ref.dtype)

    @pl.when(j > n_qk_tiles)
    def _():
        g_ref[...] = acc


def _inproj(xn, w_uqkg, qk_gain, head_dim, pool_width, qk_width):
    n, d = xn.shape
    tn = TN_INPROJ
    assert pool_width == tn and qk_width % tn == 0
    tm = min(TM_INPROJ, n)
    n_qk_tiles = 2 * qk_width // tn
    n_gate_tiles = (w_uqkg.shape[1] - pool_width - 2 * qk_width) // tn
    n_col = 1 + n_qk_tiles + n_gate_tiles
    rows = lax.broadcasted_iota(I32, (MXU_DIM, MXU_DIM), 0) // head_dim
    cols = lax.broadcasted_iota(I32, (MXU_DIM, MXU_DIM), 1) // head_dim
    bd = (rows == cols).astype(BF16)
    kern = functools.partial(_inproj_kernel, n_qk_tiles=n_qk_tiles, head_dim=head_dim)
    return pl.pallas_call(
        kern,
        grid=(n // tm, n_col),
        in_specs=[
            pl.BlockSpec((tm, d), lambda i, j: (i, 0)),
            pl.BlockSpec((d, tn), lambda i, j: (0, j)),
            pl.BlockSpec((1, tn), lambda i, j: (0, jnp.clip(j - 1, 0, n_qk_tiles - 1))),
            pl.BlockSpec((MXU_DIM, MXU_DIM), lambda i, j: (0, 0)),
        ],
        out_specs=[
            pl.BlockSpec((tm, tn), lambda i, j: (i, 0)),
            pl.BlockSpec((tm, tn), lambda i, j: (i, jnp.clip(j - 1, 0, n_qk_tiles - 1))),
            pl.BlockSpec((tm, tn), lambda i, j: (i, jnp.clip(j - 1 - n_qk_tiles, 0, n_gate_tiles - 1))),
        ],
        out_shape=[
            jax.ShapeDtypeStruct((n, pool_width), F32),
            jax.ShapeDtypeStruct((n, 2 * qk_width), BF16),
            jax.ShapeDtypeStruct((n, n_gate_tiles * tn), F32),
        ],
        compiler_params=_cparams("parallel", "arbitrary"),
    )(xn, w_uqkg, qk_gain, bd)


def _vt_kernel(w_ref, x_ref, o_ref):
    o_ref[...] = lax.dot_general(w_ref[...], x_ref[...], (((1,), (1,)), ((), ())),
                                 preferred_element_type=F32).astype(o_ref.dtype)


def _v_transposed(xn, w_vt):
    n, d = xn.shape
    aw = w_vt.shape[0]
    tm = min(TM_INPROJ, n)
    return pl.pallas_call(
        _vt_kernel,
        grid=(n // tm,),
        in_specs=[pl.BlockSpec((aw, d), lambda i: (0, 0)), pl.BlockSpec((tm, d), lambda i: (i, 0))],
        out_specs=pl.BlockSpec((aw, tm), lambda i: (0, i)),
        out_shape=jax.ShapeDtypeStruct((aw, n), BF16),
        compiler_params=_cparams("parallel"),
    )(w_vt, xn)


def _pool_kernel(cur_ref, prev_ref, next_ref, pw_ref, ps_ref, o_ref, ext, *, seq, ts, group):
    i = pl.program_id(1)
    halo = prev_ref.shape[1]
    n = ts + 2 * halo
    cur = cur_ref[0]
    ext[0:halo, :] = jnp.where(i > 0, prev_ref[0], 0.0)
    ext[halo:halo + ts, :] = cur
    ext[halo + ts:, :] = jnp.where(i < pl.num_programs(1) - 1, next_ref[0], 0.0)
    t_col = i * ts + lax.broadcasted_iota(I32, (ts, 1), 0)
    for gi, w in enumerate(POOL_WINDOWS):
        cnt = (jnp.minimum(t_col + (w - w // 2), seq) - jnp.maximum(t_col - w // 2, 0)).astype(F32)
        ch = slice(gi * group, (gi + 1) * group)
        a = ext[:, ch]
        span = 1
        while span < w:
            a = a + pltpu.roll(a, n - span, axis=0)
            span *= 2
        wsum = pltpu.roll(a, w // 2, axis=0)[halo:halo + ts]
        pooled = wsum / cnt - cur[:, ch]
        y = jnp.dot(pooled.astype(BF16), pw_ref[gi], preferred_element_type=F32) * ps_ref[:, ch]
        o_ref[0, :, ch] = y.astype(o_ref.dtype)


def _pool_mixer(u3, pool_w, pool_scale):
    b, s, pwid = u3.shape
    ng, group, _ = pool_w.shape
    assert ng == len(POOL_WINDOWS) and ng * group == pwid
    ts = min(TS_POOL, s)
    halo = 16
    assert all(w & (w - 1) == 0 for w in POOL_WINDOWS)
    assert max(POOL_WINDOWS) <= halo and s % ts == 0 and ts % halo == 0
    nblk = ts // halo
    last = s // halo - 1
    kern = functools.partial(_pool_kernel, seq=s, ts=ts, group=group)
    return pl.pallas_call(
        kern,
        grid=(b, s // ts),
        in_specs=[
            pl.BlockSpec((1, ts, pwid), lambda bi, i: (bi, i, 0)),
            pl.BlockSpec((1, halo, pwid), lambda bi, i: (bi, jnp.maximum(i * nblk - 1, 0), 0)),
            pl.BlockSpec((1, halo, pwid), lambda bi, i: (bi, jnp.minimum((i + 1) * nblk, last), 0)),
            pl.BlockSpec((ng, group, group), lambda bi, i: (0, 0, 0)),
            pl.BlockSpec((1, pwid), lambda bi, i: (0, 0)),
        ],
        out_specs=pl.BlockSpec((1, ts, pwid), lambda bi, i: (bi, i, 0)),
        out_shape=jax.ShapeDtypeStruct((b, s, pwid), BF16),
        scratch_shapes=[pltpu.VMEM((ts + 2 * halo, pwid), F32)],
        compiler_params=_cparams("parallel", "parallel"),
    )(u3, u3, u3, pool_w, pool_scale.reshape(1, pwid))


def _bucket_thresholds(num_buckets):
    half = num_buckets // 2
    max_exact = half // 2
    ratio = MAX_DISTANCE // max_exact
    assert ratio * max_exact == MAX_DISTANCE
    steps = half - max_exact
    thr = []
    for jj in range(1, steps):
        n = max_exact
        while n ** steps < (max_exact ** steps) * (ratio ** jj):
            n += 1
        thr.append(n)
    return half, max_exact, thr


def _bias_kernel(tab_ref, o_ref, *, tile, num_buckets):
    h = pl.program_id(0)
    delta = pl.program_id(1) - 2
    half, max_exact, thr = _bucket_thresholds(num_buckets)
    kk = lax.broadcasted_iota(I32, (tile, tile), 0)
    qq = lax.broadcasted_iota(I32, (tile, tile), 1)
    rel = delta * tile + kk - qq
    n = jnp.abs(rel)
    large = jnp.full((tile, tile), max_exact, I32)
    for th in thr:
        large = large + (n >= th).astype(I32)
    bucket = jnp.where(n < max_exact, n, large) + jnp.where(rel > 0, half, 0)
    out = jnp.zeros((tile, tile), F32)
    for bkt in range(num_buckets):
        out = jnp.where(bucket == bkt, tab_ref[bkt, h], out)
    o_ref[0, 0] = out * LOG2E


def _bias_tiles(rel_bias, tile):
    nb, nh = rel_bias.shape
    assert tile >= MAX_DISTANCE
    kern = functools.partial(_bias_kernel, tile=tile, num_buckets=nb)
    return pl.pallas_call(
        kern,
        grid=(nh, 5),
        in_specs=[pl.BlockSpec(memory_space=pltpu.SMEM)],
        out_specs=pl.BlockSpec((1, 1, tile, tile), lambda h, dd: (h, dd, 0, 0)),
        out_shape=jax.ShapeDtypeStruct((nh, 5, tile, tile), F32),
        compiler_params=_cparams("parallel", "parallel"),
    )(rel_bias)


def _attn_kernel(tab_ref, q_ref, k_ref, vt_ref, b_ref, lqk_ref, sg_ref, o_ref,
                 s_sc, mb_sc, off_sc, m_sc, acc_sc, *, head_dim, lam_init, nblk):
    h = pl.program_id(1)
    i = pl.program_id(2)
    tile = q_ref.shape[1]
    vdim = vt_ref.shape[0]
    nsteps = nblk // 2
    q = q_ref[0]
    lane = lax.broadcasted_iota(I32, (tile, q.shape[1]), 1)
    ones = jnp.ones((acc_sc.shape[1] - vdim, tile), BF16)
    nb = tab_ref.shape[0]

    def start(blk):
        return blk * tile if isinstance(blk, int) else pl.multiple_of(blk * tile, tile)

    def scores(slot, blk, far):
        k = k_ref[0, pl.ds(start(blk), tile), :]
        if far:
            off = jnp.where(blk < i, tab_ref[nb // 2 - 1, h], tab_ref[nb - 1, h]) * LOG2E
        else:
            off = 0.0
            bias = b_ref[0, jnp.clip(blk - i, -2, 2) + 2]
        off_sc[slot] = jnp.full((1, tile), off, F32)
        for c in range(2):
            kc = jnp.where((lane >= c * head_dim) & (lane < (c + 1) * head_dim), k, jnp.zeros_like(k))
            s = lax.dot_general(kc, q, (((1,), (1,)), ((), ())), preferred_element_type=F32)
            if not far:
                s = s + bias
            s_sc[slot, c] = s
            mb_sc[slot, c] = jnp.max(s, axis=0, keepdims=True) + off

    def consume(slot, blk):
        vt = vt_ref[:, pl.ds(start(blk), tile)]
        vt_ones = jnp.concatenate([vt, ones], axis=0)
        for c in range(2):
            m_prev = m_sc[c]
            m_new = jnp.maximum(m_prev, mb_sc[slot, c])
            alpha = jnp.exp2(m_prev - m_new)
            p = jnp.exp2(s_sc[slot, c] - (m_new - off_sc[slot]))
            acc_sc[c] = alpha * acc_sc[c] + jnp.dot(vt_ones, p.astype(BF16), preferred_element_type=F32)
            m_sc[c] = m_new

    m_sc[...] = jnp.full(m_sc.shape, -1e30, F32)
    acc_sc[...] = jnp.zeros(acc_sc.shape, F32)
    scores(0, 0, False)

    def pairs(j0, npairs):
        first = 2 * j0 + 1
        last = 2 * (j0 + npairs)
        all_far = (first - i >= 2) | (i - last >= 2)

        def run(far):
            for jj in range(npairs):
                j = j0 + jj
                scores(1, 2 * j + 1, far)
                consume(0, 2 * j)
                scores(0, 2 * j + 2, far)
                consume(1, 2 * j + 1)

        @pl.when(all_far)
        def _():
            run(True)

        @pl.when(jnp.logical_not(all_far))
        def _():
            run(False)

    group = PAIRS_PER_TRIP if nsteps - 1 >= 2 * PAIRS_PER_TRIP else min(2, PAIRS_PER_TRIP)
    ngroups = (nsteps - 1) // group

    def trip(g, carry):
        pairs(g * group, group)
        return carry

    lax.fori_loop(0, ngroups, trip, 0)
    if (nsteps - 1) % group:
        pairs(ngroups * group, (nsteps - 1) % group)
    scores(1, nblk - 1, False)
    consume(0, nblk - 2)
    consume(1, nblk - 1)

    lq = lqk_ref[...]
    lam = (jnp.exp(jnp.sum(lq[0:1] * lq[1:2], axis=1, keepdims=True))
           - jnp.exp(jnp.sum(lq[2:3] * lq[3:4], axis=1, keepdims=True)) + lam_init)
    a0 = acc_sc[0]
    a1 = acc_sc[1]
    o = a0[:vdim] / a0[vdim:vdim + 1] - lam * (a1[:vdim] / a1[vdim:vdim + 1])
    ms = jnp.mean(o * o, axis=0, keepdims=True)
    y = o * lax.rsqrt(ms + EPS) * sg_ref[...] * (1.0 - lam_init)
    o_ref[0] = y.T.astype(o_ref.dtype)


def _diff_attention(qk3, vt, bias_tiles, rel_bias, lambda_qk, subln_g, head_dim, lam_init):
    b, s, two_qk = qk3.shape
    nh = bias_tiles.shape[0]
    tile = bias_tiles.shape[2]
    vdim = vt.shape[0] // nh
    assert vdim == 2 * head_dim == LANES and two_qk == 2 * nh * LANES and s % (2 * tile) == 0
    nblk = s // tile
    ones_rows = 16
    kern = functools.partial(_attn_kernel, head_dim=head_dim, lam_init=lam_init, nblk=nblk)
    return pl.pallas_call(
        kern,
        grid=(b, nh, nblk),
        in_specs=[
            pl.BlockSpec(memory_space=pltpu.SMEM),
            pl.BlockSpec((1, tile, LANES), lambda bi, h, i: (bi, i, h)),
            pl.BlockSpec((1, s, LANES), lambda bi, h, i: (bi, 0, nh + h)),
            pl.BlockSpec((vdim, s), lambda bi, h, i: (h, bi)),
            pl.BlockSpec((1, 5, tile, tile), lambda bi, h, i: (h, 0, 0, 0)),
            pl.BlockSpec(lambda_qk.shape, lambda bi, h, i: (0, 0)),
            pl.BlockSpec((vdim, 1), lambda bi, h, i: (0, 0)),
        ],
        out_specs=pl.BlockSpec((1, tile, vdim), lambda bi, h, i: (bi, i, h)),
        out_shape=jax.ShapeDtypeStruct((b, s, nh * vdim), BF16),
        scratch_shapes=[
            pltpu.VMEM((2, 2, tile, tile), F32),
            pltpu.VMEM((2, 2, 1, tile), F32),
            pltpu.VMEM((2, 1, tile), F32),
            pltpu.VMEM((2, 1, tile), F32),
            pltpu.VMEM((2, vdim + ones_rows, tile), F32),
        ],
        compiler_params=_cparams("parallel", "parallel", "parallel"),
    )(rel_bias, qk3, qk3, vt, bias_tiles, lambda_qk, subln_g.reshape(vdim, 1))


def _sigmoid(x):
    return 1.0 / (1.0 + jnp.exp(-x))


def _merge_kernel(pm_ref, att_ref, gp_ref, ga_ref, wp_ref, wa_ref, o_ref):
    pool_out = jnp.dot(pm_ref[...], wp_ref[...], preferred_element_type=F32)
    attn_out = jnp.dot(att_ref[...], wa_ref[...], preferred_element_type=F32)
    o_ref[...] = (_sigmoid(gp_ref[...]) * pool_out + _sigmoid(ga_ref[...]) * attn_out).astype(o_ref.dtype)


def _merge(pm, att, gates, w_pool_out, w_attn_out):
    n, pwid = pm.shape
    aw = att.shape[1]
    d = w_pool_out.shape[1]
    tm = min(TM_MERGE, n)
    return pl.pallas_call(
        _merge_kernel,
        grid=(n // tm,),
        in_specs=[
            pl.BlockSpec((tm, pwid), lambda i: (i, 0)),
            pl.BlockSpec((tm, aw), lambda i: (i, 0)),
            pl.BlockSpec((tm, d), lambda i: (i, 0)),
            pl.BlockSpec((tm, d), lambda i: (i, 1)),
            pl.BlockSpec((pwid, d), lambda i: (0, 0), pipeline_mode=pl.Buffered(1)),
            pl.BlockSpec((aw, d), lambda i: (0, 0), pipeline_mode=pl.Buffered(1)),
        ],
        out_specs=pl.BlockSpec((tm, d), lambda i: (i, 0)),
        out_shape=jax.ShapeDtypeStruct((n, d), BF16),
        compiler_params=_cparams("parallel"),
    )(pm, att, gates, gates, w_pool_out, w_attn_out)


def _oproj_kernel(m_ref, x_ref, wo_ref, g2_ref, wr_ref, h_ref, t_ref, aff_ref):
    h = x_ref[...] + jnp.dot(m_ref[...], wo_ref[...], preferred_element_type=F32)
    h_ref[...] = h
    ms = jnp.mean(h * h, axis=-1, keepdims=True)
    t = h * lax.rsqrt(ms + EPS) * g2_ref[...]
    half = t.shape[1] // 2
    bits = lax.bitcast_convert_type(t.astype(BF16).astype(F32), U32)
    words = bits[:, :half] | (bits[:, half:] >> 16)
    for p in range(t_ref.shape[0]):
        t_ref[p] = words[:, p * SC_PIECE:(p + 1) * SC_PIECE]
    t_hi, t_lo = _split_bf16(t)
    w2 = wr_ref[...]
    p_hi = jnp.dot(t_hi, w2, preferred_element_type=F32)
    p_lo = jnp.dot(t_lo, w2, preferred_element_type=F32)
    logits = (p_hi[:, :LANES] + p_hi[:, LANES:]) + p_lo[:, :LANES]
    ne = aff_ref.shape[0]
    lt = logits.T[:ne]
    e = jnp.exp(lt - jnp.max(lt, axis=0, keepdims=True))
    aff_ref[...] = e / jnp.sum(e, axis=0, keepdims=True)


def _oproj_router(merged, x2d, w_o, norm2_g, w_router):
    n, d = x2d.shape
    ne = w_router.shape[1]
    assert ne <= LANES
    w_hi, w_lo = _split_bf16(w_router)
    w2 = jnp.zeros((d, 2 * LANES), BF16).at[:, :ne].set(w_hi).at[:, LANES:LANES + ne].set(w_lo)
    tm = min(TM_OPROJ, n)
    return pl.pallas_call(
        _oproj_kernel,
        grid=(n // tm,),
        in_specs=[
            pl.BlockSpec((tm, d), lambda i: (i, 0)),
            pl.BlockSpec((tm, d), lambda i: (i, 0)),
            pl.BlockSpec((d, d), lambda i: (0, 0), pipeline_mode=pl.Buffered(1)),
            pl.BlockSpec((1, d), lambda i: (0, 0)),
            pl.BlockSpec((d, 2 * LANES), lambda i: (0, 0), pipeline_mode=pl.Buffered(1)),
        ],
        out_specs=[
            pl.BlockSpec((tm, d), lambda i: (i, 0)),
            pl.BlockSpec((d // 2 // SC_PIECE, tm, SC_PIECE), lambda i: (0, i, 0)),
            pl.BlockSpec((ne, tm), lambda i: (0, i)),
        ],
        out_shape=[
            jax.ShapeDtypeStruct((n, d), F32),
            jax.ShapeDtypeStruct((d // 2 // SC_PIECE, n, SC_PIECE), U32),
            jax.ShapeDtypeStruct((ne, n), F32),
        ],
        compiler_params=_cparams("parallel"),
    )(merged, x2d, w_o, norm2_g.reshape(1, d), w2)


def _route_kernel(a_ref, idx_ref, gate_ref, *, cap):
    a = a_ref[0]
    nc, wid = a.shape
    keys = lax.bitcast_convert_type(a, I32)

    def search(bit, thr):
        cand = thr | jnp.left_shift(jnp.int32(1), 30 - bit)
        cnt = jnp.sum(jnp.sum((keys >= cand).astype(I32), axis=1, keepdims=True), axis=0, keepdims=True)
        return jnp.where(cnt >= cap, cand, thr)

    thr = lax.fori_loop(0, 31, search, jnp.zeros((1, 1), I32))
    gt = keys > thr
    eq = keys == thr
    n_gt = jnp.sum(jnp.sum(gt.astype(F32), axis=1, keepdims=True), axis=0, keepdims=True)
    need = cap - n_gt

    upper = (lax.broadcasted_iota(I32, (wid, wid), 0) <= lax.broadcasted_iota(I32, (wid, wid), 1)).astype(BF16)
    lower = (lax.broadcasted_iota(I32, (nc, nc), 1) < lax.broadcasted_iota(I32, (nc, nc), 0)).astype(BF16)

    def prefix(mask):
        inc = jnp.dot(mask.astype(BF16), upper, preferred_element_type=F32)
        tot = inc[:, wid - 1:wid]
        off = jnp.dot(lower, jnp.broadcast_to(tot, (nc, LANES)).astype(BF16), preferred_element_type=F32)[:, 0:1]
        return inc, tot, off

    inc_e, _, off_e = prefix(eq)
    tie_rank = off_e + inc_e - eq.astype(F32)
    sel = gt | (eq & (tie_rank < need))
    inc_s, tot_s, off_s = prefix(sel)

    slot = lax.broadcasted_iota(I32, (1, cap), 1).astype(F32)
    chunk_end = off_s + tot_s
    k_row = jnp.sum((chunk_end <= slot).astype(F32), axis=0, keepdims=True)
    onehot = lax.broadcasted_iota(I32, (nc, cap), 0).astype(F32) == k_row
    off_k = jnp.sum(jnp.where(onehot, off_s, 0.0), axis=0, keepdims=True)
    rank = slot - off_k
    onehot_b = onehot.astype(BF16)
    inc_of_slot = jnp.dot(inc_s.T.astype(BF16), onehot_b, preferred_element_type=F32)
    local = jnp.sum((inc_of_slot <= rank).astype(F32), axis=0, keepdims=True)
    idx_ref[0] = (k_row * wid + local).astype(I32)

    at = a.T
    a1 = at.astype(BF16)
    r1 = at - a1.astype(F32)
    a2 = r1.astype(BF16)
    a3 = (r1 - a2.astype(F32)).astype(BF16)
    aff_of_slot = ((jnp.dot(a1, onehot_b, preferred_element_type=F32) + jnp.dot(a2, onehot_b, preferred_element_type=F32))
                   + jnp.dot(a3, onehot_b, preferred_element_type=F32))
    pick = lax.broadcasted_iota(I32, (wid, cap), 0).astype(F32) == local
    gate_ref[0] = jnp.sum(jnp.where(pick, aff_of_slot, 0.0), axis=0, keepdims=True)


def _route(aff_t, cap):
    ne, n = aff_t.shape
    wid = ROUTE_CHUNK
    assert n % wid == 0
    nc = n // wid
    a3 = aff_t.reshape(ne, nc, wid)
    kern = functools.partial(_route_kernel, cap=cap)
    return pl.pallas_call(
        kern,
        grid=(ne,),
        in_specs=[pl.BlockSpec((1, nc, wid), lambda e: (e, 0, 0))],
        out_specs=[pl.BlockSpec((1, 1, cap), lambda e: (e, 0, 0)), pl.BlockSpec((1, 1, cap), lambda e: (e, 0, 0))],
        out_shape=[jax.ShapeDtypeStruct((ne, 1, cap), I32), jax.ShapeDtypeStruct((ne, 1, cap), F32)],
        compiler_params=_cparams("parallel"),
    )(a3)


def _row_copy(src_hbm, tok, buf, r, sem):
    return pltpu.make_async_copy(src_hbm.at[pl.ds(tok, 1)], buf.at[pl.ds(r, 1)], sem)


def _gather_rows_sc(t, idx):
    npieces, n, _ = t.shape
    ne, _, cap = idx.shape
    total = npieces * ne * cap
    piece_idx = (jnp.arange(npieces, dtype=I32)[:, None] * n + idx.reshape(1, -1)).reshape(1, total)
    pieces = t.reshape(npieces * n, SC_PIECE)
    mesh = plsc.VectorSubcoreMesh(core_axis_name="core", subcore_axis_name="subcore")

    @pl.kernel(out_type=jax.ShapeDtypeStruct((total, SC_PIECE), t.dtype), mesh=mesh)
    def gather(x_hbm, i_hbm, o_hbm):
        def body(i_vmem, o_vmem):
            pltpu.sync_copy(x_hbm.at[i_vmem.at[0]], o_vmem)

        pltpu.emit_pipeline(
            body,
            grid=(total // SC_WINDOW,),
            in_specs=[pl.BlockSpec((1, SC_WINDOW), index_map=lambda i: (0, i))],
            out_specs=[pl.BlockSpec((SC_WINDOW, SC_PIECE), index_map=lambda i: (i, 0))],
            core_axis_name=("core", "subcore"),
            dimension_semantics=(pltpu.PARALLEL,),
        )(i_hbm, o_hbm)

    return gather(pieces, piece_idx).reshape(npieces, ne, cap, SC_PIECE)


def _ffn_up_kernel(x_ref, wg_ref, wu_ref, o_ref, xb):
    @pl.when(pl.program_id(2) == 0)
    def _():
        npieces, _, _, piece = x_ref.shape
        half = npieces * piece
        for p in range(npieces):
            words = x_ref[p, 0]
            lo, hi = p * piece, (p + 1) * piece
            xb[:, lo:hi] = lax.bitcast_convert_type(words & jnp.uint32(0xFFFF0000), F32).astype(BF16)
            xb[:, half + lo:half + hi] = lax.bitcast_convert_type(words << 16, F32).astype(BF16)

    x = xb[...]
    a = jnp.dot(x, wg_ref[0].astype(BF16), preferred_element_type=F32)
    b = jnp.dot(x, wu_ref[0].astype(BF16), preferred_element_type=F32)
    o_ref[0] = ((a * _sigmoid(a)) * b).astype(o_ref.dtype)


def _ffn_down_kernel(h_ref, wd_ref, gate_ref, o_ref):
    o_ref[0] = jnp.dot(h_ref[0], wd_ref[0].astype(BF16), preferred_element_type=F32) * gate_ref[0]


def _expert_ffn(xe, w_gate, w_up, w_down, gate_col):
    npieces, ne, cap, piece = xe.shape
    d = 2 * npieces * piece
    dff = w_gate.shape[2]
    tm = min(TM_FFN, cap)
    tf = min(TF_FFN, dff)
    tn = min(TN_FFN, d)
    hid = pl.pallas_call(
        _ffn_up_kernel,
        grid=(ne, cap // tm, dff // tf),
        in_specs=[
            pl.BlockSpec((npieces, 1, tm, piece), lambda e, c, f: (0, e, c, 0)),
            pl.BlockSpec((1, d, tf), lambda e, c, f: (e, 0, f)),
            pl.BlockSpec((1, d, tf), lambda e, c, f: (e, 0, f)),
        ],
        out_specs=pl.BlockSpec((1, tm, tf), lambda e, c, f: (e, c, f)),
        out_shape=jax.ShapeDtypeStruct((ne, cap, dff), BF16),
        scratch_shapes=[pltpu.VMEM((tm, d), BF16)],
        compiler_params=_cparams("parallel", "parallel", "arbitrary"),
    )(xe, w_gate, w_up)
    return pl.pallas_call(
        _ffn_down_kernel,
        grid=(ne, cap // tm, d // tn),
        in_specs=[
            pl.BlockSpec((1, tm, dff), lambda e, c, n: (e, c, 0)),
            pl.BlockSpec((1, dff, tn), lambda e, c, n: (e, 0, n)),
            pl.BlockSpec((1, tm, 1), lambda e, c, n: (e, c, 0)),
        ],
        out_specs=pl.BlockSpec((1, tm, tn), lambda e, c, n: (e, c, n)),
        out_shape=jax.ShapeDtypeStruct((ne, cap, d), F32),
        compiler_params=_cparams("parallel", "parallel", "parallel"),
    )(hid, w_down, gate_col)


def _scatter_kernel(idx_ref, idx_next_ref, ye_ref, h_hbm, y_hbm, buf, sem_in, sem_out, *, rows, steps):
    del h_hbm
    c = pl.program_id(1)
    slot = c % 2

    def fetch(iref, s):
        def body(r, carry):
            _row_copy(y_hbm, iref[0, 0, r], buf.at[s], r, sem_in.at[s]).start()
            return carry
        lax.fori_loop(0, rows, body, 0, unroll=DMA_UNROLL)

    def row_put(tok, s, r):
        return pltpu.make_async_copy(buf.at[s].at[pl.ds(r, 1)], y_hbm.at[pl.ds(tok, 1)], sem_out.at[s])

    def put_done(s):
        def body(r, carry):
            row_put(0, s, r).wait()
            return carry
        lax.fori_loop(0, rows, body, 0, unroll=DMA_UNROLL)

    @pl.when(c == 0)
    def _():
        fetch(idx_ref, 0)

    @pl.when(c >= 1)
    def _():
        put_done(1 - slot)

    @pl.when(c + 1 < steps)
    def _():
        fetch(idx_next_ref, 1 - slot)

    def fetched(r, carry):
        _row_copy(y_hbm, 0, buf.at[slot], r, sem_in.at[slot]).wait()
        return carry

    lax.fori_loop(0, rows, fetched, 0, unroll=DMA_UNROLL)
    buf[slot] = buf[slot] + ye_ref[0]

    def put(r, carry):
        row_put(idx_ref[0, 0, r], slot, r).start()
        return carry

    lax.fori_loop(0, rows, put, 0, unroll=DMA_UNROLL)

    @pl.when(c == steps - 1)
    def _():
        put_done(slot)


def _scatter_add(h, ye, idx):
    n, d = h.shape
    ne, _, cap = idx.shape
    rows = min(R_GATHER, cap)
    steps = cap // rows
    total = ne * steps
    idx3 = idx.reshape(total, 1, rows)
    kern = functools.partial(_scatter_kernel, rows=rows, steps=steps)
    return pl.pallas_call(
        kern,
        grid=(ne, steps),
        in_specs=[
            pl.BlockSpec((1, 1, rows), lambda e, c: (e * steps + c, 0, 0), memory_space=pltpu.SMEM),
            pl.BlockSpec((1, 1, rows), lambda e, c: (jnp.minimum(e * steps + c + 1, total - 1), 0, 0),
                         memory_space=pltpu.SMEM),
            pl.BlockSpec((1, rows, d), lambda e, c: (e, c, 0)),
            pl.BlockSpec(memory_space=pl.ANY),
        ],
        out_specs=pl.BlockSpec(memory_space=pl.ANY),
        out_shape=jax.ShapeDtypeStruct((n, d), F32),
        input_output_aliases={3: 0},
        scratch_shapes=[pltpu.VMEM((2, rows, d), F32), pltpu.SemaphoreType.DMA((2,)), pltpu.SemaphoreType.DMA((2,))],
        compiler_params=_cparams("arbitrary", "arbitrary"),
    )(idx3, idx3, ye, h)


def _prepare_weights(l, norm1_g, w_in, pool_w, pool_scale, w_pool_out, q_norm_g, k_norm_g, rel_bias, lambda_qk, subln_g,
                     w_attn_out, w_o, norm2_g, w_router, w_gate, w_up, w_down):
    head_dim = q_norm_g.shape[1]
    nh = rel_bias.shape[1]
    vdim = subln_g.shape[1]
    pool_width = pool_scale.shape[1]
    qk_width = nh * 2 * head_dim
    attn_width = nh * vdim
    o1, o2, o3 = pool_width, pool_width + 2 * qk_width, pool_width + 2 * qk_width + attn_width
    w = w_in[l]
    reps = qk_width // head_dim
    qk_gain = jnp.concatenate([jnp.tile(q_norm_g[l] * (head_dim ** -0.5) * LOG2E, reps), jnp.tile(k_norm_g[l], reps)])
    return dict(
        head_dim=head_dim, nh=nh, vdim=vdim, pool_width=pool_width, qk_width=qk_width,
        lam_init=0.8 - 0.6 * math.exp(-0.3 * l),
        norm1_g=norm1_g[l],
        w_uqkg=jnp.concatenate([w[:, :o2], w[:, o3:]], axis=1).astype(BF16),
        w_vt=w[:, o2:o3].T.astype(BF16),
        qk_gain=qk_gain.reshape(1, -1).astype(F32),
        pool_w=pool_w[l].astype(BF16), pool_scale=pool_scale[l],
        w_pool_out=w_pool_out[l].astype(BF16), w_attn_out=w_attn_out[l].astype(BF16), w_o=w_o[l].astype(BF16),
        rel_bias=rel_bias, lambda_qk=lambda_qk[l], subln_g=subln_g[l], norm2_g=norm2_g[l],
        w_router=w_router[l],
        w_gate=w_gate[l], w_up=w_up[l], w_down=w_down[l],
    )


def _layer(x, p, bias_tiles):
    b, s, d = x.shape
    n = b * s
    x2d = x.reshape(n, d)
    xn = _rmsnorm(x2d, p["norm1_g"])
    u, qk, gates = _inproj(xn, p["w_uqkg"], p["qk_gain"], p["head_dim"], p["pool_width"], p["qk_width"])
    vt = _v_transposed(xn, p["w_vt"])
    pm = _pool_mixer(u.reshape(b, s, -1), p["pool_w"], p["pool_scale"]).reshape(n, -1)
    att = _diff_attention(qk.reshape(b, s, -1), vt, bias_tiles, p["rel_bias"], p["lambda_qk"], p["subln_g"],
                          p["head_dim"], p["lam_init"]).reshape(n, -1)
    merged = _merge(pm, att, gates, p["w_pool_out"], p["w_attn_out"])
    h, t, aff_t = _oproj_router(merged, x2d, p["w_o"], p["norm2_g"], p["w_router"])
    ne = aff_t.shape[0]
    cap = max(1, (CAPACITY_FACTOR * n) // ne)
    idx, gate = _route(aff_t, cap)
    xe = _gather_rows_sc(t, idx)
    ye = _expert_ffn(xe, p["w_gate"], p["w_up"], p["w_down"], gate.reshape(ne, cap, 1))
    y = _scatter_add(h, ye, idx)
    return y.reshape(b, s, d)


def kernel(x_prompt, x_sample, norm1_g, w_in, pool_w, pool_scale, w_pool_out, q_norm_g, k_norm_g, rel_bias, lambda_qk,
           subln_g, w_attn_out, w_o, norm2_g, w_router, w_gate, w_up, w_down):
    tile = min(T_ATTN, x_prompt.shape[1], x_sample.shape[1])
    bias_tiles = _bias_tiles(rel_bias, tile)
    hp, hs = x_prompt, x_sample
    for l in range(norm1_g.shape[0]):
        p = _prepare_weights(l, norm1_g, w_in, pool_w, pool_scale, w_pool_out, q_norm_g, k_norm_g, rel_bias, lambda_qk,
                             subln_g, w_attn_out, w_o, norm2_g, w_router, w_gate, w_up, w_down)
        hp = _layer(hp, p, bias_tiles)
        hs = _layer(hs, p, bias_tiles)
    return (hp, hs)
```
